```python
import jax, jax.numpy as jnp
from jax import lax
import numpy as np

D_MODEL = 2048
BATCH = 8
SEQ = 4096
DEPTH = 4

HEAD_DIM = 128
BLOCK = 128
EPS = 1e-6
ROPE_THETA = 500000.0
PARTIAL_ROPE = HEAD_DIM // 4
FOX_HEADS = (D_MODEL // 4) // HEAD_DIM
FOX_W = FOX_HEADS * HEAD_DIM
FORGET_BIAS_CENTER = 2.0
MLA_HEADS = (D_MODEL // 2) // HEAD_DIM
MLA_Q_RANK = D_MODEL // 4
MLA_KV_RANK = D_MODEL // 4
MLA_NOPE = 128
MLA_ROPE = 64
MLA_V = HEAD_DIM
MLA_W = MLA_HEADS * MLA_V
DIL_HEADS = (D_MODEL // 4) // HEAD_DIM
DIL_W = DIL_HEADS * HEAD_DIM
DIL_BRANCHES = ((128, 1), (512, 4), (2048, 16))
MIX_W = FOX_W + MLA_W + DIL_W
IN_SIZES = (FOX_W, FOX_W, FOX_W, FOX_HEADS,
            MLA_Q_RANK, MLA_KV_RANK, MLA_ROPE,
            DIL_W, DIL_W, DIL_W)
IN_W = sum(IN_SIZES)
D_FF = 5632

kernel_name = "hybrid_fox_mla_dilated_macaron"


def rms_norm(x, g):
    xf = x.astype(jnp.float32)
    y = xf * lax.rsqrt(jnp.mean(xf * xf, axis=-1, keepdims=True) + EPS)
    return (y * g.astype(jnp.float32)).astype(x.dtype)


def swiglu(h, w_gate, w_up, w_down):
    return (jax.nn.silu(h @ w_gate) * (h @ w_up)) @ w_down


def rope_tables(seq, dim):
    inv = 1.0 / (ROPE_THETA ** (jnp.arange(0, dim, 2, dtype=jnp.float32) / dim))
    ang = jnp.arange(seq, dtype=jnp.float32)[:, None] * inv[None, :]
    return jnp.cos(ang), jnp.sin(ang)


def apply_rope(x, cos, sin):
    x1, x2 = jnp.split(x, 2, axis=-1)
    c, s = cos.astype(x.dtype), sin.astype(x.dtype)
    return jnp.concatenate([x1 * c - x2 * s, x1 * s + x2 * c], axis=-1)


def partial_rope(x, cos, sin):
    return jnp.concatenate([apply_rope(x[..., :PARTIAL_ROPE], cos, sin), x[..., PARTIAL_ROPE:]], axis=-1)


def to_heads(t, n_heads):
    b, s, _ = t.shape
    return t.reshape(b, s, n_heads, -1).transpose(0, 2, 1, 3)


def merge_heads(t):
    b, h, s, d = t.shape
    return t.transpose(0, 2, 1, 3).reshape(b, s, h * d)


def causal_block_attention(q, k, v, scale, cum_log_f=None):
    b, h, s_len, _ = q.shape
    nb = s_len // BLOCK
    kpos = jnp.arange(s_len)
    xs = [jnp.arange(nb), q.reshape(b, h, nb, BLOCK, -1).transpose(2, 0, 1, 3, 4)]
    if cum_log_f is not None:
        xs.append(cum_log_f.reshape(b, h, nb, BLOCK).transpose(2, 0, 1, 3))

    def attend(blk):
        i, qi = blk[0], blk[1]
        sc = jnp.einsum("bhqd,bhkd->bhqk", qi, k, preferred_element_type=jnp.float32) * scale
        if cum_log_f is not None:
            sc = sc + (blk[2][..., :, None] - cum_log_f[..., None, :])
        qpos = i * BLOCK + jnp.arange(BLOCK)
        sc = jnp.where(kpos[None, :] <= qpos[:, None], sc, -jnp.inf)
        p = jax.nn.softmax(sc, axis=-1)
        return jnp.einsum("bhqk,bhkd->bhqd", p.astype(v.dtype), v)

    out = lax.map(attend, tuple(xs))
    return out.transpose(1, 2, 0, 3, 4).reshape(b, h, s_len, -1)


def dilated_branch(q, k, v, window, dilation):
    b, h, s_len, hd = q.shape
    L = s_len // dilation
    n_back = window // dilation
    Lp = -(-L // BLOCK) * BLOCK
    nb = Lp // BLOCK

    def to_blocks(t):
        t = t.reshape(b, h, L, dilation, hd).transpose(0, 1, 3, 2, 4)
        t = jnp.pad(t, ((0, 0), (0, 0), (0, 0), (0, Lp - L), (0, 0)))
        return t.reshape(b, h, dilation, nb, BLOCK, hd)

    def with_prev(t):
        prev = jnp.pad(t, ((0, 0), (0, 0), (0, 0), (1, 0), (0, 0), (0, 0)))[:, :, :, :-1]
        return jnp.concatenate([prev, t], axis=4)

    qb = to_blocks(q)
    kc = with_prev(to_blocks(k))
    vc = with_prev(to_blocks(v))
    sc = jnp.einsum("bhrnqd,bhrnkd->bhrnqk", qb, kc, preferred_element_type=jnp.float32) * (hd ** -0.5)
    kidx = jnp.arange(2 * BLOCK)
    dist = (BLOCK + jnp.arange(BLOCK))[:, None] - kidx[None, :]
    band = (dist >= 0) & (dist <= n_back)
    has_prev = (jnp.arange(nb)[:, None, None] > 0) | (kidx[None, None, :] >= BLOCK)
    sc = jnp.where(band[None] & has_prev, sc, -jnp.inf)
    m = jnp.max(sc, axis=-1, keepdims=True)
    e = jnp.exp(sc - m)
    l = jnp.sum(e, axis=-1, keepdims=True)
    o = jnp.einsum("bhrnqk,bhrnkd->bhrnqd", (e / l).astype(v.dtype), vc)
    lse = (m + jnp.log(l))[..., 0]
    o = o.reshape(b, h, dilation, Lp, hd)[:, :, :, :L].transpose(0, 1, 3, 2, 4).reshape(b, h, s_len, hd)
    lse = lse.reshape(b, h, dilation, Lp)[..., :L].transpose(0, 1, 3, 2).reshape(b, h, s_len)
    return o, lse


def dilated_mixture(q, k, v):
    outs, lses = [], []
    for window, dilation in DIL_BRANCHES:
        o, lse = dilated_branch(q, k, v, window, dilation)
        outs.append(o)
        lses.append(lse)
    wts = jax.nn.softmax(jnp.stack(lses, axis=0), axis=0)
    return jnp.sum(wts[..., None].astype(q.dtype) * jnp.stack(outs, axis=0), axis=0)


def setup_inputs(seed: int = 0) -> dict:
    key = jax.random.key(seed)
    ks = iter(jax.random.split(key, 32))

    def dense(shape, fan_in):
        return jax.random.normal(next(ks), shape, jnp.float32) * (fan_in ** -0.5)

    def gain(shape):
        return 1.0 + 0.02 * jax.random.normal(next(ks), shape, jnp.float32)

    x = jax.random.normal(next(ks), (BATCH, SEQ, D_MODEL), jnp.float32)
    return {
        "x": x,
        "ffn1_norm": gain((DEPTH, D_MODEL)),
        "ffn1_w_gate": dense((DEPTH, D_MODEL, D_FF), D_MODEL),
        "ffn1_w_up": dense((DEPTH, D_MODEL, D_FF), D_MODEL),
        "ffn1_w_down": dense((DEPTH, D_FF, D_MODEL), D_FF),
        "mix_norm": gain((DEPTH, D_MODEL)),
        "w_in": dense((DEPTH, D_MODEL, IN_W), D_MODEL),
        "fox_forget_bias": FORGET_BIAS_CENTER + 0.5 * jax.random.normal(next(ks), (DEPTH, FOX_HEADS), jnp.float32),
        "mla_q_norm": gain((DEPTH, MLA_Q_RANK)),
        "mla_kv_norm": gain((DEPTH, MLA_KV_RANK)),
        "mla_w_uq": dense((DEPTH, MLA_Q_RANK, MLA_HEADS * (MLA_NOPE + MLA_ROPE)), MLA_Q_RANK),
        "mla_w_ukv": dense((DEPTH, MLA_KV_RANK, MLA_HEADS * (MLA_NOPE + MLA_V)), MLA_KV_RANK),
        "w_out": dense((DEPTH, MIX_W, D_MODEL), MIX_W),
        "ffn2_norm": gain((DEPTH, D_MODEL)),
        "ffn2_w_gate": dense((DEPTH, D_MODEL, D_FF), D_MODEL),
        "ffn2_w_up": dense((DEPTH, D_MODEL, D_FF), D_MODEL),
        "ffn2_w_down": dense((DEPTH, D_FF, D_MODEL), D_FF),
        "final_norm": gain((D_MODEL,)),
    }


def reference(x, ffn1_norm, ffn1_w_gate, ffn1_w_up, ffn1_w_down, mix_norm, w_in, fox_forget_bias,
              mla_q_norm, mla_kv_norm, mla_w_uq, mla_w_ukv, w_out, ffn2_norm, ffn2_w_gate, ffn2_w_up,
              ffn2_w_down, final_norm):
    b, s_len, _ = x.shape
    cos_p, sin_p = rope_tables(s_len, PARTIAL_ROPE)
    cos_m, sin_m = rope_tables(s_len, MLA_ROPE)
    offsets = []
    acc = 0
    for size in IN_SIZES[:-1]:
        acc += size
        offsets.append(acc)

    for l in range(DEPTH):
        x = x + 0.5 * swiglu(rms_norm(x, ffn1_norm[l]), ffn1_w_gate[l], ffn1_w_up[l], ffn1_w_down[l])

        h = rms_norm(x, mix_norm[l])
        proj = h @ w_in[l]
        fq, fk, fv, f_logit, c_q, c_kv, k_r, dq, dk, dv = jnp.split(proj, offsets, axis=-1)

        log_f = jax.nn.log_sigmoid((f_logit + fox_forget_bias[l]).astype(jnp.float32))
        cum = jnp.cumsum(log_f, axis=1).transpose(0, 2, 1)
        out_a = causal_block_attention(to_heads(fq, FOX_HEADS), to_heads(fk, FOX_HEADS),
                                       to_heads(fv, FOX_HEADS), HEAD_DIM ** -0.5, cum)

        q_b = to_heads(rms_norm(c_q, mla_q_norm[l]) @ mla_w_uq[l], MLA_HEADS)
        q_b = jnp.concatenate([q_b[..., :MLA_NOPE], apply_rope(q_b[..., MLA_NOPE:], cos_m, sin_m)], axis=-1)
        kv_b = to_heads(rms_norm(c_kv, mla_kv_norm[l]) @ mla_w_ukv[l], MLA_HEADS)
        k_rope = apply_rope(k_r[:, None], cos_m, sin_m)
        k_b = jnp.concatenate([kv_b[..., :MLA_NOPE],
                               jnp.broadcast_to(k_rope, (b, MLA_HEADS, s_len, MLA_ROPE))], axis=-1)
        out_b = causal_block_attention(q_b, k_b, kv_b[..., MLA_NOPE:], (MLA_NOPE + MLA_ROPE) ** -0.5)

        out_c = dilated_mixture(partial_rope(to_heads(dq, DIL_HEADS), cos_p, sin_p),
                                partial_rope(to_heads(dk, DIL_HEADS), cos_p, sin_p),
                                to_heads(dv, DIL_HEADS))

        mixed = jnp.concatenate([merge_heads(out_a), merge_heads(out_b), merge_heads(out_c)], axis=-1)
        x = x + mixed @ w_out[l]

        x = x + 0.5 * swiglu(rms_norm(x, ffn2_norm[l]), ffn2_w_gate[l], ffn2_w_up[l], ffn2_w_down[l])

    return rms_norm(x, final_norm)
```

```python
import functools

import jax
import jax.numpy as jnp
from jax import lax
from jax.experimental import pallas as pl
from jax.experimental.pallas import tpu as pltpu

F32 = jnp.float32
BF16 = jnp.bfloat16

LANES = 128
HEAD_DIM = 128
BLOCK = 128
EPS = 1e-6
ROPE_THETA = 500000.0
PARTIAL_ROPE = HEAD_DIM // 4
FOX_HEADS = 4
MLA_HEADS = 8
MLA_Q_RANK = 512
MLA_KV_RANK = 512
MLA_NOPE = 128
MLA_ROPE = 64
DIL_HEADS = 4
DIL_BRANCHES = ((128, 1), (512, 4), (2048, 16))
NEG = -1e30
VMEM_LIMIT = 56 * 1024 * 1024


def _params(*sem):
    return pltpu.CompilerParams(dimension_semantics=sem, vmem_limit_bytes=VMEM_LIMIT)


def _rms(x, g):
    return x * lax.rsqrt(jnp.mean(x * x, axis=-1, keepdims=True) + EPS) * g


def _apply_rope(x, cos, s_hi, s_lo, half):
    return x * cos + pltpu.roll(x, LANES - half, 1) * s_hi + pltpu.roll(x, half, 1) * s_lo


def _ffn_kernel(*refs, emit_norm, final_norm):
    x_ref, g_ref, wg_ref, wu_ref, wd_ref = refs[:5]
    pos = 5
    g2_ref = None
    if emit_norm or final_norm:
        g2_ref = refs[pos]
        pos += 1
    o_ref = refs[pos]
    pos += 1
    hn_ref = None
    if emit_norm:
        hn_ref = refs[pos]
        pos += 1
    xn_ref = refs[pos]

    j = pl.program_id(1)
    nj = pl.num_programs(1)

    @pl.when(j == 0)
    def _():
        xn_ref[...] = _rms(x_ref[...], g_ref[...]).astype(BF16)

    xn = xn_ref[...]
    gate = jnp.dot(xn, wg_ref[...], preferred_element_type=F32)
    up = jnp.dot(xn, wu_ref[...], preferred_element_type=F32)
    h = (gate * jax.nn.sigmoid(gate) * up).astype(BF16)
    d = 0.5 * jnp.dot(h, wd_ref[...], preferred_element_type=F32)

    @pl.when(j == 0)
    def _():
        o_ref[...] = x_ref[...] + d

    @pl.when(j > 0)
    def _():
        o_ref[...] += d

    if emit_norm or final_norm:
        @pl.when(j == nj - 1)
        def _():
            y = _rms(o_ref[...], g2_ref[...])
            if emit_norm:
                hn_ref[...] = y.astype(BF16)
            else:
                o_ref[...] = y


def _ffn(x, g, wg, wu, wd, g2=None, *, emit_norm=False, final_norm=False, tm=512, tf=512):
    t, d = x.shape
    f = wg.shape[1]
    tm = min(tm, t)
    in_specs = [
        pl.BlockSpec((tm, d), lambda i, j: (i, 0)),
        pl.BlockSpec((1, d), lambda i, j: (0, 0)),
        pl.BlockSpec((d, tf), lambda i, j: (0, j)),
        pl.BlockSpec((d, tf), lambda i, j: (0, j)),
        pl.BlockSpec((tf, d), lambda i, j: (j, 0)),
    ]
    args = [x, g.reshape(1, d), wg, wu, wd]
    if emit_norm or final_norm:
        in_specs.append(pl.BlockSpec((1, d), lambda i, j: (0, 0)))
        args.append(g2.reshape(1, d))
    out_shape = [jax.ShapeDtypeStruct((t, d), F32)]
    out_specs = [pl.BlockSpec((tm, d), lambda i, j: (i, 0))]
    if emit_norm:
        out_shape.append(jax.ShapeDtypeStruct((t, d), BF16))
        out_specs.append(pl.BlockSpec((tm, d), lambda i, j: (i, 0)))
    res = pl.pallas_call(
        functools.partial(_ffn_kernel, emit_norm=emit_norm, final_norm=final_norm),
        grid=(t // tm, f // tf),
        in_specs=in_specs,
        out_specs=out_specs,
        out_shape=out_shape,
        scratch_shapes=[pltpu.VMEM((tm, d), BF16)],
        compiler_params=_params("parallel", "arbitrary"),
        name="ffn",
    )(*args)
    return res if emit_norm else res[0]


def _fox_proj_kernel(hn_ref, w_ref, wl_ref, b_ref, qkv_ref, lf_ref):
    hn = hn_ref[...]
    qkv_ref[...] = jnp.dot(hn, w_ref[...], preferred_element_type=F32).astype(BF16)
    z = jnp.dot(hn, wl_ref[...], preferred_element_type=F32) + b_ref[...]
    lf_ref[...] = jnp.minimum(z, 0.0) - jnp.log1p(jnp.exp(-jnp.abs(z)))


def _fox_proj(hn, w, wl, bias, *, tm=512):
    t, d = hn.shape
    tm = min(tm, t)
    n = w.shape[1]
    return pl.pallas_call(
        _fox_proj_kernel,
        grid=(t // tm,),
        in_specs=[
            pl.BlockSpec((tm, d), lambda i: (i, 0)),
            pl.BlockSpec((d, n), lambda i: (0, 0)),
            pl.BlockSpec((d, LANES), lambda i: (0, 0)),
            pl.BlockSpec((1, LANES), lambda i: (0, 0)),
        ],
        out_specs=[
            pl.BlockSpec((tm, n), lambda i: (i, 0)),
            pl.BlockSpec((tm, LANES), lambda i: (i, 0)),
        ],
        out_shape=[
            jax.ShapeDtypeStruct((t, n), BF16),
            jax.ShapeDtypeStruct((t, LANES), F32),
        ],
        compiler_params=_params("parallel"),
        name="fox_proj",
    )(hn, w, wl, bias)


def _mla_proj_kernel(hn_ref, wc_ref, gq_ref, gkv_ref, wuq_ref, wk_ref, wv_ref, tab_ref,
                     q_ref, k_ref, v_ref):
    hn = hn_ref[...]
    c = jnp.dot(hn, wc_ref[...], preferred_element_type=F32)
    cqn = _rms(c[:, :MLA_Q_RANK], gq_ref[...]).astype(BF16)
    ckvn = _rms(c[:, MLA_Q_RANK:MLA_Q_RANK + MLA_KV_RANK], gkv_ref[...]).astype(BF16)
    cos, s_hi, s_lo = tab_ref[0], tab_ref[1], tab_ref[2]
    half = MLA_ROPE // 2
    kr = _apply_rope(c[:, MLA_Q_RANK + MLA_KV_RANK:], cos, s_hi, s_lo, half).astype(BF16)
    q = jnp.dot(cqn, wuq_ref[...], preferred_element_type=F32)
    kn = jnp.dot(ckvn, wk_ref[...], preferred_element_type=F32)
    v_ref[...] = jnp.dot(ckvn, wv_ref[...], preferred_element_type=F32).astype(BF16)
    for h in range(MLA_HEADS):
        lo = 2 * h * LANES
        q_ref[:, lo:lo + LANES] = q[:, lo:lo + LANES].astype(BF16)
        q_ref[:, lo + LANES:lo + 2 * LANES] = _apply_rope(
            q[:, lo + LANES:lo + 2 * LANES], cos, s_hi, s_lo, half).astype(BF16)
        k_ref[:, lo:lo + LANES] = kn[:, h * LANES:(h + 1) * LANES].astype(BF16)
        k_ref[:, lo + LANES:lo + 2 * LANES] = kr


def _mla_proj(hn, wc, gq, gkv, wuq, wk, wv, tab, *, seq, tm=512):
    t, d = hn.shape
    tm = min(tm, seq)
    ns = seq // tm
    nq = wuq.shape[1]
    nv = wv.shape[1]
    full = lambda a: pl.BlockSpec(a.shape, lambda i: (0,) * a.ndim)
    return pl.pallas_call(
        _mla_proj_kernel,
        grid=(t // tm,),
        in_specs=[
            pl.BlockSpec((tm, d), lambda i: (i, 0)),
            full(wc), full(gq), full(gkv), full(wuq), full(wk), full(wv),
            pl.BlockSpec((3, tm, LANES), lambda i: (0, i % ns, 0)),
        ],
        out_specs=[
            pl.BlockSpec((tm, nq), lambda i: (i, 0)),
            pl.BlockSpec((tm, nq), lambda i: (i, 0)),
            pl.BlockSpec((tm, nv), lambda i: (i, 0)),
        ],
        out_shape=[
            jax.ShapeDtypeStruct((t, nq), BF16),
            jax.ShapeDtypeStruct((t, nq), BF16),
            jax.ShapeDtypeStruct((t, nv), BF16),
        ],
        compiler_params=_params("parallel"),
        name="mla_proj",
    )(hn, wc, gq, gkv, wuq, wk, wv, tab)


def _dil_proj_kernel(hn_ref, w_ref, tab_ref, o_ref):
    y = jnp.dot(hn_ref[...], w_ref[...], preferred_element_type=F32)
    cos, s_hi, s_lo = tab_ref[0], tab_ref[1], tab_ref[2]
    half = PARTIAL_ROPE // 2
    n_rot = 2 * DIL_HEADS
    for h in range(n_rot):
        sl = slice(h * LANES, (h + 1) * LANES)
        o_ref[:, sl] = _apply_rope(y[:, sl], cos, s_hi, s_lo, half).astype(BF16)
    o_ref[:, n_rot * LANES:] = y[:, n_rot * LANES:].astype(BF16)


def _dil_proj(hn, w, tab, *, seq, tm=512):
    t, d = hn.shape
    tm = min(tm, seq)
    ns = seq // tm
    n = w.shape[1]
    return pl.pallas_call(
        _dil_proj_kernel,
        grid=(t // tm,),
        in_specs=[
            pl.BlockSpec((tm, d), lambda i: (i, 0)),
            pl.BlockSpec((d, n), lambda i: (0, 0)),
            pl.BlockSpec((3, tm, LANES), lambda i: (0, i % ns, 0)),
        ],
        out_specs=pl.BlockSpec((tm, n), lambda i: (i, 0)),
        out_shape=jax.ShapeDtypeStruct((t, n), BF16),
        compiler_params=_params("parallel"),
        name="dil_proj",
    )(hn, w, tab)


def _cumsum_kernel(x_ref, o_ref):
    x = x_ref[...]
    n = x.shape[-1]
    lane = lax.broadcasted_iota(jnp.int32, x.shape, 1)
    k = 1
    while k < n:
        x = x + jnp.where(lane >= k, pltpu.roll(x, k, 1), 0.0)
        k *= 2
    o_ref[...] = -x


def _neg_cumsum(x):
    return pl.pallas_call(
        _cumsum_kernel,
        out_shape=jax.ShapeDtypeStruct(x.shape, F32),
        compiler_params=pltpu.CompilerParams(vmem_limit_bytes=VMEM_LIMIT),
        name="forget_cumsum",
    )(x)


def _attn_kernel(*refs, tq, scale, has_bias):
    if has_bias:
        q_ref, k_ref, v_ref, b_ref, o_ref, m_ref, l_ref, acc_ref = refs
    else:
        q_ref, k_ref, v_ref, o_ref, m_ref, l_ref, acc_ref = refs
        b_ref = None
    qi = pl.program_id(2)
    q = q_ref[0]
    m_ref[...] = jnp.full(m_ref.shape, NEG, F32)
    l_ref[...] = jnp.zeros(l_ref.shape, F32)
    acc_ref[...] = jnp.zeros(acc_ref.shape, F32)

    def step(ks, diagonal):
        k = k_ref[0, pl.ds(ks, tq), :]
        v = v_ref[0, pl.ds(ks, tq), :]
        s = lax.dot_general(q, k, (((1,), (1,)), ((), ())), preferred_element_type=F32) * scale
        if has_bias:
            s = s + b_ref[0, 0, :, pl.ds(ks, tq)]
        if diagonal:
            row = lax.broadcasted_iota(jnp.int32, s.shape, 0)
            col = lax.broadcasted_iota(jnp.int32, s.shape, 1)
            s = jnp.where(col <= row, s, NEG)
        m_prev = m_ref[...]
        m_new = jnp.maximum(m_prev, jnp.max(s, axis=-1, keepdims=True))
        p = jnp.exp(s - m_new)
        alpha = jnp.exp(m_prev - m_new)
        l_ref[...] = alpha * l_ref[...] + jnp.sum(p, axis=-1, keepdims=True)
        acc_ref[...] = alpha * acc_ref[...] + jnp.dot(p.astype(BF16), v, preferred_element_type=F32)
        m_ref[...] = m_new

    def body(j, carry):
        step(pl.multiple_of(j * tq, tq), False)
        return carry

    lax.fori_loop(0, qi, body, 0)
    step(pl.multiple_of(qi * tq, tq), True)
    o_ref[0] = (acc_ref[...] / l_ref[...]).astype(o_ref.dtype)


def _attention(q_arr, k_arr, v_arr, bias, *, heads, dk, q_off, k_off, v_off, scale, tq=512):
    b, s, _ = q_arr.shape
    tq = min(tq, s)
    dv = HEAD_DIM
    in_specs = [
        pl.BlockSpec((1, tq, dk), lambda bi, h, qi: (bi, qi, q_off + h)),
        pl.BlockSpec((1, s, dk), lambda bi, h, qi: (bi, 0, k_off + h)),
        pl.BlockSpec((1, s, dv), lambda bi, h, qi: (bi, 0, v_off + h)),
    ]
    args = [q_arr, k_arr, v_arr]
    if bias is not None:
        in_specs.append(pl.BlockSpec((1, 1, 1, s), lambda bi, h, qi: (bi, h, 0, 0)))
        args.append(bias)
    return pl.pallas_call(
        functools.partial(_attn_kernel, tq=tq, scale=scale, has_bias=bias is not None),
        grid=(b, heads, s // tq),
        in_specs=in_specs,
        out_specs=pl.BlockSpec((1, tq, dv), lambda bi, h, qi: (bi, qi, h)),
        out_shape=jax.ShapeDtypeStruct((b, s, heads * dv), BF16),
        scratch_shapes=[
            pltpu.VMEM((tq, 1), F32),
            pltpu.VMEM((tq, 1), F32),
            pltpu.VMEM((tq, dv), F32),
        ],
        compiler_params=_params("parallel", "parallel", "arbitrary"),
        name="attn_bias" if bias is not None else "attn",
    )(*args)


def _dil_kernel(q_ref, k_ref, v_ref, kp_ref, vp_ref, o_ref, lse_ref, *, scale):
    nb = q_ref.shape[1] // BLOCK
    row = lax.broadcasted_iota(jnp.int32, (BLOCK, 2 * BLOCK), 0)
    col = lax.broadcasted_iota(jnp.int32, (BLOCK, 2 * BLOCK), 1)
    band = (col >= row) & (col <= row + BLOCK)

    def block(own, kc, vc, mask, hs):
        q = q_ref[0, pl.ds(own, BLOCK), hs]
        s = lax.dot_general(q, kc, (((1,), (1,)), ((), ())), preferred_element_type=F32) * scale
        s = jnp.where(mask, s, NEG)
        m = jnp.max(s, axis=-1, keepdims=True)
        e = jnp.exp(s - m)
        l = jnp.sum(e, axis=-1, keepdims=True)
        o = jnp.dot(e.astype(BF16), vc, preferred_element_type=F32) / l
        o_ref[0, pl.ds(own, BLOCK), hs] = o
        lse_ref[0, pl.ds(own, BLOCK), hs] = jnp.broadcast_to(m + jnp.log(l), (BLOCK, HEAD_DIM))

    first_valid = jnp.where(pl.program_id(2) > 0, 0, BLOCK)
    mask0 = band & (col >= first_valid)
    for h in range(DIL_HEADS):
        hs = slice(h * HEAD_DIM, (h + 1) * HEAD_DIM)
        kc = jnp.concatenate([kp_ref[0, :, hs], k_ref[0, :BLOCK, hs]], axis=0)
        vc = jnp.concatenate([vp_ref[0, :, hs], v_ref[0, :BLOCK, hs]], axis=0)
        block(0, kc, vc, mask0, hs)

    def body(n, carry):
        own = pl.multiple_of(n * BLOCK, BLOCK)
        prev = pl.multiple_of((n - 1) * BLOCK, BLOCK)
        for h in range(DIL_HEADS):
            hs = slice(h * HEAD_DIM, (h + 1) * HEAD_DIM)
            block(own, k_ref[0, pl.ds(prev, 2 * BLOCK), hs], v_ref[0, pl.ds(prev, 2 * BLOCK), hs], band, hs)
        return carry

    lax.fori_loop(1, nb, body, 0)


def _dil_branch(qkv, *, dil, scale, chunk=1024):
    b, s, w3 = qkv.shape
    w = w3 // 3
    sub = s // dil
    tl = min(chunk, sub)
    bpc = tl // BLOCK
    view = qkv.reshape(b, sub, dil * w3)
    own = lambda j: pl.BlockSpec((1, tl, w), lambda bi, r, c: (bi, c, 3 * r + j))
    prev = lambda j: pl.BlockSpec((1, BLOCK, w), lambda bi, r, c: (bi, jnp.maximum(c * bpc - 1, 0), 3 * r + j))
    o, lse = pl.pallas_call(
        functools.partial(_dil_kernel, scale=scale),
        grid=(b, dil, sub // tl),
        in_specs=[own(0), own(1), own(2), prev(1), prev(2)],
        out_specs=[
            pl.BlockSpec((1, tl, w), lambda bi, r, c: (bi, c, r)),
            pl.BlockSpec((1, tl, w), lambda bi, r, c: (bi, c, r)),
        ],
        out_shape=[
            jax.ShapeDtypeStruct((b, sub, dil * w), F32),
            jax.ShapeDtypeStruct((b, sub, dil * w), F32),
        ],
        compiler_params=_params("parallel", "parallel", "parallel"),
        name=f"dilated_{dil}",
    )(view, view, view, view, view)
    return o.reshape(b, s, w), lse.reshape(b, s, w)


def _out_proj_kernel(x_ref, a_ref, b_ref, o1_ref, o2_ref, o3_ref, l1_ref, l2_ref, l3_ref, w_ref, y_ref):
    l1, l2, l3 = l1_ref[...], l2_ref[...], l3_ref[...]
    m = jnp.maximum(jnp.maximum(l1, l2), l3)
    e1, e2, e3 = jnp.exp(l1 - m), jnp.exp(l2 - m), jnp.exp(l3 - m)
    tot = e1 + e2 + e3
    c = (e1 / tot) * o1_ref[...] + (e2 / tot) * o2_ref[...] + (e3 / tot) * o3_ref[...]
    mixed = jnp.concatenate([a_ref[...], b_ref[...], c.astype(BF16)], axis=-1)
    y_ref[...] = x_ref[...] + jnp.dot(mixed, w_ref[...], preferred_element_type=F32)


def _out_proj(x, a, bm, os_, ls_, w, *, tm=256):
    t, d = x.shape
    tm = min(tm, t)
    row = lambda arr: pl.BlockSpec((tm, arr.shape[1]), lambda i: (i, 0))
    return pl.pallas_call(
        _out_proj_kernel,
        grid=(t // tm,),
        in_specs=[row(x), row(a), row(bm)] + [row(o) for o in os_] + [row(l) for l in ls_]
                 + [pl.BlockSpec(w.shape, lambda i: (0, 0))],
        out_specs=row(x),
        out_shape=jax.ShapeDtypeStruct((t, d), F32),
        compiler_params=_params("parallel"),
        name="out_proj",
    )(x, a, bm, *os_, *ls_, w)


def _rope_table(seq, dim, fill):
    half = dim // 2
    inv = 1.0 / (ROPE_THETA ** (jnp.arange(0, dim, 2, dtype=F32) / dim))
    ang = jnp.arange(seq, dtype=F32)[:, None] * inv[None, :]
    cos, sin = jnp.cos(ang), jnp.sin(ang)
    rest = LANES - dim
    zeros_h = jnp.zeros((seq, half), F32)
    zeros_r = jnp.zeros((seq, rest), F32)
    cos_t = jnp.concatenate([cos, cos, jnp.full((seq, rest), fill, F32)], axis=1)
    s_hi = jnp.concatenate([-sin, zeros_h, zeros_r], axis=1)
    s_lo = jnp.concatenate([zeros_h, sin, zeros_r], axis=1)
    return jnp.stack([cos_t, s_hi, s_lo], axis=0)


def _pad_cols(w, n):
    return jnp.pad(w, ((0, 0), (0, n - w.shape[1])))


def kernel(x, ffn1_norm, ffn1_w_gate, ffn1_w_up, ffn1_w_down, mix_norm, w_in, fox_forget_bias, mla_q_norm, mla_kv_norm, mla_w_uq, mla_w_ukv, w_out, ffn2_norm, ffn2_w_gate, ffn2_w_up, ffn2_w_down, final_norm):
    b, s, d = x.shape
    depth = w_in.shape[0]
    t = b * s
    fw = FOX_HEADS * HEAD_DIM
    dw = DIL_HEADS * HEAD_DIM
    o_fl = 3 * fw
    o_cq = o_fl + FOX_HEADS
    o_kr = o_cq + MLA_Q_RANK + MLA_KV_RANK
    o_dq = o_kr + MLA_ROPE

    tab_mla = _rope_table(s, MLA_ROPE, 1.0)
    tab_dil = _rope_table(s, PARTIAL_ROPE, 1.0)

    xf = x.reshape(t, d)
    for l in range(depth):
        last = l == depth - 1
        wi = w_in[l]
        w_fox = wi[:, :o_fl].astype(BF16)
        w_fl = _pad_cols(wi[:, o_fl:o_cq], LANES).astype(BF16)
        fbias = _pad_cols(fox_forget_bias[l].reshape(1, FOX_HEADS), LANES)
        w_c = jnp.concatenate([wi[:, o_cq:o_kr], _pad_cols(wi[:, o_kr:o_dq], LANES)], axis=1).astype(BF16)
        w_dil = wi[:, o_dq:].astype(BF16)
        wuq = mla_w_uq[l].reshape(MLA_Q_RANK, MLA_HEADS, MLA_NOPE + MLA_ROPE)
        wuq = jnp.pad(wuq, ((0, 0), (0, 0), (0, 2 * LANES - MLA_NOPE - MLA_ROPE)))
        wuq = wuq.reshape(MLA_Q_RANK, MLA_HEADS * 2 * LANES).astype(BF16)
        wukv = mla_w_ukv[l].reshape(MLA_KV_RANK, MLA_HEADS, MLA_NOPE + HEAD_DIM)
        w_k = wukv[:, :, :MLA_NOPE].reshape(MLA_KV_RANK, MLA_HEADS * MLA_NOPE).astype(BF16)
        w_v = wukv[:, :, MLA_NOPE:].reshape(MLA_KV_RANK, MLA_HEADS * HEAD_DIM).astype(BF16)

        xf, hn = _ffn(xf, ffn1_norm[l], ffn1_w_gate[l].astype(BF16), ffn1_w_up[l].astype(BF16),
                      ffn1_w_down[l].astype(BF16), mix_norm[l], emit_norm=True)

        fqkv, logf = _fox_proj(hn, w_fox, w_fl, fbias)
        logf = logf.reshape(b, s, LANES)[:, :, :FOX_HEADS].transpose(0, 2, 1).reshape(b * FOX_HEADS, s)
        neg_cum = _neg_cumsum(logf).reshape(b, FOX_HEADS, 1, s)
        fqkv = fqkv.reshape(b, s, 3 * fw)
        out_a = _attention(fqkv, fqkv, fqkv, neg_cum, heads=FOX_HEADS, dk=HEAD_DIM,
                           q_off=0, k_off=FOX_HEADS, v_off=2 * FOX_HEADS, scale=HEAD_DIM ** -0.5)

        q_b, k_b, v_b = _mla_proj(hn, w_c, mla_q_norm[l].reshape(1, -1), mla_kv_norm[l].reshape(1, -1),
                                  wuq, w_k, w_v, tab_mla, seq=s)
        out_b = _attention(q_b.reshape(b, s, -1), k_b.reshape(b, s, -1), v_b.reshape(b, s, -1), None,
                           heads=MLA_HEADS, dk=2 * LANES, q_off=0, k_off=0, v_off=0,
                           scale=(MLA_NOPE + MLA_ROPE) ** -0.5)

        dqkv = _dil_proj(hn, w_dil, tab_dil, seq=s).reshape(b, s, 3 * dw)
        outs, lses = [], []
        for _, dil in DIL_BRANCHES:
            o, lse = _dil_branch(dqkv, dil=dil, scale=HEAD_DIM ** -0.5)
            outs.append(o.reshape(t, dw))
            lses.append(lse.reshape(t, dw))

        xf = _out_proj(xf, out_a.reshape(t, fw), out_b.reshape(t, MLA_HEADS * HEAD_DIM), outs, lses,
                       w_out[l].astype(BF16))

        xf = _ffn(xf, ffn2_norm[l], ffn2_w_gate[l].astype(BF16), ffn2_w_up[l].astype(BF16),
                  ffn2_w_down[l].astype(BF16), final_norm if last else None, final_norm=last)
    return xf.reshape(b, s, d)
```

```python
import functools

import jax
import jax.numpy as jnp
from jax import lax
from jax.experimental import pallas as pl
from jax.experimental.pallas import tpu as pltpu

F32 = jnp.float32
BF16 = jnp.bfloat16

LANES = 128
HEAD_DIM = 128
BLOCK = 128
EPS = 1e-6
ROPE_THETA = 500000.0
PARTIAL_ROPE = HEAD_DIM // 4
FOX_HEADS = 4
MLA_HEADS = 8
MLA_Q_RANK = 512
MLA_KV_RANK = 512
MLA_NOPE = 128
MLA_ROPE = 64
DIL_HEADS = 4
DIL_BRANCHES = ((128, 1), (512, 4), (2048, 16))
NEG = -1e30
LOG2E = 1.4426950408889634
VMEM_LIMIT = 56 * 1024 * 1024


def _params(*sem):
    return pltpu.CompilerParams(dimension_semantics=sem, vmem_limit_bytes=VMEM_LIMIT)


def _rms(x, g):
    return x * lax.rsqrt(jnp.mean(x * x, axis=-1, keepdims=True) + EPS) * g


def _apply_rope(x, cos, s_hi, s_lo, half):
    return x * cos + pltpu.roll(x, LANES - half, 1) * s_hi + pltpu.roll(x, half, 1) * s_lo


def _ffn_kernel(*refs, emit_norm, final_norm):
    x_ref, g_ref, wg_ref, wu_ref, wd_ref = refs[:5]
    pos = 5
    g2_ref = None
    if emit_norm or final_norm:
        g2_ref = refs[pos]
        pos += 1
    o_ref = refs[pos]
    pos += 1
    hn_ref = None
    if emit_norm:
        hn_ref = refs[pos]
        pos += 1
    xn_ref = refs[pos]

    j = pl.program_id(1)
    nj = pl.num_programs(1)

    @pl.when(j == 0)
    def _():
        x = x_ref[...]
        xn_ref[...] = _rms(x, g_ref[...]).astype(BF16)
        o_ref[...] = x

    xn = xn_ref[...]
    gate = jnp.dot(xn, wg_ref[...], preferred_element_type=F32)
    up = jnp.dot(xn, wu_ref[...], preferred_element_type=F32)
    h = (gate * jax.nn.sigmoid(gate) * up * 0.5).astype(BF16)
    o_ref[...] += jnp.dot(h, wd_ref[...], preferred_element_type=F32)

    if emit_norm or final_norm:
        @pl.when(j == nj - 1)
        def _():
            y = _rms(o_ref[...], g2_ref[...])
            if emit_norm:
                hn_ref[...] = y.astype(BF16)
            else:
                o_ref[...] = y


def _ffn(x, g, wg, wu, wd, g2=None, *, emit_norm=False, final_norm=False, tm=512, tf=512):
    t, d = x.shape
    f = wg.shape[1]
    tm = min(tm, t)
    in_specs = [
        pl.BlockSpec((tm, d), lambda i, j: (i, 0)),
        pl.BlockSpec((1, d), lambda i, j: (0, 0)),
        pl.BlockSpec((d, tf), lambda i, j: (0, j)),
        pl.BlockSpec((d, tf), lambda i, j: (0, j)),
        pl.BlockSpec((tf, d), lambda i, j: (j, 0)),
    ]
    args = [x, g.reshape(1, d), wg, wu, wd]
    if emit_norm or final_norm:
        in_specs.append(pl.BlockSpec((1, d), lambda i, j: (0, 0)))
        args.append(g2.reshape(1, d))
    out_shape = [jax.ShapeDtypeStruct((t, d), F32)]
    out_specs = [pl.BlockSpec((tm, d), lambda i, j: (i, 0))]
    if emit_norm:
        out_shape.append(jax.ShapeDtypeStruct((t, d), BF16))
        out_specs.append(pl.BlockSpec((tm, d), lambda i, j: (i, 0)))
    res = pl.pallas_call(
        functools.partial(_ffn_kernel, emit_norm=emit_norm, final_norm=final_norm),
        grid=(t // tm, f // tf),
        in_specs=in_specs,
        out_specs=out_specs,
        out_shape=out_shape,
        scratch_shapes=[pltpu.VMEM((tm, d), BF16)],
        compiler_params=_params("parallel", "arbitrary"),
        name="ffn",
    )(*args)
    return res if emit_norm else res[0]


def _store_heads_transposed(vt_ref, v):
    for h in range(vt_ref.shape[0]):
        vt_ref[h] = v[:, h * HEAD_DIM:(h + 1) * HEAD_DIM].T.astype(BF16)


def _fox_proj_kernel(hn_ref, wqk_ref, wv_ref, wl_ref, b_ref, qk_ref, vt_ref, lf_ref):
    hn = hn_ref[...]
    qk_ref[...] = jnp.dot(hn, wqk_ref[...], preferred_element_type=F32).astype(BF16)
    _store_heads_transposed(vt_ref, jnp.dot(hn, wv_ref[...], preferred_element_type=F32))
    z = jnp.dot(hn, wl_ref[...], preferred_element_type=F32) + b_ref[...]
    lf_ref[...] = jnp.minimum(z, 0.0) - jnp.log1p(jnp.exp(-jnp.abs(z)))


def _fox_proj(hn, wqk, wv, wl, bias, *, tm=512):
    b, s, d = hn.shape
    tm = min(tm, s)
    n = wqk.shape[1]
    heads = wv.shape[1] // HEAD_DIM
    full = lambda a: pl.BlockSpec(a.shape, lambda bi, i: (0,) * a.ndim)
    return pl.pallas_call(
        _fox_proj_kernel,
        grid=(b, s // tm),
        in_specs=[pl.BlockSpec((None, tm, d), lambda bi, i: (bi, i, 0)), full(wqk), full(wv), full(wl), full(bias)],
        out_specs=[
            pl.BlockSpec((None, tm, n), lambda bi, i: (bi, i, 0)),
            pl.BlockSpec((None, heads, HEAD_DIM, tm), lambda bi, i: (bi, 0, 0, i)),
            pl.BlockSpec((None, tm, LANES), lambda bi, i: (bi, i, 0)),
        ],
        out_shape=[
            jax.ShapeDtypeStruct((b, s, n), BF16),
            jax.ShapeDtypeStruct((b, heads, HEAD_DIM, s), BF16),
            jax.ShapeDtypeStruct((b, s, LANES), F32),
        ],
        compiler_params=_params("parallel", "parallel"),
        name="fox_proj",
    )(hn, wqk, wv, wl, bias)


def _mla_proj_kernel(hn_ref, wc_ref, gq_ref, gkv_ref, wuq_ref, wk_ref, wv_ref, tab_ref,
                     q_ref, k_ref, vt_ref):
    hn = hn_ref[...]
    c = jnp.dot(hn, wc_ref[...], preferred_element_type=F32)
    cqn = _rms(c[:, :MLA_Q_RANK], gq_ref[...]).astype(BF16)
    ckvn = _rms(c[:, MLA_Q_RANK:MLA_Q_RANK + MLA_KV_RANK], gkv_ref[...]).astype(BF16)
    cos, s_hi, s_lo = tab_ref[0], tab_ref[1], tab_ref[2]
    half = MLA_ROPE // 2
    kr = _apply_rope(c[:, MLA_Q_RANK + MLA_KV_RANK:], cos, s_hi, s_lo, half).astype(BF16)
    q = jnp.dot(cqn, wuq_ref[...], preferred_element_type=F32)
    kn = jnp.dot(ckvn, wk_ref[...], preferred_element_type=F32)
    _store_heads_transposed(vt_ref, jnp.dot(ckvn, wv_ref[...], preferred_element_type=F32))
    for h in range(MLA_HEADS):
        lo = 2 * h * LANES
        q_ref[:, lo:lo + LANES] = q[:, lo:lo + LANES].astype(BF16)
        q_ref[:, lo + LANES:lo + 2 * LANES] = _apply_rope(
            q[:, lo + LANES:lo + 2 * LANES], cos, s_hi, s_lo, half).astype(BF16)
        k_ref[:, lo:lo + LANES] = kn[:, h * LANES:(h + 1) * LANES].astype(BF16)
        k_ref[:, lo + LANES:lo + 2 * LANES] = kr


def _mla_proj(hn, wc, gq, gkv, wuq, wk, wv, tab, *, tm=512):
    b, s, d = hn.shape
    tm = min(tm, s)
    nq = wuq.shape[1]
    heads = wv.shape[1] // HEAD_DIM
    full = lambda a: pl.BlockSpec(a.shape, lambda bi, i: (0,) * a.ndim)
    return pl.pallas_call(
        _mla_proj_kernel,
        grid=(b, s // tm),
        in_specs=[
            pl.BlockSpec((None, tm, d), lambda bi, i: (bi, i, 0)),
            full(wc), full(gq), full(gkv), full(wuq), full(wk), full(wv),
            pl.BlockSpec((3, tm, LANES), lambda bi, i: (0, i, 0)),
        ],
        out_specs=[
            pl.BlockSpec((None, tm, nq), lambda bi, i: (bi, i, 0)),
            pl.BlockSpec((None, tm, nq), lambda bi, i: (bi, i, 0)),
            pl.BlockSpec((None, heads, HEAD_DIM, tm), lambda bi, i: (bi, 0, 0, i)),
        ],
        out_shape=[
            jax.ShapeDtypeStruct((b, s, nq), BF16),
            jax.ShapeDtypeStruct((b, s, nq), BF16),
            jax.ShapeDtypeStruct((b, heads, HEAD_DIM, s), BF16),
        ],
        compiler_params=_params("parallel", "parallel"),
        name="mla_proj",
    )(hn, wc, gq, gkv, wuq, wk, wv, tab)


def _dil_proj_kernel(hn_ref, w_ref, tab_ref, *rest, dils):
    out_refs, scr = rest[:len(dils)], rest[len(dils):]
    y = jnp.dot(hn_ref[...], w_ref[...], preferred_element_type=F32)
    cos, s_hi, s_lo = tab_ref[0], tab_ref[1], tab_ref[2]
    half = PARTIAL_ROPE // 2
    n_rot = 2 * DIL_HEADS
    tm = y.shape[0]
    n_slab = y.shape[1] // LANES
    for j in range(n_slab):
        v = y[:, j * LANES:(j + 1) * LANES]
        scr[j][...] = _apply_rope(v, cos, s_hi, s_lo, half) if j < n_rot else v
    for o_ref, dil in zip(out_refs, dils):
        rows = tm // dil
        for r in range(dil):
            for j in range(n_slab):
                o_ref[r, :, j * LANES:(j + 1) * LANES] = scr[j][pl.ds(r, rows, stride=dil), :].astype(BF16)


def _dil_proj(hn, w, tab, *, dils, tm=512):
    b, s, d = hn.shape
    tm = min(tm, s)
    n = w.shape[1]
    return pl.pallas_call(
        functools.partial(_dil_proj_kernel, dils=dils),
        grid=(b, s // tm),
        in_specs=[
            pl.BlockSpec((None, tm, d), lambda bi, i: (bi, i, 0)),
            pl.BlockSpec((d, n), lambda bi, i: (0, 0)),
            pl.BlockSpec((3, tm, LANES), lambda bi, i: (0, i, 0)),
        ],
        out_specs=[pl.BlockSpec((None, dil, tm // dil, n), lambda bi, i: (bi, 0, i, 0)) for dil in dils],
        out_shape=[jax.ShapeDtypeStruct((b, dil, s // dil, n), BF16) for dil in dils],
        scratch_shapes=[pltpu.VMEM((tm, LANES), F32)] * (n // LANES),
        compiler_params=_params("parallel", "parallel"),
        name="dil_proj",
    )(hn, w, tab)


def _cumsum_kernel(x_ref, o_ref, *, out_scale):
    x = x_ref[...]
    n = x.shape[-1]
    lane = lax.broadcasted_iota(jnp.int32, x.shape, 1)
    k = 1
    while k < n:
        x = x + jnp.where(lane >= k, pltpu.roll(x, k, 1), 0.0)
        k *= 2
    o_ref[...] = x * out_scale


def _key_bias(x, scale):
    return pl.pallas_call(
        functools.partial(_cumsum_kernel, out_scale=-1.0 / scale),
        out_shape=jax.ShapeDtypeStruct(x.shape, F32),
        compiler_params=pltpu.CompilerParams(vmem_limit_bytes=VMEM_LIMIT),
        name="forget_cumsum",
    )(x)


def _attn_kernel(*refs, tq, cq, scale, has_bias):
    if has_bias:
        q_ref, k_ref, vt_ref, b_ref, o_ref, m_ref, acc_ref, brep_ref = refs
    else:
        q_ref, k_ref, vt_ref, o_ref, m_ref, acc_ref = refs
    qi = pl.program_id(2)
    n_chain = tq // cq
    dv = vt_ref.shape[0]
    c2 = scale * LOG2E

    if has_bias:
        @pl.when(qi == 0)
        def _():
            for j in range(brep_ref.shape[0] // tq):
                rows = jnp.broadcast_to(b_ref[:, j * tq:(j + 1) * tq], (LANES, tq))
                brep_ref[j * tq:(j + 1) * tq, :] = rows.T

    m_ref[...] = jnp.full(m_ref.shape, NEG, F32)
    acc_ref[...] = jnp.zeros(acc_ref.shape, F32)

    def scores(c, ks, size):
        q = q_ref[c * cq:(c + 1) * cq, :]
        k = k_ref[pl.ds(ks, size), :]
        return lax.dot_general(k, q, (((1,), (1,)), ((), ())), preferred_element_type=F32)

    def update(c, st, ks, size, diagonal):
        if has_bias:
            st = st + jnp.tile(brep_ref[pl.ds(ks, size), :], (1, cq // LANES))
        if diagonal:
            key = lax.broadcasted_iota(jnp.int32, st.shape, 0)
            qry = lax.broadcasted_iota(jnp.int32, st.shape, 1)
            st = jnp.where(key <= qry, st, NEG)
        m_prev = m_ref[c]
        m_new = jnp.maximum(m_prev, jnp.max(st, axis=0, keepdims=True))
        pt = jnp.exp2((st - m_new) * c2).astype(BF16)
        alpha = jnp.exp2((m_prev - m_new) * c2)
        v_ext = jnp.concatenate([vt_ref[:, pl.ds(ks, size)], jnp.ones((dv, size), BF16)], axis=0)
        acc_ref[c] = alpha * acc_ref[c] + jnp.dot(v_ext, pt, preferred_element_type=F32)
        m_ref[c] = m_new

    def body(j, carry):
        ks = pl.multiple_of(j * tq, tq)
        sts = [scores(c, ks, tq) for c in range(n_chain)]
        for c in range(n_chain):
            update(c, sts[c], ks, tq, False)
        return carry

    lax.fori_loop(0, qi, body, 0)
    for j in range(n_chain):
        ks = pl.multiple_of(qi * tq + j * cq, cq)
        sts = [scores(c, ks, cq) for c in range(j, n_chain)]
        for c in range(j, n_chain):
            update(c, sts[c - j], ks, cq, c == j)
    for c in range(n_chain):
        acc = acc_ref[c]
        o_ref[c * cq:(c + 1) * cq, :] = (acc[:dv] / acc[dv:]).T.astype(o_ref.dtype)


def _attention(q_arr, k_arr, vt_arr, bias, *, dk, q_off, k_off, scale, tq=1024, cq=256):
    b, s, _ = q_arr.shape
    heads, dv = vt_arr.shape[1], vt_arr.shape[2]
    tq = min(tq, s)
    cq = min(cq, tq)
    n_chain = tq // cq
    in_specs = [
        pl.BlockSpec((None, tq, dk), lambda bi, h, qi: (bi, qi, q_off + h)),
        pl.BlockSpec((None, s, dk), lambda bi, h, qi: (bi, 0, k_off + h)),
        pl.BlockSpec((None, None, dv, s), lambda bi, h, qi: (bi, h, 0, 0)),
    ]
    args = [q_arr, k_arr, vt_arr]
    scratch = [pltpu.VMEM((n_chain, 1, cq), F32), pltpu.VMEM((n_chain, 2 * dv, cq), F32)]
    if bias is not None:
        in_specs.append(pl.BlockSpec((None, None, 1, s), lambda bi, h, qi: (bi, h, 0, 0)))
        args.append(bias)
        scratch.append(pltpu.VMEM((s, LANES), F32))
    return pl.pallas_call(
        functools.partial(_attn_kernel, tq=tq, cq=cq, scale=scale, has_bias=bias is not None),
        grid=(b, heads, s // tq),
        in_specs=in_specs,
        out_specs=pl.BlockSpec((None, tq, dv), lambda bi, h, qi: (bi, qi, h)),
        out_shape=jax.ShapeDtypeStruct((b, s, heads * dv), BF16),
        scratch_shapes=scratch,
        compiler_params=_params("arbitrary", "arbitrary", "arbitrary"),
        name="attn_bias" if bias is not None else "attn",
    )(*args)


def _dil_kernel(q_ref, k_ref, v_ref, kp_ref, vp_ref, o_ref, lse_ref, *, scale):
    nb = q_ref.shape[0] // BLOCK
    c2 = scale * LOG2E
    row = lax.broadcasted_iota(jnp.int32, (BLOCK, 2 * BLOCK), 0)
    col = lax.broadcasted_iota(jnp.int32, (BLOCK, 2 * BLOCK), 1)
    band = (col >= row) & (col <= row + BLOCK)
    lane = lax.broadcasted_iota(jnp.int32, (BLOCK, LANES), 1)
    heads = [slice(h * HEAD_DIM, (h + 1) * HEAD_DIM) for h in range(DIL_HEADS)]

    def block(own, kcs, vcs, mask):
        lse_tile = jnp.zeros((BLOCK, LANES), F32)
        for h, hs in enumerate(heads):
            q = q_ref[pl.ds(own, BLOCK), hs]
            s = lax.dot_general(q, kcs[h], (((1,), (1,)), ((), ())), preferred_element_type=F32)
            s = jnp.where(mask, s, NEG)
            m = jnp.max(s, axis=-1, keepdims=True)
            e = jnp.exp2((s - m) * c2)
            l = jnp.sum(e, axis=-1, keepdims=True)
            o_ref[pl.ds(own, BLOCK), hs] = jnp.dot(e.astype(BF16), vcs[h], preferred_element_type=F32) / l
            lse_tile = jnp.where(lane == h, m * scale + jnp.log(l), lse_tile)
        lse_ref[pl.ds(own, BLOCK), :] = lse_tile

    first_valid = jnp.where(pl.program_id(2) > 0, 0, BLOCK)
    block(0,
          [jnp.concatenate([kp_ref[:, hs], k_ref[:BLOCK, hs]], axis=0) for hs in heads],
          [jnp.concatenate([vp_ref[:, hs], v_ref[:BLOCK, hs]], axis=0) for hs in heads],
          band & (col >= first_valid))

    def body(n, carry):
        own = pl.multiple_of(n * BLOCK, BLOCK)
        prev = pl.multiple_of((n - 1) * BLOCK, BLOCK)
        block(own,
              [k_ref[pl.ds(prev, 2 * BLOCK), hs] for hs in heads],
              [v_ref[pl.ds(prev, 2 * BLOCK), hs] for hs in heads],
              band)
        return carry

    lax.fori_loop(1, nb, body, 0)


def _dil_branch(qkv, *, scale, chunk=1024):
    b, dil, sub, w3 = qkv.shape
    w = w3 // 3
    tl = min(chunk, sub)
    bpc = tl // BLOCK
    own = lambda j: pl.BlockSpec((None, None, tl, w), lambda bi, r, c: (bi, r, c, j))
    prev = lambda j: pl.BlockSpec((None, None, BLOCK, w),
                                  lambda bi, r, c: (bi, r, jnp.maximum(c * bpc - 1, 0), j))
    return pl.pallas_call(
        functools.partial(_dil_kernel, scale=scale),
        grid=(b, dil, sub // tl),
        in_specs=[own(0), own(1), own(2), prev(1), prev(2)],
        out_specs=[
            pl.BlockSpec((None, None, tl, w), lambda bi, r, c: (bi, r, c, 0)),
            pl.BlockSpec((None, None, tl, LANES), lambda bi, r, c: (bi, r, c, 0)),
        ],
        out_shape=[
            jax.ShapeDtypeStruct((b, dil, sub, w), F32),
            jax.ShapeDtypeStruct((b, dil, sub, LANES), F32),
        ],
        compiler_params=_params("parallel", "parallel", "parallel"),
        name=f"dilated_{dil}",
    )(qkv, qkv, qkv, qkv, qkv)


def _out_proj_kernel(x_ref, a_ref, b_ref, *rest, dils):
    nbr = len(dils)
    o_refs, l_refs = rest[:nbr], rest[nbr:2 * nbr]
    w_ref, y_ref = rest[2 * nbr:2 * nbr + 2]
    scr = rest[2 * nbr + 2:]
    tm = x_ref.shape[0]

    def natural(ref, dil, slot, j):
        sl = slice(j * LANES, (j + 1) * LANES)
        if dil == 1:
            return ref[0, :, sl]
        rows = tm // dil
        for r in range(dil):
            scr[slot][pl.ds(r, rows, stride=dil), :] = ref[r, :, sl]
        return scr[slot][...]

    lses = [natural(l_refs[i], dils[i], 2 * i, 0) for i in range(nbr)]
    m = functools.reduce(jnp.maximum, lses)
    es = [jnp.exp(l - m) for l in lses]
    tot = functools.reduce(lambda u, v: u + v, es)
    wts = [e / tot for e in es]
    parts = [a_ref[...], b_ref[...]]
    for h in range(DIL_HEADS):
        c = None
        for i in range(nbr):
            term = wts[i][:, h:h + 1] * natural(o_refs[i], dils[i], 2 * i + 1, h)
            c = term if c is None else c + term
        parts.append(c.astype(BF16))
    mixed = jnp.concatenate(parts, axis=-1)
    y_ref[...] = x_ref[...] + jnp.dot(mixed, w_ref[...], preferred_element_type=F32)


def _out_proj(x, a, bm, os_, ls_, w, *, tm=512):
    b, s, d = x.shape
    tm = min(tm, s)
    dils = tuple(o.shape[1] for o in os_)
    row = lambda arr: pl.BlockSpec((None, tm, arr.shape[2]), lambda bi, i: (bi, i, 0))
    plane = lambda arr: pl.BlockSpec((None, arr.shape[1], tm // arr.shape[1], arr.shape[3]),
                                     lambda bi, i: (bi, 0, i, 0))
    return pl.pallas_call(
        functools.partial(_out_proj_kernel, dils=dils),
        grid=(b, s // tm),
        in_specs=[row(x), row(a), row(bm)] + [plane(o) for o in os_] + [plane(l) for l in ls_]
                 + [pl.BlockSpec(w.shape, lambda bi, i: (0, 0))],
        out_specs=row(x),
        out_shape=jax.ShapeDtypeStruct((b, s, d), F32),
        scratch_shapes=[pltpu.VMEM((tm, LANES), F32)] * (2 * len(dils)),
        compiler_params=_params("parallel", "parallel"),
        name="out_proj",
    )(x, a, bm, *os_, *ls_, w)


def _rope_table(seq, dim, fill):
    half = dim // 2
    inv = 1.0 / (ROPE_THETA ** (jnp.arange(0, dim, 2, dtype=F32) / dim))
    ang = jnp.arange(seq, dtype=F32)[:, None] * inv[None, :]
    cos, sin = jnp.cos(ang), jnp.sin(ang)
    rest = LANES - dim
    zeros_h = jnp.zeros((seq, half), F32)
    zeros_r = jnp.zeros((seq, rest), F32)
    cos_t = jnp.concatenate([cos, cos, jnp.full((seq, rest), fill, F32)], axis=1)
    s_hi = jnp.concatenate([-sin, zeros_h, zeros_r], axis=1)
    s_lo = jnp.concatenate([zeros_h, sin, zeros_r], axis=1)
    return jnp.stack([cos_t, s_hi, s_lo], axis=0)


def _pad_cols(w, n):
    return jnp.pad(w, ((0, 0), (0, n - w.shape[1])))


def kernel(x, ffn1_norm, ffn1_w_gate, ffn1_w_up, ffn1_w_down, mix_norm, w_in, fox_forget_bias, mla_q_norm, mla_kv_norm, mla_w_uq, mla_w_ukv, w_out, ffn2_norm, ffn2_w_gate, ffn2_w_up, ffn2_w_down, final_norm):
    b, s, d = x.shape
    depth = w_in.shape[0]
    t = b * s
    fw = FOX_HEADS * HEAD_DIM
    dw = DIL_HEADS * HEAD_DIM
    o_fl = 3 * fw
    o_cq = o_fl + FOX_HEADS
    o_kr = o_cq + MLA_Q_RANK + MLA_KV_RANK
    o_dq = o_kr + MLA_ROPE

    assert all(window == BLOCK * dil and s % (BLOCK * dil) == 0 for window, dil in DIL_BRANCHES)
    tab_mla = _rope_table(s, MLA_ROPE, 1.0)
    tab_dil = _rope_table(s, PARTIAL_ROPE, 1.0)

    xf = x.reshape(t, d)
    for l in range(depth):
        last = l == depth - 1
        wi = w_in[l]
        w_fqk = wi[:, :2 * fw].astype(BF16)
        w_fv = wi[:, 2 * fw:o_fl].astype(BF16)
        w_fl = _pad_cols(wi[:, o_fl:o_cq], LANES).astype(BF16)
        fbias = _pad_cols(fox_forget_bias[l].reshape(1, FOX_HEADS), LANES)
        w_c = jnp.concatenate([wi[:, o_cq:o_kr], _pad_cols(wi[:, o_kr:o_dq], LANES)], axis=1).astype(BF16)
        w_dil = wi[:, o_dq:].astype(BF16)
        wuq = mla_w_uq[l].reshape(MLA_Q_RANK, MLA_HEADS, MLA_NOPE + MLA_ROPE)
        wuq = jnp.pad(wuq, ((0, 0), (0, 0), (0, 2 * LANES - MLA_NOPE - MLA_ROPE)))
        wuq = wuq.reshape(MLA_Q_RANK, MLA_HEADS * 2 * LANES).astype(BF16)
        wukv = mla_w_ukv[l].reshape(MLA_KV_RANK, MLA_HEADS, MLA_NOPE + HEAD_DIM)
        w_k = wukv[:, :, :MLA_NOPE].reshape(MLA_KV_RANK, MLA_HEADS * MLA_NOPE).astype(BF16)
        w_v = wukv[:, :, MLA_NOPE:].reshape(MLA_KV_RANK, MLA_HEADS * HEAD_DIM).astype(BF16)

        xf, hn = _ffn(xf, ffn1_norm[l], ffn1_w_gate[l].astype(BF16), ffn1_w_up[l].astype(BF16),
                      ffn1_w_down[l].astype(BF16), mix_norm[l], emit_norm=True)

        hn = hn.reshape(b, s, d)
        fox_scale = HEAD_DIM ** -0.5
        fqk, fvt, logf = _fox_proj(hn, w_fqk, w_fv, w_fl, fbias)
        logf = logf[:, :, :FOX_HEADS].transpose(0, 2, 1).reshape(b * FOX_HEADS, s)
        key_bias = _key_bias(logf, fox_scale).reshape(b, FOX_HEADS, 1, s)
        out_a = _attention(fqk, fqk, fvt, key_bias, dk=HEAD_DIM, q_off=0, k_off=FOX_HEADS, scale=fox_scale)

        q_b, k_b, vt_b = _mla_proj(hn, w_c, mla_q_norm[l].reshape(1, -1), mla_kv_norm[l].reshape(1, -1),
                                   wuq, w_k, w_v, tab_mla)
        out_b = _attention(q_b, k_b, vt_b, None, dk=2 * LANES, q_off=0, k_off=0,
                           scale=(MLA_NOPE + MLA_ROPE) ** -0.5)

        dqkvs = _dil_proj(hn, w_dil, tab_dil, dils=tuple(dil for _, dil in DIL_BRANCHES))
        outs, lses = [], []
        for dqkv in dqkvs:
            o, lse = _dil_branch(dqkv, scale=HEAD_DIM ** -0.5)
            outs.append(o)
            lses.append(lse)

        xf = _out_proj(xf.reshape(b, s, d), out_a, out_b, outs, lses, w_out[l].astype(BF16)).reshape(t, d)

        xf = _ffn(xf, ffn2_norm[l], ffn2_w_gate[l].astype(BF16), ffn2_w_up[l].astype(BF16),
                  ffn2_w_down[l].astype(BF16), final_norm if last else None, final_norm=last)
    return xf.reshape(b, s, d)
```

```python
import functools

import jax
import jax.numpy as jnp
from jax import lax
from jax.experimental import pallas as pl
from jax.experimental.pallas import tpu as pltpu

F32 = jnp.float32
BF16 = jnp.bfloat16

LANES = 128
HEAD_DIM = 128
BLOCK = 128
EPS = 1e-6
ROPE_THETA = 500000.0
PARTIAL_ROPE = HEAD_DIM // 4
FOX_HEADS = 4
MLA_HEADS = 8
MLA_Q_RANK = 512
MLA_KV_RANK = 512
MLA_NOPE = 128
MLA_ROPE = 64
DIL_HEADS = 4
DIL_BRANCHES = ((128, 1), (512, 4), (2048, 16))
NEG = -1e30
LOG2E = 1.4426950408889634
VMEM_LIMIT = 56 * 1024 * 1024


def _params(*sem):
    return pltpu.CompilerParams(dimension_semantics=sem, vmem_limit_bytes=VMEM_LIMIT)


def _rms(x, g):
    return x * lax.rsqrt(jnp.mean(x * x, axis=-1, keepdims=True) + EPS) * g


def _apply_rope(x, cos, s_hi, s_lo, half):
    return x * cos + pltpu.roll(x, LANES - half, 1) * s_hi + pltpu.roll(x, half, 1) * s_lo


def _ffn_kernel(*refs, emit_norm, final_norm):
    x_ref, g_ref, wg_ref, wu_ref, wd_ref = refs[:5]
    pos = 5
    g2_ref = None
    if emit_norm or final_norm:
        g2_ref = refs[pos]
        pos += 1
    o_ref = refs[pos]
    pos += 1
    hn_ref = None
    if emit_norm:
        hn_ref = refs[pos]
        pos += 1
    xn_ref = refs[pos]

    j = pl.program_id(1)
    nj = pl.num_programs(1)

    @pl.when(j == 0)
    def _():
        x = x_ref[...]
        xn_ref[...] = _rms(x, g_ref[...]).astype(BF16)
        o_ref[...] = x

    xn = xn_ref[...]
    gate = jnp.dot(xn, wg_ref[...], preferred_element_type=F32)
    up = jnp.dot(xn, wu_ref[...], preferred_element_type=F32)
    h = (gate * jax.nn.sigmoid(gate) * up * 0.5).astype(BF16)
    o_ref[...] += jnp.dot(h, wd_ref[...], preferred_element_type=F32)

    if emit_norm or final_norm:
        @pl.when(j == nj - 1)
        def _():
            y = _rms(o_ref[...], g2_ref[...])
            if emit_norm:
                hn_ref[...] = y.astype(BF16)
            else:
                o_ref[...] = y


def _ffn(x, g, wg, wu, wd, g2=None, *, emit_norm=False, final_norm=False, tm=512, tf=512):
    t, d = x.shape
    f = wg.shape[1]
    tm = min(tm, t)
    in_specs = [
        pl.BlockSpec((tm, d), lambda i, j: (i, 0)),
        pl.BlockSpec((1, d), lambda i, j: (0, 0)),
        pl.BlockSpec((d, tf), lambda i, j: (0, j)),
        pl.BlockSpec((d, tf), lambda i, j: (0, j)),
        pl.BlockSpec((tf, d), lambda i, j: (j, 0)),
    ]
    args = [x, g.reshape(1, d), wg, wu, wd]
    if emit_norm or final_norm:
        in_specs.append(pl.BlockSpec((1, d), lambda i, j: (0, 0)))
        args.append(g2.reshape(1, d))
    out_shape = [jax.ShapeDtypeStruct((t, d), F32)]
    out_specs = [pl.BlockSpec((tm, d), lambda i, j: (i, 0))]
    if emit_norm:
        out_shape.append(jax.ShapeDtypeStruct((t, d), BF16))
        out_specs.append(pl.BlockSpec((tm, d), lambda i, j: (i, 0)))
    res = pl.pallas_call(
        functools.partial(_ffn_kernel, emit_norm=emit_norm, final_norm=final_norm),
        grid=(t // tm, f // tf),
        in_specs=in_specs,
        out_specs=out_specs,
        out_shape=out_shape,
        scratch_shapes=[pltpu.VMEM((tm, d), BF16)],
        compiler_params=_params("parallel", "arbitrary"),
        name="ffn",
    )(*args)
    return res if emit_norm else res[0]


def _store_heads_transposed(vt_ref, v):
    for h in range(vt_ref.shape[0]):
        vt_ref[h] = v[:, h * HEAD_DIM:(h + 1) * HEAD_DIM].T.astype(BF16)


def _fox_proj_kernel(hn_ref, wqk_ref, wv_ref, wl_ref, b_ref, qk_ref, vt_ref, lf_ref):
    hn = hn_ref[...]
    qk_ref[...] = jnp.dot(hn, wqk_ref[...], preferred_element_type=F32).astype(BF16)
    _store_heads_transposed(vt_ref, jnp.dot(hn, wv_ref[...], preferred_element_type=F32))
    z = jnp.dot(hn, wl_ref[...], preferred_element_type=F32) + b_ref[...]
    lf_ref[...] = jnp.minimum(z, 0.0) - jnp.log1p(jnp.exp(-jnp.abs(z)))


def _fox_proj(hn, wqk, wv, wl, bias, *, tm=512):
    b, s, d = hn.shape
    tm = min(tm, s)
    n = wqk.shape[1]
    heads = wv.shape[1] // HEAD_DIM
    full = lambda a: pl.BlockSpec(a.shape, lambda bi, i: (0,) * a.ndim)
    return pl.pallas_call(
        _fox_proj_kernel,
        grid=(b, s // tm),
        in_specs=[pl.BlockSpec((None, tm, d), lambda bi, i: (bi, i, 0)), full(wqk), full(wv), full(wl), full(bias)],
        out_specs=[
            pl.BlockSpec((None, tm, n), lambda bi, i: (bi, i, 0)),
            pl.BlockSpec((None, heads, HEAD_DIM, tm), lambda bi, i: (bi, 0, 0, i)),
            pl.BlockSpec((None, tm, LANES), lambda bi, i: (bi, i, 0)),
        ],
        out_shape=[
            jax.ShapeDtypeStruct((b, s, n), BF16),
            jax.ShapeDtypeStruct((b, heads, HEAD_DIM, s), BF16),
            jax.ShapeDtypeStruct((b, s, LANES), F32),
        ],
        compiler_params=_params("parallel", "parallel"),
        name="fox_proj",
    )(hn, wqk, wv, wl, bias)


def _mla_proj_kernel(hn_ref, wc_ref, gq_ref, gkv_ref, wuq_ref, wk_ref, wv_ref, tab_ref,
                     q_ref, k_ref, vt_ref):
    hn = hn_ref[...]
    c = jnp.dot(hn, wc_ref[...], preferred_element_type=F32)
    cqn = _rms(c[:, :MLA_Q_RANK], gq_ref[...]).astype(BF16)
    ckvn = _rms(c[:, MLA_Q_RANK:MLA_Q_RANK + MLA_KV_RANK], gkv_ref[...]).astype(BF16)
    cos, s_hi, s_lo = tab_ref[0], tab_ref[1], tab_ref[2]
    half = MLA_ROPE // 2
    kr = _apply_rope(c[:, MLA_Q_RANK + MLA_KV_RANK:], cos, s_hi, s_lo, half).astype(BF16)
    q = jnp.dot(cqn, wuq_ref[...], preferred_element_type=F32)
    kn = jnp.dot(ckvn, wk_ref[...], preferred_element_type=F32)
    _store_heads_transposed(vt_ref, jnp.dot(ckvn, wv_ref[...], preferred_element_type=F32))
    for h in range(MLA_HEADS):
        lo = 2 * h * LANES
        q_ref[:, lo:lo + LANES] = q[:, lo:lo + LANES].astype(BF16)
        q_ref[:, lo + LANES:lo + 2 * LANES] = _apply_rope(
            q[:, lo + LANES:lo + 2 * LANES], cos, s_hi, s_lo, half).astype(BF16)
        k_ref[:, lo:lo + LANES] = kn[:, h * LANES:(h + 1) * LANES].astype(BF16)
        k_ref[:, lo + LANES:lo + 2 * LANES] = kr


def _mla_proj(hn, wc, gq, gkv, wuq, wk, wv, tab, *, tm=512):
    b, s, d = hn.shape
    tm = min(tm, s)
    nq = wuq.shape[1]
    heads = wv.shape[1] // HEAD_DIM
    full = lambda a: pl.BlockSpec(a.shape, lambda bi, i: (0,) * a.ndim)
    return pl.pallas_call(
        _mla_proj_kernel,
        grid=(b, s // tm),
        in_specs=[
            pl.BlockSpec((None, tm, d), lambda bi, i: (bi, i, 0)),
            full(wc), full(gq), full(gkv), full(wuq), full(wk), full(wv),
            pl.BlockSpec((3, tm, LANES), lambda bi, i: (0, i, 0)),
        ],
        out_specs=[
            pl.BlockSpec((None, tm, nq), lambda bi, i: (bi, i, 0)),
            pl.BlockSpec((None, tm, nq), lambda bi, i: (bi, i, 0)),
            pl.BlockSpec((None, heads, HEAD_DIM, tm), lambda bi, i: (bi, 0, 0, i)),
        ],
        out_shape=[
            jax.ShapeDtypeStruct((b, s, nq), BF16),
            jax.ShapeDtypeStruct((b, s, nq), BF16),
            jax.ShapeDtypeStruct((b, heads, HEAD_DIM, s), BF16),
        ],
        compiler_params=_params("parallel", "parallel"),
        name="mla_proj",
    )(hn, wc, gq, gkv, wuq, wk, wv, tab)


def _dil_proj_kernel(hn_ref, w_ref, tab_ref, *rest, dils):
    out_refs, scr = rest[:len(dils)], rest[len(dils):]
    y = jnp.dot(hn_ref[...], w_ref[...], preferred_element_type=F32)
    cos, s_hi, s_lo = tab_ref[0], tab_ref[1], tab_ref[2]
    half = PARTIAL_ROPE // 2
    n_rot = 2 * DIL_HEADS
    tm = y.shape[0]
    n_slab = y.shape[1] // LANES
    for j in range(n_slab):
        v = y[:, j * LANES:(j + 1) * LANES]
        scr[j][...] = _apply_rope(v, cos, s_hi, s_lo, half) if j < n_rot else v
    for o_ref, dil in zip(out_refs, dils):
        rows = tm // dil
        for r in range(dil):
            for j in range(n_slab):
                o_ref[r, :, j * LANES:(j + 1) * LANES] = scr[j][pl.ds(r, rows, stride=dil), :].astype(BF16)


def _dil_proj(hn, w, tab, *, dils, tm=512):
    b, s, d = hn.shape
    tm = min(tm, s)
    n = w.shape[1]
    return pl.pallas_call(
        functools.partial(_dil_proj_kernel, dils=dils),
        grid=(b, s // tm),
        in_specs=[
            pl.BlockSpec((None, tm, d), lambda bi, i: (bi, i, 0)),
            pl.BlockSpec((d, n), lambda bi, i: (0, 0)),
            pl.BlockSpec((3, tm, LANES), lambda bi, i: (0, i, 0)),
        ],
        out_specs=[pl.BlockSpec((None, dil, tm // dil, n), lambda bi, i: (bi, 0, i, 0)) for dil in dils],
        out_shape=[jax.ShapeDtypeStruct((b, dil, s // dil, n), BF16) for dil in dils],
        scratch_shapes=[pltpu.VMEM((tm, LANES), F32)] * (n // LANES),
        compiler_params=_params("parallel", "parallel"),
        name="dil_proj",
    )(hn, w, tab)


def _cumsum_kernel(x_ref, o_ref, *, out_scale):
    x = x_ref[...]
    n = x.shape[-1]
    lane = lax.broadcasted_iota(jnp.int32, x.shape, 1)
    k = 1
    while k < n:
        x = x + jnp.where(lane >= k, pltpu.roll(x, k, 1), 0.0)
        k *= 2
    o_ref[...] = x * out_scale


def _key_bias(x, scale):
    return pl.pallas_call(
        functools.partial(_cumsum_kernel, out_scale=-1.0 / scale),
        out_shape=jax.ShapeDtypeStruct(x.shape, F32),
        compiler_params=pltpu.CompilerParams(vmem_limit_bytes=VMEM_LIMIT),
        name="forget_cumsum",
    )(x)


def _attn_kernel(*refs, tq, cq, scale, has_bias):
    if has_bias:
        q_ref, k_ref, vt_ref, b_ref, o_ref, m_ref, acc_ref, brep_ref = refs
    else:
        q_ref, k_ref, vt_ref, o_ref, m_ref, acc_ref = refs
    qi = pl.program_id(2)
    n_chain = tq // cq
    dv = vt_ref.shape[0]
    c2 = scale * LOG2E

    if has_bias:
        @pl.when(qi == 0)
        def _():
            for j in range(brep_ref.shape[0] // tq):
                rows = jnp.broadcast_to(b_ref[:, j * tq:(j + 1) * tq], (LANES, tq))
                brep_ref[j * tq:(j + 1) * tq, :] = rows.T

    m_ref[...] = jnp.full(m_ref.shape, NEG, F32)
    acc_ref[...] = jnp.zeros(acc_ref.shape, F32)

    def scores(c, ks, size):
        q = q_ref[c * cq:(c + 1) * cq, :]
        k = k_ref[pl.ds(ks, size), :]
        return lax.dot_general(k, q, (((1,), (1,)), ((), ())), preferred_element_type=F32)

    def update(c, st, ks, size, diagonal):
        if has_bias:
            st = st + jnp.tile(brep_ref[pl.ds(ks, size), :], (1, cq // LANES))
        if diagonal:
            key = lax.broadcasted_iota(jnp.int32, st.shape, 0)
            qry = lax.broadcasted_iota(jnp.int32, st.shape, 1)
            st = jnp.where(key <= qry, st, NEG)
        m_prev = m_ref[c]
        m_new = jnp.maximum(m_prev, jnp.max(st, axis=0, keepdims=True))
        pt = jnp.exp2((st - m_new) * c2).astype(BF16)
        alpha = jnp.exp2((m_prev - m_new) * c2)
        v_ext = jnp.concatenate([vt_ref[:, pl.ds(ks, size)], jnp.ones((dv, size), BF16)], axis=0)
        acc_ref[c] = alpha * acc_ref[c] + jnp.dot(v_ext, pt, preferred_element_type=F32)
        m_ref[c] = m_new

    def run(work, ahead):
        pending = [scores(c, ks, size) for c, ks, size, _ in work[:ahead]]
        for i, (c, ks, size, diagonal) in enumerate(work):
            st = pending.pop(0)
            if i + ahead < len(work):
                cn, ksn, sizen, _ = work[i + ahead]
                pending.append(scores(cn, ksn, sizen))
            update(c, st, ks, size, diagonal)

    def body(j, carry):
        ks = pl.multiple_of(j * tq, tq)
        run([(c, ks, tq, False) for c in range(n_chain)], 2)
        return carry

    lax.fori_loop(0, qi, body, 0)
    run([(c, pl.multiple_of(qi * tq + j * cq, cq), cq, c == j)
         for j in range(n_chain) for c in range(j, n_chain)], 4)
    for c in range(n_chain):
        acc = acc_ref[c]
        o_ref[c * cq:(c + 1) * cq, :] = (acc[:dv] / acc[dv:]).T.astype(o_ref.dtype)


def _attention(q_arr, k_arr, vt_arr, bias, *, dk, q_off, k_off, scale, tq=1024, cq=256):
    b, s, _ = q_arr.shape
    heads, dv = vt_arr.shape[1], vt_arr.shape[2]
    tq = min(tq, s)
    cq = min(cq, tq)
    n_chain = tq // cq
    in_specs = [
        pl.BlockSpec((None, tq, dk), lambda bi, h, qi: (bi, qi, q_off + h)),
        pl.BlockSpec((None, s, dk), lambda bi, h, qi: (bi, 0, k_off + h)),
        pl.BlockSpec((None, None, dv, s), lambda bi, h, qi: (bi, h, 0, 0)),
    ]
    args = [q_arr, k_arr, vt_arr]
    scratch = [pltpu.VMEM((n_chain, 1, cq), F32), pltpu.VMEM((n_chain, 2 * dv, cq), F32)]
    if bias is not None:
        in_specs.append(pl.BlockSpec((None, None, 1, s), lambda bi, h, qi: (bi, h, 0, 0)))
        args.append(bias)
        scratch.append(pltpu.VMEM((s, LANES), F32))
    return pl.pallas_call(
        functools.partial(_attn_kernel, tq=tq, cq=cq, scale=scale, has_bias=bias is not None),
        grid=(b, heads, s // tq),
        in_specs=in_specs,
        out_specs=pl.BlockSpec((None, tq, dv), lambda bi, h, qi: (bi, qi, h)),
        out_shape=jax.ShapeDtypeStruct((b, s, heads * dv), BF16),
        scratch_shapes=scratch,
        compiler_params=_params("arbitrary", "arbitrary", "arbitrary"),
        name="attn_bias" if bias is not None else "attn",
    )(*args)


def _dil_kernel(q_ref, k_ref, v_ref, kp_ref, vp_ref, o_ref, lse_ref, kbuf, vbuf, *, scale, group):
    planes, tl, _ = q_ref.shape
    nb = tl // BLOCK
    c2 = scale * LOG2E
    kbuf[:, :BLOCK, :] = kp_ref[...]
    kbuf[:, BLOCK:, :] = k_ref[...]
    vbuf[:, :BLOCK, :] = vp_ref[...]
    vbuf[:, BLOCK:, :] = v_ref[...]
    row = lax.broadcasted_iota(jnp.int32, (BLOCK, 2 * BLOCK), 0)
    col = lax.broadcasted_iota(jnp.int32, (BLOCK, 2 * BLOCK), 1)
    band = (col >= row) & (col <= row + BLOCK)
    lane = lax.broadcasted_iota(jnp.int32, (BLOCK, LANES), 1)
    ones = jnp.ones((2 * BLOCK, HEAD_DIM), BF16)
    heads = [slice(h * HEAD_DIM, (h + 1) * HEAD_DIM) for h in range(DIL_HEADS)]
    first_chunk = pl.program_id(2) == 0

    def locate(g, u):
        if nb % group == 0:
            gpp = nb // group
            p = g // gpp if planes > 1 else 0
            first = (g % gpp == 0) if u == 0 else None
            return p, (g % gpp) * group + u, first
        per = group // nb
        return g * per + u // nb, u % nb, (True if u % nb == 0 else None)

    def body(g, carry):
        work, masks = [], []
        for u in range(group):
            p, n, first = locate(g, u)
            start = n * BLOCK if isinstance(n, int) else pl.multiple_of(n * BLOCK, BLOCK)
            if first is None:
                masks.append(band)
            else:
                cond = first_chunk if first is True else jnp.logical_and(first_chunk, first)
                masks.append(band & (col >= jnp.where(cond, BLOCK, 0)))
            work += [(u, p, start, h) for h in range(DIL_HEADS)]

        def scores(u, p, start, h):
            q = q_ref[p, pl.ds(start, BLOCK), heads[h]]
            kc = kbuf[p, pl.ds(start, 2 * BLOCK), heads[h]]
            return lax.dot_general(q, kc, (((1,), (1,)), ((), ())), preferred_element_type=F32)

        ahead = 2 * DIL_HEADS
        pending = [scores(*w) for w in work[:ahead]]
        tiles = {}
        for i, (u, p, start, h) in enumerate(work):
            s = pending.pop(0)
            if i + ahead < len(work):
                pending.append(scores(*work[i + ahead]))
            s = jnp.where(masks[u], s, NEG)
            m = jnp.max(s, axis=-1, keepdims=True)
            e = jnp.exp2((s - m) * c2).astype(BF16)
            v_ext = jnp.concatenate([vbuf[p, pl.ds(start, 2 * BLOCK), heads[h]], ones], axis=1)
            pv = jnp.dot(e, v_ext, preferred_element_type=F32)
            denom = pv[:, HEAD_DIM:]
            o_ref[p, pl.ds(start, BLOCK), heads[h]] = pv[:, :HEAD_DIM] / denom
            tile = tiles.get(u, jnp.zeros((BLOCK, LANES), F32))
            tiles[u] = jnp.where(lane == h, m * scale + jnp.log(denom), tile)
            if h == DIL_HEADS - 1:
                lse_ref[p, pl.ds(start, BLOCK), :] = tiles.pop(u)
        return carry

    n_groups = planes * nb // group
    if n_groups == 1:
        body(0, 0)
    else:
        lax.fori_loop(0, n_groups, body, 0)


def _dil_branch(qkv, *, scale, rows=1024, group=4):
    b, dil, sub, w3 = qkv.shape
    w = w3 // 3
    tl = min(rows, sub)
    planes = min(rows // tl, dil)
    bpc = tl // BLOCK
    group = min(group, planes * bpc)
    assert (bpc % group == 0 or group % bpc == 0) and (planes * bpc) % group == 0
    own = lambda j: pl.BlockSpec((None, planes, tl, w), lambda bi, r, c: (bi, r, c, j))
    prev = lambda j: pl.BlockSpec((None, planes, BLOCK, w),
                                  lambda bi, r, c: (bi, r, jnp.maximum(c * bpc - 1, 0), j))
    return pl.pallas_call(
        functools.partial(_dil_kernel, scale=scale, group=group),
        grid=(b, dil // planes, sub // tl),
        in_specs=[own(0), own(1), own(2), prev(1), prev(2)],
        out_specs=[
            pl.BlockSpec((None, planes, tl, w), lambda bi, r, c: (bi, r, c, 0)),
            pl.BlockSpec((None, planes, tl, LANES), lambda bi, r, c: (bi, r, c, 0)),
        ],
        out_shape=[
            jax.ShapeDtypeStruct((b, dil, sub, w), F32),
            jax.ShapeDtypeStruct((b, dil, sub, LANES), F32),
        ],
        scratch_shapes=[pltpu.VMEM((planes, tl + BLOCK, w), BF16)] * 2,
        compiler_params=_params("parallel", "parallel", "parallel"),
        name=f"dilated_{dil}",
    )(qkv, qkv, qkv, qkv, qkv)


def _out_proj_kernel(x_ref, a_ref, b_ref, *rest, dils):
    nbr = len(dils)
    o_refs, l_refs = rest[:nbr], rest[nbr:2 * nbr]
    w_ref, y_ref = rest[2 * nbr:2 * nbr + 2]
    scr = rest[2 * nbr + 2:]
    tm = x_ref.shape[0]

    def natural(ref, dil, slot, j):
        sl = slice(j * LANES, (j + 1) * LANES)
        if dil == 1:
            return ref[0, :, sl]
        rows = tm // dil
        for r in range(dil):
            scr[slot][pl.ds(r, rows, stride=dil), :] = ref[r, :, sl]
        return scr[slot][...]

    lses = [natural(l_refs[i], dils[i], 2 * i, 0) for i in range(nbr)]
    m = functools.reduce(jnp.maximum, lses)
    es = [jnp.exp(l - m) for l in lses]
    tot = functools.reduce(lambda u, v: u + v, es)
    wts = [e / tot for e in es]
    parts = [a_ref[...], b_ref[...]]
    for h in range(DIL_HEADS):
        c = None
        for i in range(nbr):
            term = wts[i][:, h:h + 1] * natural(o_refs[i], dils[i], 2 * i + 1, h)
            c = term if c is None else c + term
        parts.append(c.astype(BF16))
    mixed = jnp.concatenate(parts, axis=-1)
    y_ref[...] = x_ref[...] + jnp.dot(mixed, w_ref[...], preferred_element_type=F32)


def _out_proj(x, a, bm, os_, ls_, w, *, tm=512):
    b, s, d = x.shape
    tm = min(tm, s)
    dils = tuple(o.shape[1] for o in os_)
    row = lambda arr: pl.BlockSpec((None, tm, arr.shape[2]), lambda bi, i: (bi, i, 0))
    plane = lambda arr: pl.BlockSpec((None, arr.shape[1], tm // arr.shape[1], arr.shape[3]),
                                     lambda bi, i: (bi, 0, i, 0))
    return pl.pallas_call(
        functools.partial(_out_proj_kernel, dils=dils),
        grid=(b, s // tm),
        in_specs=[row(x), row(a), row(bm)] + [plane(o) for o in os_] + [plane(l) for l in ls_]
                 + [pl.BlockSpec(w.shape, lambda bi, i: (0, 0))],
        out_specs=row(x),
        out_shape=jax.ShapeDtypeStruct((b, s, d), F32),
        scratch_shapes=[pltpu.VMEM((tm, LANES), F32)] * (2 * len(dils)),
        compiler_params=_params("parallel", "parallel"),
        name="out_proj",
    )(x, a, bm, *os_, *ls_, w)


def _rope_table(seq, dim, fill):
    half = dim // 2
    inv = 1.0 / (ROPE_THETA ** (jnp.arange(0, dim, 2, dtype=F32) / dim))
    ang = jnp.arange(seq, dtype=F32)[:, None] * inv[None, :]
    cos, sin = jnp.cos(ang), jnp.sin(ang)
    rest = LANES - dim
    zeros_h = jnp.zeros((seq, half), F32)
    zeros_r = jnp.zeros((seq, rest), F32)
    cos_t = jnp.concatenate([cos, cos, jnp.full((seq, rest), fill, F32)], axis=1)
    s_hi = jnp.concatenate([-sin, zeros_h, zeros_r], axis=1)
    s_lo = jnp.concatenate([zeros_h, sin, zeros_r], axis=1)
    return jnp.stack([cos_t, s_hi, s_lo], axis=0)


def _pad_cols(w, n):
    return jnp.pad(w, ((0, 0), (0, n - w.shape[1])))


def kernel(x, ffn1_norm, ffn1_w_gate, ffn1_w_up, ffn1_w_down, mix_norm, w_in, fox_forget_bias, mla_q_norm, mla_kv_norm, mla_w_uq, mla_w_ukv, w_out, ffn2_norm, ffn2_w_gate, ffn2_w_up, ffn2_w_down, final_norm):
    b, s, d = x.shape
    depth = w_in.shape[0]
    t = b * s
    fw = FOX_HEADS * HEAD_DIM
    dw = DIL_HEADS * HEAD_DIM
    o_fl = 3 * fw
    o_cq = o_fl + FOX_HEADS
    o_kr = o_cq + MLA_Q_RANK + MLA_KV_RANK
    o_dq = o_kr + MLA_ROPE

    assert all(window == BLOCK * dil and s % (BLOCK * dil) == 0 for window, dil in DIL_BRANCHES)
    tab_mla = _rope_table(s, MLA_ROPE, 1.0)
    tab_dil = _rope_table(s, PARTIAL_ROPE, 1.0)

    xf = x.reshape(t, d)
    for l in range(depth):
        last = l == depth - 1
        wi = w_in[l]
        w_fqk = wi[:, :2 * fw].astype(BF16)
        w_fv = wi[:, 2 * fw:o_fl].astype(BF16)
        w_fl = _pad_cols(wi[:, o_fl:o_cq], LANES).astype(BF16)
        fbias = _pad_cols(fox_forget_bias[l].reshape(1, FOX_HEADS), LANES)
        w_c = jnp.concatenate([wi[:, o_cq:o_kr], _pad_cols(wi[:, o_kr:o_dq], LANES)], axis=1).astype(BF16)
        w_dil = wi[:, o_dq:].astype(BF16)
        wuq = mla_w_uq[l].reshape(MLA_Q_RANK, MLA_HEADS, MLA_NOPE + MLA_ROPE)
        wuq = jnp.pad(wuq, ((0, 0), (0, 0), (0, 2 * LANES - MLA_NOPE - MLA_ROPE)))
        wuq = wuq.reshape(MLA_Q_RANK, MLA_HEADS * 2 * LANES).astype(BF16)
        wukv = mla_w_ukv[l].reshape(MLA_KV_RANK, MLA_HEADS, MLA_NOPE + HEAD_DIM)
        w_k = wukv[:, :, :MLA_NOPE].reshape(MLA_KV_RANK, MLA_HEADS * MLA_NOPE).astype(BF16)
        w_v = wukv[:, :, MLA_NOPE:].reshape(MLA_KV_RANK, MLA_HEADS * HEAD_DIM).astype(BF16)

        xf, hn = _ffn(xf, ffn1_norm[l], ffn1_w_gate[l].astype(BF16), ffn1_w_up[l].astype(BF16),
                      ffn1_w_down[l].astype(BF16), mix_norm[l], emit_norm=True)

        hn = hn.reshape(b, s, d)
        fox_scale = HEAD_DIM ** -0.5
        fqk, fvt, logf = _fox_proj(hn, w_fqk, w_fv, w_fl, fbias)
        logf = logf[:, :, :FOX_HEADS].transpose(0, 2, 1).reshape(b * FOX_HEADS, s)
        key_bias = _key_bias(logf, fox_scale).reshape(b, FOX_HEADS, 1, s)
        out_a = _attention(fqk, fqk, fvt, key_bias, dk=HEAD_DIM, q_off=0, k_off=FOX_HEADS, scale=fox_scale)

        q_b, k_b, vt_b = _mla_proj(hn, w_c, mla_q_norm[l].reshape(1, -1), mla_kv_norm[l].reshape(1, -1),
                                   wuq, w_k, w_v, tab_mla)
        out_b = _attention(q_b, k_b, vt_b, None, dk=2 * LANES, q_off=0, k_off=0,
                           scale=(MLA_NOPE + MLA_ROPE) ** -0.5)

        dqkvs = _dil_proj(hn, w_dil, tab_dil, dils=tuple(dil for _, dil in DIL_BRANCHES))
        outs, lses = [], []
        for dqkv in dqkvs:
            o, lse = _dil_branch(dqkv, scale=HEAD_DIM ** -0.5)
            outs.append(o)
            lses.append(lse)

        xf = _out_proj(xf.reshape(b, s, d), out_a, out_b, outs, lses, w_out[l].astype(BF16)).reshape(t, d)

        xf = _ffn(xf, ffn2_norm[l], ffn2_w_gate[l].astype(BF16), ffn2_w_up[l].astype(BF16),
                  ffn2_w_down[l].astype(BF16), final_norm if last else None, final_norm=last)
    return xf.reshape(b, s, d)
```

```python
import functools

import jax
import jax.numpy as jnp
from jax import lax
from jax.experimental import pallas as pl
from jax.experimental.pallas import tpu as pltpu

F32 = jnp.float32
BF16 = jnp.bfloat16

LANES = 128
HEAD_DIM = 128
BLOCK = 128
EPS = 1e-6
ROPE_THETA = 500000.0
PARTIAL_ROPE = HEAD_DIM // 4
FOX_HEADS = 4
MLA_HEADS = 8
MLA_Q_RANK = 512
MLA_KV_RANK = 512
MLA_NOPE = 128
MLA_ROPE = 64
DIL_HEADS = 4
DIL_BRANCHES = ((128, 1), (512, 4), (2048, 16))
NEG = -1e30
LOG2E = 1.4426950408889634
VMEM_LIMIT = 56 * 1024 * 1024
FFN_TILE = 512
BF16_ROWS = 16


def _params(*sem):
    return pltpu.CompilerParams(dimension_semantics=sem, vmem_limit_bytes=VMEM_LIMIT)


def _rms(x, g):
    return x * lax.rsqrt(jnp.mean(x * x, axis=-1, keepdims=True) + EPS) * g


def _apply_rope(x, cos, s_hi, s_lo, half):
    return x * cos + pltpu.roll(x, LANES - half, 1) * s_hi + pltpu.roll(x, half, 1) * s_lo


def _ffn_kernel(*refs, emit_norm, final_norm):
    x_hbm, g_ref, wgu_ref, wd_ref = refs[:4]
    pos = 4
    g2_ref = None
    if emit_norm or final_norm:
        g2_ref = refs[pos]
        pos += 1
    o_ref = refs[pos]
    pos += 1
    hn_ref = None
    if emit_norm:
        hn_ref = refs[pos]
        pos += 1
    xn_ref, xbuf, sem = refs[pos:pos + 3]

    i, j = pl.program_id(0), pl.program_id(1)
    ni, nj = pl.num_programs(0), pl.num_programs(1)
    tm = xbuf.shape[0]
    tf = wd_ref.shape[0]

    def x_copy(tile):
        return pltpu.make_async_copy(x_hbm.at[pl.ds(pl.multiple_of(tile * tm, tm), tm), :], xbuf, sem)

    @pl.when(jnp.logical_and(i == 0, j == 0))
    def _():
        x_copy(0).start()

    @pl.when(j == 0)
    def _():
        x_copy(i).wait()
        x = xbuf[...]
        xn_ref[...] = _rms(x, g_ref[...]).astype(BF16)
        o_ref[...] = x

    @pl.when(jnp.logical_and(j == 1, i + 1 < ni))
    def _():
        x_copy(i + 1).start()

    gu = jnp.dot(xn_ref[...], wgu_ref[...], preferred_element_type=F32)
    gate, up = gu[:, :tf], gu[:, tf:]
    h = (gate * jax.nn.sigmoid(gate) * up * 0.5).astype(BF16)
    o_ref[...] += jnp.dot(h, wd_ref[...], preferred_element_type=F32)

    if emit_norm or final_norm:
        @pl.when(j == nj - 1)
        def _():
            y = _rms(o_ref[...], g2_ref[...])
            if emit_norm:
                hn_ref[...] = y.astype(BF16)
            else:
                o_ref[...] = y


def _ffn(x, g, wgu, wd, g2=None, *, emit_norm=False, final_norm=False, tm=1024):
    t, d = x.shape
    nj, tf, _ = wd.shape
    tm = min(tm, t)
    assert nj >= 2
    in_specs = [
        pl.BlockSpec(memory_space=pl.ANY),
        pl.BlockSpec((1, d), lambda i, j: (0, 0)),
        pl.BlockSpec((None, d, 2 * tf), lambda i, j: (j, 0, 0)),
        pl.BlockSpec((None, tf, d), lambda i, j: (j, 0, 0)),
    ]
    args = [x, g.reshape(1, d), wgu, wd]
    if emit_norm or final_norm:
        in_specs.append(pl.BlockSpec((1, d), lambda i, j: (0, 0)))
        args.append(g2.reshape(1, d))
    out_shape = [jax.ShapeDtypeStruct((t, d), F32)]
    out_specs = [pl.BlockSpec((tm, d), lambda i, j: (i, 0))]
    if emit_norm:
        out_shape.append(jax.ShapeDtypeStruct((t, d), BF16))
        out_specs.append(pl.BlockSpec((tm, d), lambda i, j: (i, 0), pipeline_mode=pl.Buffered(1)))
    res = pl.pallas_call(
        functools.partial(_ffn_kernel, emit_norm=emit_norm, final_norm=final_norm),
        grid=(t // tm, nj),
        in_specs=in_specs,
        out_specs=out_specs,
        out_shape=out_shape,
        scratch_shapes=[pltpu.VMEM((tm, d), BF16), pltpu.VMEM((tm, d), F32), pltpu.SemaphoreType.DMA],
        compiler_params=_params("arbitrary", "arbitrary"),
        name="ffn",
    )(*args)
    return res if emit_norm else res[0]


def _store_heads_transposed(vt_ref, v):
    for h in range(vt_ref.shape[0]):
        vt_ref[h] = v[:, h * HEAD_DIM:(h + 1) * HEAD_DIM].T.astype(BF16)


def _fox_proj_kernel(hn_ref, wqk_ref, wv_ref, wl_ref, b_ref, qk_ref, vt_ref, lf_ref):
    hn = hn_ref[...]
    qk_ref[...] = jnp.dot(hn, wqk_ref[...], preferred_element_type=F32).astype(BF16)
    _store_heads_transposed(vt_ref, jnp.dot(hn, wv_ref[...], preferred_element_type=F32))
    z = jnp.dot(hn, wl_ref[...], preferred_element_type=F32) + b_ref[...]
    lf_ref[...] = jnp.minimum(z, 0.0) - jnp.log1p(jnp.exp(-jnp.abs(z)))


def _fox_proj(hn, wqk, wv, wl, bias, *, tm=512):
    b, s, d = hn.shape
    tm = min(tm, s)
    n = wqk.shape[1]
    heads = wv.shape[1] // HEAD_DIM
    full = lambda a: pl.BlockSpec(a.shape, lambda bi, i: (0,) * a.ndim)
    return pl.pallas_call(
        _fox_proj_kernel,
        grid=(b, s // tm),
        in_specs=[pl.BlockSpec((None, tm, d), lambda bi, i: (bi, i, 0)), full(wqk), full(wv), full(wl), full(bias)],
        out_specs=[
            pl.BlockSpec((None, tm, n), lambda bi, i: (bi, i, 0)),
            pl.BlockSpec((None, heads, HEAD_DIM, tm), lambda bi, i: (bi, 0, 0, i)),
            pl.BlockSpec((None, tm, LANES), lambda bi, i: (bi, i, 0)),
        ],
        out_shape=[
            jax.ShapeDtypeStruct((b, s, n), BF16),
            jax.ShapeDtypeStruct((b, heads, HEAD_DIM, s), BF16),
            jax.ShapeDtypeStruct((b, s, LANES), F32),
        ],
        compiler_params=_params("parallel", "parallel"),
        name="fox_proj",
    )(hn, wqk, wv, wl, bias)


def _mla_proj_kernel(hn_ref, wc_ref, gq_ref, gkv_ref, wuq_ref, wk_ref, wv_ref, tab_ref,
                     q_ref, k_ref, vt_ref):
    hn = hn_ref[...]
    c = jnp.dot(hn, wc_ref[...], preferred_element_type=F32)
    cqn = _rms(c[:, :MLA_Q_RANK], gq_ref[...]).astype(BF16)
    ckvn = _rms(c[:, MLA_Q_RANK:MLA_Q_RANK + MLA_KV_RANK], gkv_ref[...]).astype(BF16)
    cos, s_hi, s_lo = tab_ref[0], tab_ref[1], tab_ref[2]
    half = MLA_ROPE // 2
    kr = _apply_rope(c[:, MLA_Q_RANK + MLA_KV_RANK:], cos, s_hi, s_lo, half).astype(BF16)
    q = jnp.dot(cqn, wuq_ref[...], preferred_element_type=F32)
    kn = jnp.dot(ckvn, wk_ref[...], preferred_element_type=F32)
    _store_heads_transposed(vt_ref, jnp.dot(ckvn, wv_ref[...], preferred_element_type=F32))
    for h in range(MLA_HEADS):
        lo = 2 * h * LANES
        q_ref[:, lo:lo + LANES] = q[:, lo:lo + LANES].astype(BF16)
        q_ref[:, lo + LANES:lo + 2 * LANES] = _apply_rope(
            q[:, lo + LANES:lo + 2 * LANES], cos, s_hi, s_lo, half).astype(BF16)
        k_ref[:, lo:lo + LANES] = kn[:, h * LANES:(h + 1) * LANES].astype(BF16)
        k_ref[:, lo + LANES:lo + 2 * LANES] = kr


def _mla_proj(hn, wc, gq, gkv, wuq, wk, wv, tab, *, tm=512):
    b, s, d = hn.shape
    tm = min(tm, s)
    nq = wuq.shape[1]
    heads = wv.shape[1] // HEAD_DIM
    full = lambda a: pl.BlockSpec(a.shape, lambda bi, i: (0,) * a.ndim)
    return pl.pallas_call(
        _mla_proj_kernel,
        grid=(b, s // tm),
        in_specs=[
            pl.BlockSpec((None, tm, d), lambda bi, i: (bi, i, 0)),
            full(wc), full(gq), full(gkv), full(wuq), full(wk), full(wv),
            pl.BlockSpec((3, tm, LANES), lambda bi, i: (0, i, 0)),
        ],
        out_specs=[
            pl.BlockSpec((None, tm, nq), lambda bi, i: (bi, i, 0)),
            pl.BlockSpec((None, tm, nq), lambda bi, i: (bi, i, 0)),
            pl.BlockSpec((None, heads, HEAD_DIM, tm), lambda bi, i: (bi, 0, 0, i)),
        ],
        out_shape=[
            jax.ShapeDtypeStruct((b, s, nq), BF16),
            jax.ShapeDtypeStruct((b, s, nq), BF16),
            jax.ShapeDtypeStruct((b, heads, HEAD_DIM, s), BF16),
        ],
        compiler_params=_params("parallel", "parallel"),
        name="mla_proj",
    )(hn, wc, gq, gkv, wuq, wk, wv, tab)


def _dil_proj_kernel(hn_ref, w_ref, tab_ref, *rest, dils):
    out_refs, scr = rest[:len(dils)], rest[len(dils):]
    y = jnp.dot(hn_ref[...], w_ref[...], preferred_element_type=F32)
    cos, s_hi, s_lo = tab_ref[0], tab_ref[1], tab_ref[2]
    half = PARTIAL_ROPE // 2
    n_rot = 2 * DIL_HEADS
    tm = y.shape[0]
    n_slab = y.shape[1] // LANES
    for j in range(n_slab):
        v = y[:, j * LANES:(j + 1) * LANES]
        scr[j][...] = _apply_rope(v, cos, s_hi, s_lo, half) if j < n_rot else v
    for o_ref, dil in zip(out_refs, dils):
        rows = tm // dil
        for r in range(dil):
            for j in range(n_slab):
                o_ref[r, :, j * LANES:(j + 1) * LANES] = scr[j][pl.ds(r, rows, stride=dil), :].astype(BF16)


def _dil_proj(hn, w, tab, *, dils, tm=512):
    b, s, d = hn.shape
    tm = min(tm, s)
    n = w.shape[1]
    return pl.pallas_call(
        functools.partial(_dil_proj_kernel, dils=dils),
        grid=(b, s // tm),
        in_specs=[
            pl.BlockSpec((None, tm, d), lambda bi, i: (bi, i, 0)),
            pl.BlockSpec((d, n), lambda bi, i: (0, 0)),
            pl.BlockSpec((3, tm, LANES), lambda bi, i: (0, i, 0)),
        ],
        out_specs=[pl.BlockSpec((None, dil, tm // dil, n), lambda bi, i: (bi, 0, i, 0)) for dil in dils],
        out_shape=[jax.ShapeDtypeStruct((b, dil, s // dil, n), BF16) for dil in dils],
        scratch_shapes=[pltpu.VMEM((tm, LANES), F32)] * (n // LANES),
        compiler_params=_params("parallel", "parallel"),
        name="dil_proj",
    )(hn, w, tab)


def _cumsum_kernel(x_ref, o_ref, *, out_scale):
    x = x_ref[...]
    n = x.shape[-1]
    lane = lax.broadcasted_iota(jnp.int32, x.shape, 1)
    k = 1
    while k < n:
        x = x + jnp.where(lane >= k, pltpu.roll(x, k, 1), 0.0)
        k *= 2
    o_ref[...] = x * out_scale


def _key_bias(x, scale):
    return pl.pallas_call(
        functools.partial(_cumsum_kernel, out_scale=-1.0 / scale),
        out_shape=jax.ShapeDtypeStruct(x.shape, F32),
        compiler_params=pltpu.CompilerParams(vmem_limit_bytes=VMEM_LIMIT),
        name="forget_cumsum",
    )(x)


def _attn_kernel(*refs, tq, cq, scale, has_bias):
    if has_bias:
        q_ref, k_ref, vt_ref, b_ref, o_ref, m_ref, acc_ref, brep_ref = refs
    else:
        q_ref, k_ref, vt_ref, o_ref, m_ref, acc_ref = refs
    qi = pl.program_id(2)
    n_chain = tq // cq
    dv = vt_ref.shape[0]
    c2 = scale * LOG2E

    if has_bias:
        @pl.when(qi == 0)
        def _():
            for j in range(brep_ref.shape[0] // tq):
                rows = jnp.broadcast_to(b_ref[:, j * tq:(j + 1) * tq], (LANES, tq))
                brep_ref[j * tq:(j + 1) * tq, :] = rows.T

    m_ref[...] = jnp.full(m_ref.shape, NEG, F32)
    acc_ref[...] = jnp.zeros(acc_ref.shape, F32)

    def scores(c, ks, size):
        q = q_ref[c * cq:(c + 1) * cq, :]
        k = k_ref[pl.ds(ks, size), :]
        return lax.dot_general(k, q, (((1,), (1,)), ((), ())), preferred_element_type=F32)

    def update(c, st, ks, size, diagonal):
        if has_bias:
            st = st + jnp.tile(brep_ref[pl.ds(ks, size), :], (1, cq // LANES))
        if diagonal:
            key = lax.broadcasted_iota(jnp.int32, st.shape, 0)
            qry = lax.broadcasted_iota(jnp.int32, st.shape, 1)
            st = jnp.where(key <= qry, st, NEG)
        m_prev = m_ref[c]
        m_new = jnp.maximum(m_prev, jnp.max(st, axis=0, keepdims=True))
        pt = jnp.exp2((st - m_new) * c2).astype(BF16)
        alpha = jnp.exp2((m_prev - m_new) * c2)
        v_ext = jnp.concatenate([vt_ref[:, pl.ds(ks, size)], jnp.ones((BF16_ROWS, size), BF16)], axis=0)
        acc_ref[c] = alpha * acc_ref[c] + jnp.dot(v_ext, pt, preferred_element_type=F32)
        m_ref[c] = m_new

    def run(work, ahead):
        pending = [scores(c, ks, size) for c, ks, size, _ in work[:ahead]]
        for i, (c, ks, size, diagonal) in enumerate(work):
            st = pending.pop(0)
            if i + ahead < len(work):
                cn, ksn, sizen, _ = work[i + ahead]
                pending.append(scores(cn, ksn, sizen))
            update(c, st, ks, size, diagonal)

    def body(j, carry):
        ks = pl.multiple_of(j * tq, tq)
        run([(c, ks, tq, False) for c in range(n_chain)], 2)
        return carry

    lax.fori_loop(0, qi, body, 0)
    run([(c, pl.multiple_of(qi * tq + j * cq, cq), cq, c == j)
         for j in range(n_chain) for c in range(j, n_chain)], 4)
    for c in range(n_chain):
        acc = acc_ref[c]
        o_ref[c * cq:(c + 1) * cq, :] = (acc[:dv] / acc[dv:dv + 1]).T.astype(o_ref.dtype)


def _attention(q_arr, k_arr, vt_arr, bias, *, dk, q_off, k_off, scale, tq=1024, cq=256):
    b, s, _ = q_arr.shape
    heads, dv = vt_arr.shape[1], vt_arr.shape[2]
    tq = min(tq, s)
    cq = min(cq, tq)
    n_chain = tq // cq
    in_specs = [
        pl.BlockSpec((None, tq, dk), lambda bi, h, qi: (bi, qi, q_off + h)),
        pl.BlockSpec((None, s, dk), lambda bi, h, qi: (bi, 0, k_off + h)),
        pl.BlockSpec((None, None, dv, s), lambda bi, h, qi: (bi, h, 0, 0)),
    ]
    args = [q_arr, k_arr, vt_arr]
    scratch = [pltpu.VMEM((n_chain, 1, cq), F32), pltpu.VMEM((n_chain, dv + BF16_ROWS, cq), F32)]
    if bias is not None:
        in_specs.append(pl.BlockSpec((None, None, 1, s), lambda bi, h, qi: (bi, h, 0, 0)))
        args.append(bias)
        scratch.append(pltpu.VMEM((s, LANES), F32))
    return pl.pallas_call(
        functools.partial(_attn_kernel, tq=tq, cq=cq, scale=scale, has_bias=bias is not None),
        grid=(b, heads, s // tq),
        in_specs=in_specs,
        out_specs=pl.BlockSpec((None, tq, dv), lambda bi, h, qi: (bi, qi, h)),
        out_shape=jax.ShapeDtypeStruct((b, s, heads * dv), BF16),
        scratch_shapes=scratch,
        compiler_params=_params("arbitrary", "arbitrary", "arbitrary"),
        name="attn_bias" if bias is not None else "attn",
    )(*args)


def _dil_kernel(q_ref, k_ref, v_ref, kp_ref, vp_ref, o_ref, lse_ref, kbuf, vbuf, *, scale, group):
    planes, tl, _ = q_ref.shape
    nb = tl // BLOCK
    c2 = scale * LOG2E
    kbuf[:, :BLOCK, :] = kp_ref[...]
    kbuf[:, BLOCK:, :] = k_ref[...]
    vbuf[:, :BLOCK, :] = vp_ref[...]
    vbuf[:, BLOCK:, :] = v_ref[...]
    row = lax.broadcasted_iota(jnp.int32, (BLOCK, 2 * BLOCK), 0)
    col = lax.broadcasted_iota(jnp.int32, (BLOCK, 2 * BLOCK), 1)
    band = (col >= row) & (col <= row + BLOCK)
    lane = lax.broadcasted_iota(jnp.int32, (BLOCK, LANES), 1)
    ones = jnp.ones((2 * BLOCK, HEAD_DIM), BF16)
    heads = [slice(h * HEAD_DIM, (h + 1) * HEAD_DIM) for h in range(DIL_HEADS)]
    first_chunk = pl.program_id(2) == 0

    def locate(g, u):
        if nb % group == 0:
            gpp = nb // group
            p = g // gpp if planes > 1 else 0
            first = (g % gpp == 0) if u == 0 else None
            return p, (g % gpp) * group + u, first
        per = group // nb
        return g * per + u // nb, u % nb, (True if u % nb == 0 else None)

    def body(g, carry):
        work, masks = [], []
        for u in range(group):
            p, n, first = locate(g, u)
            start = n * BLOCK if isinstance(n, int) else pl.multiple_of(n * BLOCK, BLOCK)
            if first is None:
                masks.append(band)
            else:
                cond = first_chunk if first is True else jnp.logical_and(first_chunk, first)
                masks.append(band & (col >= jnp.where(cond, BLOCK, 0)))
            work += [(u, p, start, h) for h in range(DIL_HEADS)]

        def scores(u, p, start, h):
            q = q_ref[p, pl.ds(start, BLOCK), heads[h]]
            kc = kbuf[p, pl.ds(start, 2 * BLOCK), heads[h]]
            return lax.dot_general(q, kc, (((1,), (1,)), ((), ())), preferred_element_type=F32)

        ahead = 2 * DIL_HEADS
        pending = [scores(*w) for w in work[:ahead]]
        tiles = {}
        for i, (u, p, start, h) in enumerate(work):
            s = pending.pop(0)
            if i + ahead < len(work):
                pending.append(scores(*work[i + ahead]))
            s = jnp.where(masks[u], s, NEG)
            m = jnp.max(s, axis=-1, keepdims=True)
            e = jnp.exp2((s - m) * c2).astype(BF16)
            v_ext = jnp.concatenate([vbuf[p, pl.ds(start, 2 * BLOCK), heads[h]], ones], axis=1)
            pv = jnp.dot(e, v_ext, preferred_element_type=F32)
            denom = pv[:, HEAD_DIM:]
            o_ref[p, pl.ds(start, BLOCK), heads[h]] = pv[:, :HEAD_DIM] / denom
            tile = tiles.get(u, jnp.zeros((BLOCK, LANES), F32))
            tiles[u] = jnp.where(lane == h, m * scale + jnp.log(denom), tile)
            if h == DIL_HEADS - 1:
                lse_ref[p, pl.ds(start, BLOCK), :] = tiles.pop(u)
        return carry

    n_groups = planes * nb // group
    if n_groups == 1:
        body(0, 0)
    else:
        lax.fori_loop(0, n_groups, body, 0)


def _dil_branch(qkv, *, scale, rows=1024, group=4):
    b, dil, sub, w3 = qkv.shape
    w = w3 // 3
    tl = min(rows, sub)
    planes = min(rows // tl, dil)
    bpc = tl // BLOCK
    group = min(group, planes * bpc)
    assert (bpc % group == 0 or group % bpc == 0) and (planes * bpc) % group == 0
    own = lambda j: pl.BlockSpec((None, planes, tl, w), lambda bi, r, c: (bi, r, c, j))
    prev = lambda j: pl.BlockSpec((None, planes, BLOCK, w),
                                  lambda bi, r, c: (bi, r, jnp.maximum(c * bpc - 1, 0), j))
    return pl.pallas_call(
        functools.partial(_dil_kernel, scale=scale, group=group),
        grid=(b, dil // planes, sub // tl),
        in_specs=[own(0), own(1), own(2), prev(1), prev(2)],
        out_specs=[
            pl.BlockSpec((None, planes, tl, w), lambda bi, r, c: (bi, r, c, 0)),
            pl.BlockSpec((None, planes, tl, LANES), lambda bi, r, c: (bi, r, c, 0)),
        ],
        out_shape=[
            jax.ShapeDtypeStruct((b, dil, sub, w), F32),
            jax.ShapeDtypeStruct((b, dil, sub, LANES), F32),
        ],
        scratch_shapes=[pltpu.VMEM((planes, tl + BLOCK, w), BF16)] * 2,
        compiler_params=_params("parallel", "parallel", "parallel"),
        name=f"dilated_{dil}",
    )(qkv, qkv, qkv, qkv, qkv)


def _out_proj_kernel(x_ref, a_ref, b_ref, *rest, dils):
    nbr = len(dils)
    o_refs, l_refs = rest[:nbr], rest[nbr:2 * nbr]
    w_ref, y_ref = rest[2 * nbr:2 * nbr + 2]
    scr = rest[2 * nbr + 2:]
    tm = x_ref.shape[0]

    def natural(ref, dil, slot, j):
        sl = slice(j * LANES, (j + 1) * LANES)
        if dil == 1:
            return ref[0, :, sl]
        rows = tm // dil
        for r in range(dil):
            scr[slot][pl.ds(r, rows, stride=dil), :] = ref[r, :, sl]
        return scr[slot][...]

    lses = [natural(l_refs[i], dils[i], 2 * i, 0) for i in range(nbr)]
    m = functools.reduce(jnp.maximum, lses)
    es = [jnp.exp(l - m) for l in lses]
    tot = functools.reduce(lambda u, v: u + v, es)
    wts = [e / tot for e in es]
    parts = [a_ref[...], b_ref[...]]
    for h in range(DIL_HEADS):
        c = None
        for i in range(nbr):
            term = wts[i][:, h:h + 1] * natural(o_refs[i], dils[i], 2 * i + 1, h)
            c = term if c is None else c + term
        parts.append(c.astype(BF16))
    mixed = jnp.concatenate(parts, axis=-1)
    y_ref[...] = x_ref[...] + jnp.dot(mixed, w_ref[...], preferred_element_type=F32)


def _out_proj(x, a, bm, os_, ls_, w, *, tm=512):
    b, s, d = x.shape
    tm = min(tm, s)
    dils = tuple(o.shape[1] for o in os_)
    row = lambda arr: pl.BlockSpec((None, tm, arr.shape[2]), lambda bi, i: (bi, i, 0))
    plane = lambda arr: pl.BlockSpec((None, arr.shape[1], tm // arr.shape[1], arr.shape[3]),
                                     lambda bi, i: (bi, 0, i, 0))
    return pl.pallas_call(
        functools.partial(_out_proj_kernel, dils=dils),
        grid=(b, s // tm),
        in_specs=[row(x), row(a), row(bm)] + [plane(o) for o in os_] + [plane(l) for l in ls_]
                 + [pl.BlockSpec(w.shape, lambda bi, i: (0, 0))],
        out_specs=row(x),
        out_shape=jax.ShapeDtypeStruct((b, s, d), F32),
        scratch_shapes=[pltpu.VMEM((tm, LANES), F32)] * (2 * len(dils)),
        compiler_params=_params("parallel", "parallel"),
        name="out_proj",
    )(x, a, bm, *os_, *ls_, w)


def _rope_table(seq, dim, fill):
    half = dim // 2
    inv = 1.0 / (ROPE_THETA ** (jnp.arange(0, dim, 2, dtype=F32) / dim))
    ang = jnp.arange(seq, dtype=F32)[:, None] * inv[None, :]
    cos, sin = jnp.cos(ang), jnp.sin(ang)
    rest = LANES - dim
    zeros_h = jnp.zeros((seq, half), F32)
    zeros_r = jnp.zeros((seq, rest), F32)
    cos_t = jnp.concatenate([cos, cos, jnp.full((seq, rest), fill, F32)], axis=1)
    s_hi = jnp.concatenate([-sin, zeros_h, zeros_r], axis=1)
    s_lo = jnp.concatenate([zeros_h, sin, zeros_r], axis=1)
    return jnp.stack([cos_t, s_hi, s_lo], axis=0)


def kernel(x, ffn1_norm, ffn1_w_gate, ffn1_w_up, ffn1_w_down, mix_norm, w_in, fox_forget_bias, mla_q_norm, mla_kv_norm, mla_w_uq, mla_w_ukv, w_out, ffn2_norm, ffn2_w_gate, ffn2_w_up, ffn2_w_down, final_norm):
    b, s, d = x.shape
    depth = w_in.shape[0]
    t = b * s
    fw = FOX_HEADS * HEAD_DIM
    o_fl = 3 * fw
    o_cq = o_fl + FOX_HEADS
    o_kr = o_cq + MLA_Q_RANK + MLA_KV_RANK
    o_dq = o_kr + MLA_ROPE

    assert all(window == BLOCK * dil and s % (BLOCK * dil) == 0 for window, dil in DIL_BRANCHES)
    tab_mla = _rope_table(s, MLA_ROPE, 1.0)
    tab_dil = _rope_table(s, PARTIAL_ROPE, 1.0)

    pad_last = lambda w, n: jnp.pad(w, [(0, 0)] * (w.ndim - 1) + [(0, n - w.shape[-1])])
    w_fqk_all = w_in[:, :, :2 * fw].astype(BF16)
    w_fv_all = w_in[:, :, 2 * fw:o_fl].astype(BF16)
    w_fl_all = pad_last(w_in[:, :, o_fl:o_cq], LANES).astype(BF16)
    fbias_all = pad_last(fox_forget_bias.reshape(depth, 1, FOX_HEADS), LANES)
    w_c_all = jnp.concatenate([w_in[:, :, o_cq:o_kr], pad_last(w_in[:, :, o_kr:o_dq], LANES)], axis=-1).astype(BF16)
    w_dil_all = w_in[:, :, o_dq:].astype(BF16)
    wuq_all = mla_w_uq.reshape(depth, MLA_Q_RANK, MLA_HEADS, MLA_NOPE + MLA_ROPE)
    wuq_all = pad_last(wuq_all, 2 * LANES).reshape(depth, MLA_Q_RANK, MLA_HEADS * 2 * LANES).astype(BF16)
    wukv = mla_w_ukv.reshape(depth, MLA_KV_RANK, MLA_HEADS, MLA_NOPE + HEAD_DIM)
    w_k_all = wukv[..., :MLA_NOPE].reshape(depth, MLA_KV_RANK, MLA_HEADS * MLA_NOPE).astype(BF16)
    w_v_all = wukv[..., MLA_NOPE:].reshape(depth, MLA_KV_RANK, MLA_HEADS * HEAD_DIM).astype(BF16)
    w_out_all = w_out.astype(BF16)

    def ffn_tiles(w_gate, w_up, w_down, tf=FFN_TILE):
        nj = w_gate.shape[-1] // tf
        gate = w_gate.astype(BF16).reshape(depth, d, nj, tf)
        up = w_up.astype(BF16).reshape(depth, d, nj, tf)
        wgu = jnp.concatenate([gate, up], axis=-1).transpose(0, 2, 1, 3)
        return wgu, w_down.astype(BF16).reshape(depth, nj, tf, d)

    wgu1, wd1 = ffn_tiles(ffn1_w_gate, ffn1_w_up, ffn1_w_down)
    wgu2, wd2 = ffn_tiles(ffn2_w_gate, ffn2_w_up, ffn2_w_down)

    xf = x.reshape(t, d)
    for l in range(depth):
        last = l == depth - 1
        w_fqk, w_fv, w_fl, fbias = w_fqk_all[l], w_fv_all[l], w_fl_all[l], fbias_all[l]
        w_c, w_dil, wuq, w_k, w_v = w_c_all[l], w_dil_all[l], wuq_all[l], w_k_all[l], w_v_all[l]

        xf, hn = _ffn(xf, ffn1_norm[l], wgu1[l], wd1[l], mix_norm[l], emit_norm=True)

        hn = hn.reshape(b, s, d)
        fox_scale = HEAD_DIM ** -0.5
        fqk, fvt, logf = _fox_proj(hn, w_fqk, w_fv, w_fl, fbias)
        logf = logf[:, :, :FOX_HEADS].transpose(0, 2, 1).reshape(b * FOX_HEADS, s)
        key_bias = _key_bias(logf, fox_scale).reshape(b, FOX_HEADS, 1, s)
        out_a = _attention(fqk, fqk, fvt, key_bias, dk=HEAD_DIM, q_off=0, k_off=FOX_HEADS, scale=fox_scale)

        q_b, k_b, vt_b = _mla_proj(hn, w_c, mla_q_norm[l].reshape(1, -1), mla_kv_norm[l].reshape(1, -1),
                                   wuq, w_k, w_v, tab_mla)
        out_b = _attention(q_b, k_b, vt_b, None, dk=2 * LANES, q_off=0, k_off=0,
                           scale=(MLA_NOPE + MLA_ROPE) ** -0.5)

        dqkvs = _dil_proj(hn, w_dil, tab_dil, dils=tuple(dil for _, dil in DIL_BRANCHES))
        outs, lses = [], []
        for dqkv in dqkvs:
            o, lse = _dil_branch(dqkv, scale=HEAD_DIM ** -0.5)
            outs.append(o)
            lses.append(lse)

        xf = _out_proj(xf.reshape(b, s, d), out_a, out_b, outs, lses, w_out_all[l]).reshape(t, d)

        xf = _ffn(xf, ffn2_norm[l], wgu2[l], wd2[l], final_norm if last else None, final_norm=last)
    return xf.reshape(b, s, d)
```

```python
import functools

import jax
import jax.numpy as jnp
from jax import lax
from jax.experimental import pallas as pl
from jax.experimental.pallas import tpu as pltpu

F32 = jnp.float32
BF16 = jnp.bfloat16

LANES = 128
HEAD_DIM = 128
BLOCK = 128
EPS = 1e-6
ROPE_THETA = 500000.0
PARTIAL_ROPE = HEAD_DIM // 4
FOX_HEADS = 4
MLA_HEADS = 8
MLA_Q_RANK = 512
MLA_KV_RANK = 512
MLA_NOPE = 128
MLA_ROPE = 64
DIL_HEADS = 4
DIL_BRANCHES = ((128, 1), (512, 4), (2048, 16))
NEG = -1e30
LOG2E = 1.4426950408889634
VMEM_LIMIT = 56 * 1024 * 1024
FFN_TILE = 512


def _params(*sem):
    return pltpu.CompilerParams(dimension_semantics=sem, vmem_limit_bytes=VMEM_LIMIT)


def _rms(x, g):
    return x * lax.rsqrt(jnp.mean(x * x, axis=-1, keepdims=True) + EPS) * g


def _apply_rope(x, cos, s_hi, s_lo, half):
    return x * cos + pltpu.roll(x, LANES - half, 1) * s_hi + pltpu.roll(x, half, 1) * s_lo


def _ffn_kernel(*refs, emit_norm, final_norm):
    x_hbm, g_ref, wgu_ref, wd_ref = refs[:4]
    pos = 4
    g2_ref = None
    if emit_norm or final_norm:
        g2_ref = refs[pos]
        pos += 1
    o_ref = refs[pos]
    pos += 1
    hn_ref = None
    if emit_norm:
        hn_ref = refs[pos]
        pos += 1
    xn_ref, xbuf, sem = refs[pos:pos + 3]

    i, j = pl.program_id(0), pl.program_id(1)
    ni, nj = pl.num_programs(0), pl.num_programs(1)
    tm = xbuf.shape[0]
    tf = wd_ref.shape[0]

    def x_copy(tile):
        return pltpu.make_async_copy(x_hbm.at[pl.ds(pl.multiple_of(tile * tm, tm), tm), :], xbuf, sem)

    @pl.when(jnp.logical_and(i == 0, j == 0))
    def _():
        x_copy(0).start()

    @pl.when(j == 0)
    def _():
        x_copy(i).wait()
        x = xbuf[...]
        xn_ref[...] = _rms(x, g_ref[...]).astype(BF16)
        o_ref[...] = x

    @pl.when(jnp.logical_and(j == 1, i + 1 < ni))
    def _():
        x_copy(i + 1).start()

    gu = jnp.dot(xn_ref[...], wgu_ref[...], preferred_element_type=F32)
    gate, up = gu[:, :tf], gu[:, tf:]
    h = (gate * jax.nn.sigmoid(gate) * up * 0.5).astype(BF16)
    o_ref[...] += jnp.dot(h, wd_ref[...], preferred_element_type=F32)

    if emit_norm or final_norm:
        @pl.when(j == nj - 1)
        def _():
            y = _rms(o_ref[...], g2_ref[...])
            if emit_norm:
                hn_ref[...] = y.astype(BF16)
            else:
                o_ref[...] = y


def _ffn(x, g, wgu, wd, g2=None, *, emit_norm=False, final_norm=False, tm=1024, tf=FFN_TILE):
    t, d = x.shape
    nj = wd.shape[0] // tf
    tm = min(tm, t)
    assert nj >= 2
    in_specs = [
        pl.BlockSpec(memory_space=pl.ANY),
        pl.BlockSpec((1, d), lambda i, j: (0, 0)),
        pl.BlockSpec((d, 2 * tf), lambda i, j: (0, j)),
        pl.BlockSpec((tf, d), lambda i, j: (j, 0)),
    ]
    args = [x, g.reshape(1, d), wgu, wd]
    if emit_norm or final_norm:
        in_specs.append(pl.BlockSpec((1, d), lambda i, j: (0, 0)))
        args.append(g2.reshape(1, d))
    out_shape = [jax.ShapeDtypeStruct((t, d), F32)]
    out_specs = [pl.BlockSpec((tm, d), lambda i, j: (i, 0))]
    if emit_norm:
        out_shape.append(jax.ShapeDtypeStruct((t, d), BF16))
        out_specs.append(pl.BlockSpec((tm, d), lambda i, j: (i, 0)))
    res = pl.pallas_call(
        functools.partial(_ffn_kernel, emit_norm=emit_norm, final_norm=final_norm),
        grid=(t // tm, nj),
        in_specs=in_specs,
        out_specs=out_specs,
        out_shape=out_shape,
        scratch_shapes=[pltpu.VMEM((tm, d), BF16), pltpu.VMEM((tm, d), F32), pltpu.SemaphoreType.DMA],
        compiler_params=_params("arbitrary", "arbitrary"),
        name="ffn",
    )(*args)
    return res if emit_norm else res[0]


def _store_heads_transposed(vt_ref, v):
    for h in range(vt_ref.shape[0]):
        vt_ref[h] = v[:, h * HEAD_DIM:(h + 1) * HEAD_DIM].T.astype(BF16)


def _fox_proj_kernel(hn_ref, wqk_ref, wv_ref, wl_ref, b_ref, qk_ref, vt_ref, lf_ref):
    hn = hn_ref[...]
    qk_ref[...] = jnp.dot(hn, wqk_ref[...], preferred_element_type=F32).astype(BF16)
    _store_heads_transposed(vt_ref, jnp.dot(hn, wv_ref[...], preferred_element_type=F32))
    z = jnp.dot(hn, wl_ref[...], preferred_element_type=F32) + b_ref[...]
    lf_ref[...] = jnp.minimum(z, 0.0) - jnp.log1p(jnp.exp(-jnp.abs(z)))


def _fox_proj(hn, wqk, wv, wl, bias, *, tm=512):
    b, s, d = hn.shape
    tm = min(tm, s)
    n = wqk.shape[1]
    heads = wv.shape[1] // HEAD_DIM
    full = lambda a: pl.BlockSpec(a.shape, lambda bi, i: (0,) * a.ndim)
    return pl.pallas_call(
        _fox_proj_kernel,
        grid=(b, s // tm),
        in_specs=[pl.BlockSpec((None, tm, d), lambda bi, i: (bi, i, 0)), full(wqk), full(wv), full(wl), full(bias)],
        out_specs=[
            pl.BlockSpec((None, tm, n), lambda bi, i: (bi, i, 0)),
            pl.BlockSpec((None, heads, HEAD_DIM, tm), lambda bi, i: (bi, 0, 0, i)),
            pl.BlockSpec((None, tm, LANES), lambda bi, i: (bi, i, 0)),
        ],
        out_shape=[
            jax.ShapeDtypeStruct((b, s, n), BF16),
            jax.ShapeDtypeStruct((b, heads, HEAD_DIM, s), BF16),
            jax.ShapeDtypeStruct((b, s, LANES), F32),
        ],
        compiler_params=_params("parallel", "parallel"),
        name="fox_proj",
    )(hn, wqk, wv, wl, bias)


def _mla_proj_kernel(hn_ref, wc_ref, gq_ref, gkv_ref, wuq_ref, wk_ref, wv_ref, tab_ref,
                     q_ref, k_ref, vt_ref):
    hn = hn_ref[...]
    c = jnp.dot(hn, wc_ref[...], preferred_element_type=F32)
    cqn = _rms(c[:, :MLA_Q_RANK], gq_ref[...]).astype(BF16)
    ckvn = _rms(c[:, MLA_Q_RANK:MLA_Q_RANK + MLA_KV_RANK], gkv_ref[...]).astype(BF16)
    cos, s_hi, s_lo = tab_ref[0], tab_ref[1], tab_ref[2]
    half = MLA_ROPE // 2
    kr = _apply_rope(c[:, MLA_Q_RANK + MLA_KV_RANK:], cos, s_hi, s_lo, half).astype(BF16)
    q = jnp.dot(cqn, wuq_ref[...], preferred_element_type=F32)
    kn = jnp.dot(ckvn, wk_ref[...], preferred_element_type=F32)
    _store_heads_transposed(vt_ref, jnp.dot(ckvn, wv_ref[...], preferred_element_type=F32))
    for h in range(MLA_HEADS):
        lo = 2 * h * LANES
        q_ref[:, lo:lo + LANES] = q[:, lo:lo + LANES].astype(BF16)
        q_ref[:, lo + LANES:lo + 2 * LANES] = _apply_rope(
            q[:, lo + LANES:lo + 2 * LANES], cos, s_hi, s_lo, half).astype(BF16)
        k_ref[:, lo:lo + LANES] = kn[:, h * LANES:(h + 1) * LANES].astype(BF16)
        k_ref[:, lo + LANES:lo + 2 * LANES] = kr


def _mla_proj(hn, wc, gq, gkv, wuq, wk, wv, tab, *, tm=512):
    b, s, d = hn.shape
    tm = min(tm, s)
    nq = wuq.shape[1]
    heads = wv.shape[1] // HEAD_DIM
    full = lambda a: pl.BlockSpec(a.shape, lambda bi, i: (0,) * a.ndim)
    return pl.pallas_call(
        _mla_proj_kernel,
        grid=(b, s // tm),
        in_specs=[
            pl.BlockSpec((None, tm, d), lambda bi, i: (bi, i, 0)),
            full(wc), full(gq), full(gkv), full(wuq), full(wk), full(wv),
            pl.BlockSpec((3, tm, LANES), lambda bi, i: (0, i, 0)),
        ],
        out_specs=[
            pl.BlockSpec((None, tm, nq), lambda bi, i: (bi, i, 0)),
            pl.BlockSpec((None, tm, nq), lambda bi, i: (bi, i, 0)),
            pl.BlockSpec((None, heads, HEAD_DIM, tm), lambda bi, i: (bi, 0, 0, i)),
        ],
        out_shape=[
            jax.ShapeDtypeStruct((b, s, nq), BF16),
            jax.ShapeDtypeStruct((b, s, nq), BF16),
            jax.ShapeDtypeStruct((b, heads, HEAD_DIM, s), BF16),
        ],
        compiler_params=_params("parallel", "parallel"),
        name="mla_proj",
    )(hn, wc, gq, gkv, wuq, wk, wv, tab)


def _dil_proj_kernel(hn_ref, w_ref, tab_ref, *rest, dils):
    out_refs, scr = rest[:len(dils)], rest[len(dils):]
    y = jnp.dot(hn_ref[...], w_ref[...], preferred_element_type=F32)
    cos, s_hi, s_lo = tab_ref[0], tab_ref[1], tab_ref[2]
    half = PARTIAL_ROPE // 2
    n_rot = 2 * DIL_HEADS
    tm = y.shape[0]
    n_slab = y.shape[1] // LANES
    for j in range(n_slab):
        v = y[:, j * LANES:(j + 1) * LANES]
        scr[j][...] = _apply_rope(v, cos, s_hi, s_lo, half) if j < n_rot else v
    for o_ref, dil in zip(out_refs, dils):
        rows = tm // dil
        for r in range(dil):
            for j in range(n_slab):
                o_ref[r, :, j * LANES:(j + 1) * LANES] = scr[j][pl.ds(r, rows, stride=dil), :].astype(BF16)


def _dil_proj(hn, w, tab, *, dils, tm=512):
    b, s, d = hn.shape
    tm = min(tm, s)
    n = w.shape[1]
    return pl.pallas_call(
        functools.partial(_dil_proj_kernel, dils=dils),
        grid=(b, s // tm),
        in_specs=[
            pl.BlockSpec((None, tm, d), lambda bi, i: (bi, i, 0)),
            pl.BlockSpec((d, n), lambda bi, i: (0, 0)),
            pl.BlockSpec((3, tm, LANES), lambda bi, i: (0, i, 0)),
        ],
        out_specs=[pl.BlockSpec((None, dil, tm // dil, n), lambda bi, i: (bi, 0, i, 0)) for dil in dils],
        out_shape=[jax.ShapeDtypeStruct((b, dil, s // dil, n), BF16) for dil in dils],
        scratch_shapes=[pltpu.VMEM((tm, LANES), F32)] * (n // LANES),
        compiler_params=_params("parallel", "parallel"),
        name="dil_proj",
    )(hn, w, tab)


def _cumsum_kernel(x_ref, o_ref, *, out_scale):
    x = x_ref[...]
    n = x.shape[-1]
    lane = lax.broadcasted_iota(jnp.int32, x.shape, 1)
    k = 1
    while k < n:
        x = x + jnp.where(lane >= k, pltpu.roll(x, k, 1), 0.0)
        k *= 2
    o_ref[...] = x * out_scale


def _key_bias(x, scale):
    return pl.pallas_call(
        functools.partial(_cumsum_kernel, out_scale=-1.0 / scale),
        out_shape=jax.ShapeDtypeStruct(x.shape, F32),
        compiler_params=pltpu.CompilerParams(vmem_limit_bytes=VMEM_LIMIT),
        name="forget_cumsum",
    )(x)


def _attn_kernel(*refs, tq, cq, tk, ahead, scale, has_bias):
    if has_bias:
        q_ref, k_ref, vt_ref, b_ref, o_ref, m_ref, acc_ref, brep_ref = refs
    else:
        q_ref, k_ref, vt_ref, o_ref, m_ref, acc_ref = refs
    qi = pl.program_id(2)
    n_chain = tq // cq
    dv = vt_ref.shape[0]
    c2 = scale * LOG2E

    if has_bias:
        @pl.when(qi == 0)
        def _():
            for j in range(brep_ref.shape[0] // tq):
                rows = jnp.broadcast_to(b_ref[:, j * tq:(j + 1) * tq], (LANES, tq))
                brep_ref[j * tq:(j + 1) * tq, :] = rows.T

    m_ref[...] = jnp.full(m_ref.shape, NEG, F32)
    acc_ref[...] = jnp.zeros(acc_ref.shape, F32)

    def scores(c, ks, size):
        q = q_ref[c * cq:(c + 1) * cq, :]
        k = k_ref[pl.ds(ks, size), :]
        return lax.dot_general(k, q, (((1,), (1,)), ((), ())), preferred_element_type=F32)

    def update(c, st, ks, size, diagonal):
        if has_bias:
            st = st + jnp.tile(brep_ref[pl.ds(ks, size), :], (1, cq // LANES))
        if diagonal:
            key = lax.broadcasted_iota(jnp.int32, st.shape, 0)
            qry = lax.broadcasted_iota(jnp.int32, st.shape, 1)
            st = jnp.where(key <= qry, st, NEG)
        m_prev = m_ref[c]
        m_new = jnp.maximum(m_prev, jnp.max(st, axis=0, keepdims=True))
        pt = jnp.exp2((st - m_new) * c2).astype(BF16)
        alpha = jnp.exp2((m_prev - m_new) * c2)
        v_ext = jnp.concatenate([vt_ref[:, pl.ds(ks, size)], jnp.ones((dv, size), BF16)], axis=0)
        acc_ref[c] = alpha * acc_ref[c] + jnp.dot(v_ext, pt, preferred_element_type=F32)
        m_ref[c] = m_new

    def run(work, ahead):
        pending = [scores(c, ks, size) for c, ks, size, _ in work[:ahead]]
        for i, (c, ks, size, diagonal) in enumerate(work):
            st = pending.pop(0)
            if i + ahead < len(work):
                cn, ksn, sizen, _ = work[i + ahead]
                pending.append(scores(cn, ksn, sizen))
            update(c, st, ks, size, diagonal)

    def body(j, carry):
        run([(c, pl.multiple_of(j * tq + part * tk, tk), tk, False)
             for part in range(tq // tk) for c in range(n_chain)], ahead)
        return carry

    lax.fori_loop(0, qi, body, 0)
    run([(c, pl.multiple_of(qi * tq + j * cq, cq), cq, c == j)
         for j in range(n_chain) for c in range(j, n_chain)], 4)
    for c in range(n_chain):
        acc = acc_ref[c]
        o_ref[c * cq:(c + 1) * cq, :] = (acc[:dv] / acc[dv:]).T.astype(o_ref.dtype)


def _attention(q_arr, k_arr, vt_arr, bias, *, dk, q_off, k_off, scale, tq=1024, cq=256, tk=512, ahead=2):
    b, s, _ = q_arr.shape
    heads, dv = vt_arr.shape[1], vt_arr.shape[2]
    tq = min(tq, s)
    cq = min(cq, tq)
    n_chain = tq // cq
    in_specs = [
        pl.BlockSpec((None, tq, dk), lambda bi, h, qi: (bi, qi, q_off + h)),
        pl.BlockSpec((None, s, dk), lambda bi, h, qi: (bi, 0, k_off + h)),
        pl.BlockSpec((None, None, dv, s), lambda bi, h, qi: (bi, h, 0, 0)),
    ]
    args = [q_arr, k_arr, vt_arr]
    scratch = [pltpu.VMEM((n_chain, 1, cq), F32), pltpu.VMEM((n_chain, 2 * dv, cq), F32)]
    if bias is not None:
        in_specs.append(pl.BlockSpec((None, None, 1, s), lambda bi, h, qi: (bi, h, 0, 0)))
        args.append(bias)
        scratch.append(pltpu.VMEM((s, LANES), F32))
    return pl.pallas_call(
        functools.partial(_attn_kernel, tq=tq, cq=cq, tk=min(tk, tq), ahead=ahead, scale=scale,
                          has_bias=bias is not None),
        grid=(b, heads, s // tq),
        in_specs=in_specs,
        out_specs=pl.BlockSpec((None, tq, dv), lambda bi, h, qi: (bi, qi, h)),
        out_shape=jax.ShapeDtypeStruct((b, s, heads * dv), BF16),
        scratch_shapes=scratch,
        compiler_params=_params("arbitrary", "arbitrary", "arbitrary"),
        name="attn_bias" if bias is not None else "attn",
    )(*args)


def _dil_kernel(q_ref, k_ref, v_ref, kp_ref, vp_ref, o_ref, lse_ref, kbuf, vbuf, *, scale, group):
    planes, tl, _ = q_ref.shape
    nb = tl // BLOCK
    c2 = scale * LOG2E
    kbuf[:, :BLOCK, :] = kp_ref[...]
    kbuf[:, BLOCK:, :] = k_ref[...]
    vbuf[:, :BLOCK, :] = vp_ref[...]
    vbuf[:, BLOCK:, :] = v_ref[...]
    row = lax.broadcasted_iota(jnp.int32, (BLOCK, 2 * BLOCK), 0)
    col = lax.broadcasted_iota(jnp.int32, (BLOCK, 2 * BLOCK), 1)
    band = (col >= row) & (col <= row + BLOCK)
    lane = lax.broadcasted_iota(jnp.int32, (BLOCK, LANES), 1)
    ones = jnp.ones((2 * BLOCK, HEAD_DIM), BF16)
    heads = [slice(h * HEAD_DIM, (h + 1) * HEAD_DIM) for h in range(DIL_HEADS)]
    first_chunk = pl.program_id(2) == 0

    def locate(g, u):
        if nb % group == 0:
            gpp = nb // group
            p = g // gpp if planes > 1 else 0
            first = (g % gpp == 0) if u == 0 else None
            return p, (g % gpp) * group + u, first
        per = group // nb
        return g * per + u // nb, u % nb, (True if u % nb == 0 else None)

    def body(g, carry):
        work, masks = [], []
        for u in range(group):
            p, n, first = locate(g, u)
            start = n * BLOCK if isinstance(n, int) else pl.multiple_of(n * BLOCK, BLOCK)
            if first is None:
                masks.append(band)
            else:
                cond = first_chunk if first is True else jnp.logical_and(first_chunk, first)
                masks.append(band & (col >= jnp.where(cond, BLOCK, 0)))
            work += [(u, p, start, h) for h in range(DIL_HEADS)]

        def scores(u, p, start, h):
            q = q_ref[p, pl.ds(start, BLOCK), heads[h]]
            kc = kbuf[p, pl.ds(start, 2 * BLOCK), heads[h]]
            return lax.dot_general(q, kc, (((1,), (1,)), ((), ())), preferred_element_type=F32)

        ahead = 2 * DIL_HEADS
        pending = [scores(*w) for w in work[:ahead]]
        tiles = {}
        for i, (u, p, start, h) in enumerate(work):
            s = pending.pop(0)
            if i + ahead < len(work):
                pending.append(scores(*work[i + ahead]))
            s = jnp.where(masks[u], s, NEG)
            m = jnp.max(s, axis=-1, keepdims=True)
            e = jnp.exp2((s - m) * c2).astype(BF16)
            v_ext = jnp.concatenate([vbuf[p, pl.ds(start, 2 * BLOCK), heads[h]], ones], axis=1)
            pv = jnp.dot(e, v_ext, preferred_element_type=F32)
            denom = pv[:, HEAD_DIM:]
            o_ref[p, pl.ds(start, BLOCK), heads[h]] = pv[:, :HEAD_DIM] / denom
            tile = tiles.get(u, jnp.zeros((BLOCK, LANES), F32))
            tiles[u] = jnp.where(lane == h, m * scale + jnp.log(denom), tile)
            if h == DIL_HEADS - 1:
                lse_ref[p, pl.ds(start, BLOCK), :] = tiles.pop(u)
        return carry

    n_groups = planes * nb // group
    if n_groups == 1:
        body(0, 0)
    else:
        lax.fori_loop(0, n_groups, body, 0)


def _dil_branch(qkv, *, scale, rows=1024, group=4):
    b, dil, sub, w3 = qkv.shape
    w = w3 // 3
    tl = min(rows, sub)
    planes = min(rows // tl, dil)
    bpc = tl // BLOCK
    group = min(group, planes * bpc)
    assert (bpc % group == 0 or group % bpc == 0) and (planes * bpc) % group == 0
    own = lambda j: pl.BlockSpec((None, planes, tl, w), lambda bi, r, c: (bi, r, c, j))
    prev = lambda j: pl.BlockSpec((None, planes, BLOCK, w),
                                  lambda bi, r, c: (bi, r, jnp.maximum(c * bpc - 1, 0), j))
    return pl.pallas_call(
        functools.partial(_dil_kernel, scale=scale, group=group),
        grid=(b, dil // planes, sub // tl),
        in_specs=[own(0), own(1), own(2), prev(1), prev(2)],
        out_specs=[
            pl.BlockSpec((None, planes, tl, w), lambda bi, r, c: (bi, r, c, 0)),
            pl.BlockSpec((None, planes, tl, LANES), lambda bi, r, c: (bi, r, c, 0)),
        ],
        out_shape=[
            jax.ShapeDtypeStruct((b, dil, sub, w), F32),
            jax.ShapeDtypeStruct((b, dil, sub, LANES), F32),
        ],
        scratch_shapes=[pltpu.VMEM((planes, tl + BLOCK, w), BF16)] * 2,
        compiler_params=_params("parallel", "parallel", "parallel"),
        name=f"dilated_{dil}",
    )(qkv, qkv, qkv, qkv, qkv)


def _out_proj_kernel(x_ref, a_ref, b_ref, *rest, dils):
    nbr = len(dils)
    o_refs, l_refs = rest[:nbr], rest[nbr:2 * nbr]
    w_ref, y_ref = rest[2 * nbr:2 * nbr + 2]
    scr = rest[2 * nbr + 2:]
    tm = x_ref.shape[0]

    def natural(ref, dil, slot, j):
        sl = slice(j * LANES, (j + 1) * LANES)
        if dil == 1:
            return ref[0, :, sl]
        rows = tm // dil
        for r in range(dil):
            scr[slot][pl.ds(r, rows, stride=dil), :] = ref[r, :, sl]
        return scr[slot][...]

    lses = [natural(l_refs[i], dils[i], 2 * i, 0) for i in range(nbr)]
    m = functools.reduce(jnp.maximum, lses)
    es = [jnp.exp(l - m) for l in lses]
    tot = functools.reduce(lambda u, v: u + v, es)
    wts = [e / tot for e in es]
    parts = [a_ref[...], b_ref[...]]
    for h in range(DIL_HEADS):
        c = None
        for i in range(nbr):
            term = wts[i][:, h:h + 1] * natural(o_refs[i], dils[i], 2 * i + 1, h)
            c = term if c is None else c + term
        parts.append(c.astype(BF16))
    mixed = jnp.concatenate(parts, axis=-1)
    y_ref[...] = x_ref[...] + jnp.dot(mixed, w_ref[...], preferred_element_type=F32)


def _out_proj(x, a, bm, os_, ls_, w, *, tm=512):
    b, s, d = x.shape
    tm = min(tm, s)
    dils = tuple(o.shape[1] for o in os_)
    row = lambda arr: pl.BlockSpec((None, tm, arr.shape[2]), lambda bi, i: (bi, i, 0))
    plane = lambda arr: pl.BlockSpec((None, arr.shape[1], tm // arr.shape[1], arr.shape[3]),
                                     lambda bi, i: (bi, 0, i, 0))
    return pl.pallas_call(
        functools.partial(_out_proj_kernel, dils=dils),
        grid=(b, s // tm),
        in_specs=[row(x), row(a), row(bm)] + [plane(o) for o in os_] + [plane(l) for l in ls_]
                 + [pl.BlockSpec(w.shape, lambda bi, i: (0, 0))],
        out_specs=row(x),
        out_shape=jax.ShapeDtypeStruct((b, s, d), F32),
        scratch_shapes=[pltpu.VMEM((tm, LANES), F32)] * (2 * len(dils)),
        compiler_params=_params("parallel", "parallel"),
        name="out_proj",
    )(x, a, bm, *os_, *ls_, w)


def _rope_table(seq, dim, fill):
    half = dim // 2
    inv = 1.0 / (ROPE_THETA ** (jnp.arange(0, dim, 2, dtype=F32) / dim))
    ang = jnp.arange(seq, dtype=F32)[:, None] * inv[None, :]
    cos, sin = jnp.cos(ang), jnp.sin(ang)
    rest = LANES - dim
    zeros_h = jnp.zeros((seq, half), F32)
    zeros_r = jnp.zeros((seq, rest), F32)
    cos_t = jnp.concatenate([cos, cos, jnp.full((seq, rest), fill, F32)], axis=1)
    s_hi = jnp.concatenate([-sin, zeros_h, zeros_r], axis=1)
    s_lo = jnp.concatenate([zeros_h, sin, zeros_r], axis=1)
    return jnp.stack([cos_t, s_hi, s_lo], axis=0)


def kernel(x, ffn1_norm, ffn1_w_gate, ffn1_w_up, ffn1_w_down, mix_norm, w_in, fox_forget_bias, mla_q_norm, mla_kv_norm, mla_w_uq, mla_w_ukv, w_out, ffn2_norm, ffn2_w_gate, ffn2_w_up, ffn2_w_down, final_norm):
    b, s, d = x.shape
    depth = w_in.shape[0]
    t = b * s
    fw = FOX_HEADS * HEAD_DIM
    o_fl = 3 * fw
    o_cq = o_fl + FOX_HEADS
    o_kr = o_cq + MLA_Q_RANK + MLA_KV_RANK
    o_dq = o_kr + MLA_ROPE

    assert all(window == BLOCK * dil and s % (BLOCK * dil) == 0 for window, dil in DIL_BRANCHES)
    tab_mla = _rope_table(s, MLA_ROPE, 1.0)
    tab_dil = _rope_table(s, PARTIAL_ROPE, 1.0)

    pad_last = lambda w, n: jnp.pad(w, [(0, 0)] * (w.ndim - 1) + [(0, n - w.shape[-1])])
    w_fqk_all = w_in[:, :, :2 * fw].astype(BF16)
    w_fv_all = w_in[:, :, 2 * fw:o_fl].astype(BF16)
    w_fl_all = pad_last(w_in[:, :, o_fl:o_cq], LANES).astype(BF16)
    fbias_all = pad_last(fox_forget_bias.reshape(depth, 1, FOX_HEADS), LANES)
    w_c_all = jnp.concatenate([w_in[:, :, o_cq:o_kr], pad_last(w_in[:, :, o_kr:o_dq], LANES)], axis=-1).astype(BF16)
    w_dil_all = w_in[:, :, o_dq:].astype(BF16)
    wuq_all = mla_w_uq.reshape(depth, MLA_Q_RANK, MLA_HEADS, MLA_NOPE + MLA_ROPE)
    wuq_all = pad_last(wuq_all, 2 * LANES).reshape(depth, MLA_Q_RANK, MLA_HEADS * 2 * LANES).astype(BF16)
    wukv = mla_w_ukv.reshape(depth, MLA_KV_RANK, MLA_HEADS, MLA_NOPE + HEAD_DIM)
    w_k_all = wukv[..., :MLA_NOPE].reshape(depth, MLA_KV_RANK, MLA_HEADS * MLA_NOPE).astype(BF16)
    w_v_all = wukv[..., MLA_NOPE:].reshape(depth, MLA_KV_RANK, MLA_HEADS * HEAD_DIM).astype(BF16)
    w_out_all = w_out.astype(BF16)

    def ffn_tiles(w_gate, w_up, w_down, tf=FFN_TILE):
        nj = w_gate.shape[-1] // tf
        gate = w_gate.astype(BF16).reshape(depth, d, nj, tf)
        up = w_up.astype(BF16).reshape(depth, d, nj, tf)
        wgu = jnp.concatenate([gate, up], axis=-1).reshape(depth, d, 2 * nj * tf)
        return wgu, w_down.astype(BF16)

    wgu1, wd1 = ffn_tiles(ffn1_w_gate, ffn1_w_up, ffn1_w_down)
    wgu2, wd2 = ffn_tiles(ffn2_w_gate, ffn2_w_up, ffn2_w_down)

    xf = x.reshape(t, d)
    for l in range(depth):
        last = l == depth - 1
        w_fqk, w_fv, w_fl, fbias = w_fqk_all[l], w_fv_all[l], w_fl_all[l], fbias_all[l]
        w_c, w_dil, wuq, w_k, w_v = w_c_all[l], w_dil_all[l], wuq_all[l], w_k_all[l], w_v_all[l]

        xf, hn = _ffn(xf, ffn1_norm[l], wgu1[l], wd1[l], mix_norm[l], emit_norm=True)

        hn = hn.reshape(b, s, d)
        fox_scale = HEAD_DIM ** -0.5
        fqk, fvt, logf = _fox_proj(hn, w_fqk, w_fv, w_fl, fbias)
        logf = logf[:, :, :FOX_HEADS].transpose(0, 2, 1).reshape(b * FOX_HEADS, s)
        key_bias = _key_bias(logf, fox_scale).reshape(b, FOX_HEADS, 1, s)
        out_a = _attention(fqk, fqk, fvt, key_bias, dk=HEAD_DIM, q_off=0, k_off=FOX_HEADS, scale=fox_scale)

        q_b, k_b, vt_b = _mla_proj(hn, w_c, mla_q_norm[l].reshape(1, -1), mla_kv_norm[l].reshape(1, -1),
                                   wuq, w_k, w_v, tab_mla)
        out_b = _attention(q_b, k_b, vt_b, None, dk=2 * LANES, q_off=0, k_off=0,
                           scale=(MLA_NOPE + MLA_ROPE) ** -0.5)

        dqkvs = _dil_proj(hn, w_dil, tab_dil, dils=tuple(dil for _, dil in DIL_BRANCHES))
        outs, lses = [], []
        for dqkv in dqkvs:
            o, lse = _dil_branch(dqkv, scale=HEAD_DIM ** -0.5)
            outs.append(o)
            lses.append(lse)

        xf = _out_proj(xf.reshape(b, s, d), out_a, out_b, outs, lses, w_out_all[l]).reshape(t, d)

        xf = _ffn(xf, ffn2_norm[l], wgu2[l], wd2[l], final_norm if last else None, final_norm=last)
    return xf.reshape(b, s, d)
```

```python
import functools

import jax
import jax.numpy as jnp
from jax import lax
from jax.experimental import pallas as pl
from jax.experimental.pallas import tpu as pltpu

F32 = jnp.float32
BF16 = jnp.bfloat16

LANES = 128
HEAD_DIM = 128
BLOCK = 128
EPS = 1e-6
ROPE_THETA = 500000.0
PARTIAL_ROPE = HEAD_DIM // 4
FOX_HEADS = 4
MLA_HEADS = 8
MLA_Q_RANK = 512
MLA_KV_RANK = 512
MLA_NOPE = 128
MLA_ROPE = 64
DIL_HEADS = 4
DIL_BRANCHES = ((128, 1), (512, 4), (2048, 16))
NEG = -1e30
LOG2E = 1.4426950408889634
VMEM_LIMIT = 56 * 1024 * 1024
FFN_TILE = 512


def _params(*sem):
    return pltpu.CompilerParams(dimension_semantics=sem, vmem_limit_bytes=VMEM_LIMIT)


def _rms(x, g):
    return x * lax.rsqrt(jnp.mean(x * x, axis=-1, keepdims=True) + EPS) * g


def _apply_rope(x, cos, s_hi, s_lo, half):
    return x * cos + pltpu.roll(x, LANES - half, 1) * s_hi + pltpu.roll(x, half, 1) * s_lo


def _ffn_kernel(*refs, emit_norm, final_norm):
    x_hbm, g_ref, wg_ref, wu_ref, wd_ref = refs[:5]
    pos = 5
    g2_ref = None
    if emit_norm or final_norm:
        g2_ref = refs[pos]
        pos += 1
    o_ref = refs[pos]
    pos += 1
    hn_ref = None
    if emit_norm:
        hn_ref = refs[pos]
        pos += 1
    xn_ref, xbuf, sem = refs[pos:pos + 3]

    i, j = pl.program_id(0), pl.program_id(1)
    ni, nj = pl.num_programs(0), pl.num_programs(1)
    tm = xbuf.shape[0]

    def x_copy(tile):
        return pltpu.make_async_copy(x_hbm.at[pl.ds(pl.multiple_of(tile * tm, tm), tm), :], xbuf, sem)

    @pl.when(jnp.logical_and(i == 0, j == 0))
    def _():
        x_copy(0).start()

    @pl.when(j == 0)
    def _():
        x_copy(i).wait()
        x = xbuf[...]
        xn_ref[...] = _rms(x, g_ref[...]).astype(BF16)
        o_ref[...] = x

    @pl.when(jnp.logical_and(j == 1, i + 1 < ni))
    def _():
        x_copy(i + 1).start()

    xn = xn_ref[...]
    gate = jnp.dot(xn, wg_ref[...], preferred_element_type=F32)
    up = jnp.dot(xn, wu_ref[...], preferred_element_type=F32)
    h = (gate * jax.nn.sigmoid(gate) * up * 0.5).astype(BF16)
    o_ref[...] += jnp.dot(h, wd_ref[...], preferred_element_type=F32)

    if emit_norm or final_norm:
        @pl.when(j == nj - 1)
        def _():
            y = _rms(o_ref[...], g2_ref[...])
            if emit_norm:
                hn_ref[...] = y.astype(BF16)
            else:
                o_ref[...] = y


def _ffn(x, g, wg, wu, wd, g2=None, *, emit_norm=False, final_norm=False, tm=1024, tf=FFN_TILE):
    t, d = x.shape
    nj = wd.shape[0] // tf
    tm = min(tm, t)
    assert nj >= 2
    in_specs = [
        pl.BlockSpec(memory_space=pl.ANY),
        pl.BlockSpec((1, d), lambda i, j: (0, 0)),
        pl.BlockSpec((d, tf), lambda i, j: (0, j)),
        pl.BlockSpec((d, tf), lambda i, j: (0, j)),
        pl.BlockSpec((tf, d), lambda i, j: (j, 0)),
    ]
    args = [x, g.reshape(1, d), wg, wu, wd]
    if emit_norm or final_norm:
        in_specs.append(pl.BlockSpec((1, d), lambda i, j: (0, 0)))
        args.append(g2.reshape(1, d))
    out_shape = [jax.ShapeDtypeStruct((t, d), F32)]
    out_specs = [pl.BlockSpec((tm, d), lambda i, j: (i, 0))]
    if emit_norm:
        out_shape.append(jax.ShapeDtypeStruct((t, d), BF16))
        out_specs.append(pl.BlockSpec((tm, d), lambda i, j: (i, 0)))
    res = pl.pallas_call(
        functools.partial(_ffn_kernel, emit_norm=emit_norm, final_norm=final_norm),
        grid=(t // tm, nj),
        in_specs=in_specs,
        out_specs=out_specs,
        out_shape=out_shape,
        scratch_shapes=[pltpu.VMEM((tm, d), BF16), pltpu.VMEM((tm, d), F32), pltpu.SemaphoreType.DMA],
        compiler_params=_params("arbitrary", "arbitrary"),
        name="ffn",
    )(*args)
    return res if emit_norm else res[0]


def _store_heads_transposed(vt_ref, v):
    for h in range(vt_ref.shape[0]):
        vt_ref[h] = v[:, h * HEAD_DIM:(h + 1) * HEAD_DIM].T.astype(BF16)


def _fox_proj_kernel(hn_ref, wqk_ref, wv_ref, wl_ref, b_ref, qk_ref, vt_ref, lf_ref):
    hn = hn_ref[...]
    qk_ref[...] = jnp.dot(hn, wqk_ref[...], preferred_element_type=F32).astype(BF16)
    _store_heads_transposed(vt_ref, jnp.dot(hn, wv_ref[...], preferred_element_type=F32))
    z = jnp.dot(hn, wl_ref[...], preferred_element_type=F32) + b_ref[...]
    lf_ref[...] = jnp.minimum(z, 0.0) - jnp.log1p(jnp.exp(-jnp.abs(z)))


def _fox_proj(hn, wqk, wv, wl, bias, *, tm=512):
    b, s, d = hn.shape
    tm = min(tm, s)
    n = wqk.shape[1]
    heads = wv.shape[1] // HEAD_DIM
    full = lambda a: pl.BlockSpec(a.shape, lambda bi, i: (0,) * a.ndim)
    return pl.pallas_call(
        _fox_proj_kernel,
        grid=(b, s // tm),
        in_specs=[pl.BlockSpec((None, tm, d), lambda bi, i: (bi, i, 0)), full(wqk), full(wv), full(wl), full(bias)],
        out_specs=[
            pl.BlockSpec((None, tm, n), lambda bi, i: (bi, i, 0)),
            pl.BlockSpec((None, heads, HEAD_DIM, tm), lambda bi, i: (bi, 0, 0, i)),
            pl.BlockSpec((None, tm, LANES), lambda bi, i: (bi, i, 0)),
        ],
        out_shape=[
            jax.ShapeDtypeStruct((b, s, n), BF16),
            jax.ShapeDtypeStruct((b, heads, HEAD_DIM, s), BF16),
            jax.ShapeDtypeStruct((b, s, LANES), F32),
        ],
        compiler_params=_params("parallel", "parallel"),
        name="fox_proj",
    )(hn, wqk, wv, wl, bias)


def _mla_proj_kernel(hn_ref, wc_ref, gq_ref, gkv_ref, wuq_ref, wk_ref, wv_ref, tab_ref,
                     q_ref, k_ref, vt_ref):
    hn = hn_ref[...]
    c = jnp.dot(hn, wc_ref[...], preferred_element_type=F32)
    cqn = _rms(c[:, :MLA_Q_RANK], gq_ref[...]).astype(BF16)
    ckvn = _rms(c[:, MLA_Q_RANK:MLA_Q_RANK + MLA_KV_RANK], gkv_ref[...]).astype(BF16)
    cos, s_hi, s_lo = tab_ref[0], tab_ref[1], tab_ref[2]
    half = MLA_ROPE // 2
    kr = _apply_rope(c[:, MLA_Q_RANK + MLA_KV_RANK:], cos, s_hi, s_lo, half).astype(BF16)
    q = jnp.dot(cqn, wuq_ref[...], preferred_element_type=F32)
    kn = jnp.dot(ckvn, wk_ref[...], preferred_element_type=F32)
    _store_heads_transposed(vt_ref, jnp.dot(ckvn, wv_ref[...], preferred_element_type=F32))
    for h in range(MLA_HEADS):
        lo = 2 * h * LANES
        q_ref[:, lo:lo + LANES] = q[:, lo:lo + LANES].astype(BF16)
        q_ref[:, lo + LANES:lo + 2 * LANES] = _apply_rope(
            q[:, lo + LANES:lo + 2 * LANES], cos, s_hi, s_lo, half).astype(BF16)
        k_ref[:, lo:lo + LANES] = kn[:, h * LANES:(h + 1) * LANES].astype(BF16)
        k_ref[:, lo + LANES:lo + 2 * LANES] = kr


def _mla_proj(hn, wc, gq, gkv, wuq, wk, wv, tab, *, tm=512):
    b, s, d = hn.shape
    tm = min(tm, s)
    nq = wuq.shape[1]
    heads = wv.shape[1] // HEAD_DIM
    full = lambda a: pl.BlockSpec(a.shape, lambda bi, i: (0,) * a.ndim)
    return pl.pallas_call(
        _mla_proj_kernel,
        grid=(b, s // tm),
        in_specs=[
            pl.BlockSpec((None, tm, d), lambda bi, i: (bi, i, 0)),
            full(wc), full(gq), full(gkv), full(wuq), full(wk), full(wv),
            pl.BlockSpec((3, tm, LANES), lambda bi, i: (0, i, 0)),
        ],
        out_specs=[
            pl.BlockSpec((None, tm, nq), lambda bi, i: (bi, i, 0)),
            pl.BlockSpec((None, tm, nq), lambda bi, i: (bi, i, 0)),
            pl.BlockSpec((None, heads, HEAD_DIM, tm), lambda bi, i: (bi, 0, 0, i)),
        ],
        out_shape=[
            jax.ShapeDtypeStruct((b, s, nq), BF16),
            jax.ShapeDtypeStruct((b, s, nq), BF16),
            jax.ShapeDtypeStruct((b, heads, HEAD_DIM, s), BF16),
        ],
        compiler_params=_params("parallel", "parallel"),
        name="mla_proj",
    )(hn, wc, gq, gkv, wuq, wk, wv, tab)


def _dil_proj_kernel(hn_ref, w_ref, tab_ref, *rest, dils):
    out_refs, scr = rest[:len(dils)], rest[len(dils):]
    y = jnp.dot(hn_ref[...], w_ref[...], preferred_element_type=F32)
    cos, s_hi, s_lo = tab_ref[0], tab_ref[1], tab_ref[2]
    half = PARTIAL_ROPE // 2
    n_rot = 2 * DIL_HEADS
    tm = y.shape[0]
    n_slab = y.shape[1] // LANES
    for j in range(n_slab):
        v = y[:, j * LANES:(j + 1) * LANES]
        scr[j][...] = _apply_rope(v, cos, s_hi, s_lo, half) if j < n_rot else v
    for o_ref, dil in zip(out_refs, dils):
        rows = tm // dil
        for r in range(dil):
            for j in range(n_slab):
                o_ref[r, :, j * LANES:(j + 1) * LANES] = scr[j][pl.ds(r, rows, stride=dil), :].astype(BF16)


def _dil_proj(hn, w, tab, *, dils, tm=512):
    b, s, d = hn.shape
    tm = min(tm, s)
    n = w.shape[1]
    return pl.pallas_call(
        functools.partial(_dil_proj_kernel, dils=dils),
        grid=(b, s // tm),
        in_specs=[
            pl.BlockSpec((None, tm, d), lambda bi, i: (bi, i, 0)),
            pl.BlockSpec((d, n), lambda bi, i: (0, 0)),
            pl.BlockSpec((3, tm, LANES), lambda bi, i: (0, i, 0)),
        ],
        out_specs=[pl.BlockSpec((None, dil, tm // dil, n), lambda bi, i: (bi, 0, i, 0)) for dil in dils],
        out_shape=[jax.ShapeDtypeStruct((b, dil, s // dil, n), BF16) for dil in dils],
        scratch_shapes=[pltpu.VMEM((tm, LANES), F32)] * (n // LANES),
        compiler_params=_params("parallel", "parallel"),
        name="dil_proj",
    )(hn, w, tab)


def _cumsum_kernel(x_ref, o_ref, *, out_scale):
    x = x_ref[...]
    n = x.shape[-1]
    lane = lax.broadcasted_iota(jnp.int32, x.shape, 1)
    k = 1
    while k < n:
        x = x + jnp.where(lane >= k, pltpu.roll(x, k, 1), 0.0)
        k *= 2
    o_ref[...] = x * out_scale


def _key_bias(x, scale):
    return pl.pallas_call(
        functools.partial(_cumsum_kernel, out_scale=-1.0 / scale),
        out_shape=jax.ShapeDtypeStruct(x.shape, F32),
        compiler_params=pltpu.CompilerParams(vmem_limit_bytes=VMEM_LIMIT),
        name="forget_cumsum",
    )(x)


def _attn_kernel(*refs, tq, cq, tk, ahead, scale, has_bias):
    if has_bias:
        q_ref, k_ref, vt_ref, b_ref, o_ref, m_ref, acc_ref, brep_ref = refs
    else:
        q_ref, k_ref, vt_ref, o_ref, m_ref, acc_ref = refs
    qi = pl.program_id(2)
    n_chain = tq // cq
    dv = vt_ref.shape[0]
    c2 = scale * LOG2E

    if has_bias:
        @pl.when(qi == 0)
        def _():
            for j in range(brep_ref.shape[0] // tq):
                rows = jnp.broadcast_to(b_ref[:, j * tq:(j + 1) * tq], (LANES, tq))
                brep_ref[j * tq:(j + 1) * tq, :] = rows.T

    m_ref[...] = jnp.full(m_ref.shape, NEG, F32)
    acc_ref[...] = jnp.zeros(acc_ref.shape, F32)

    def scores(c, ks, size):
        q = q_ref[c * cq:(c + 1) * cq, :]
        k = k_ref[pl.ds(ks, size), :]
        return lax.dot_general(k, q, (((1,), (1,)), ((), ())), preferred_element_type=F32)

    def update(c, st, ks, size, diagonal):
        if has_bias:
            st = st + jnp.tile(brep_ref[pl.ds(ks, size), :], (1, cq // LANES))
        if diagonal:
            key = lax.broadcasted_iota(jnp.int32, (cq, cq), 0)
            qry = lax.broadcasted_iota(jnp.int32, (cq, cq), 1)
            last = jnp.where(key <= qry, st[size - cq:], NEG)
            st = last if size == cq else jnp.concatenate([st[:size - cq], last], axis=0)
        m_prev = m_ref[c]
        m_new = jnp.maximum(m_prev, jnp.max(st, axis=0, keepdims=True))
        pt = jnp.exp2((st - m_new) * c2).astype(BF16)
        alpha = jnp.exp2((m_prev - m_new) * c2)
        v_ext = jnp.concatenate([vt_ref[:, pl.ds(ks, size)], jnp.ones((dv, size), BF16)], axis=0)
        acc_ref[c] = alpha * acc_ref[c] + jnp.dot(v_ext, pt, preferred_element_type=F32)
        m_ref[c] = m_new

    def run(work, ahead):
        pending = [scores(c, ks, size) for c, ks, size, _ in work[:ahead]]
        for i, (c, ks, size, diagonal) in enumerate(work):
            st = pending.pop(0)
            if i + ahead < len(work):
                cn, ksn, sizen, _ = work[i + ahead]
                pending.append(scores(cn, ksn, sizen))
            update(c, st, ks, size, diagonal)

    def body(j, carry):
        run([(c, pl.multiple_of(j * tq + part * tk, tk), tk, False)
             for part in range(tq // tk) for c in range(n_chain)], ahead)
        return carry

    lax.fori_loop(0, qi, body, 0)
    run([(c, pl.multiple_of(qi * tq + j * cq, cq), cq, c == j)
         for j in range(n_chain) for c in range(j, n_chain)], 4)
    for c in range(n_chain):
        acc = acc_ref[c]
        o_ref[c * cq:(c + 1) * cq, :] = (acc[:dv] / acc[dv:]).T.astype(o_ref.dtype)


def _attention(q_arr, k_arr, vt_arr, bias, *, dk, q_off, k_off, scale, tq=1024, cq=256, tk=512, ahead=2):
    b, s, _ = q_arr.shape
    heads, dv = vt_arr.shape[1], vt_arr.shape[2]
    tq = min(tq, s)
    cq = min(cq, tq)
    n_chain = tq // cq
    in_specs = [
        pl.BlockSpec((None, tq, dk), lambda bi, h, qi: (bi, qi, q_off + h)),
        pl.BlockSpec((None, s, dk), lambda bi, h, qi: (bi, 0, k_off + h)),
        pl.BlockSpec((None, None, dv, s), lambda bi, h, qi: (bi, h, 0, 0)),
    ]
    args = [q_arr, k_arr, vt_arr]
    scratch = [pltpu.VMEM((n_chain, 1, cq), F32), pltpu.VMEM((n_chain, 2 * dv, cq), F32)]
    if bias is not None:
        in_specs.append(pl.BlockSpec((None, None, 1, s), lambda bi, h, qi: (bi, h, 0, 0)))
        args.append(bias)
        scratch.append(pltpu.VMEM((s, LANES), F32))
    return pl.pallas_call(
        functools.partial(_attn_kernel, tq=tq, cq=cq, tk=min(tk, tq), ahead=ahead, scale=scale,
                          has_bias=bias is not None),
        grid=(b, heads, s // tq),
        in_specs=in_specs,
        out_specs=pl.BlockSpec((None, tq, dv), lambda bi, h, qi: (bi, qi, h)),
        out_shape=jax.ShapeDtypeStruct((b, s, heads * dv), BF16),
        scratch_shapes=scratch,
        compiler_params=_params("arbitrary", "arbitrary", "arbitrary"),
        name="attn_bias" if bias is not None else "attn",
    )(*args)


def _dil_kernel(q_ref, k_ref, v_ref, kp_ref, vp_ref, o_ref, lse_ref, kbuf, vbuf, *, scale, group):
    planes, tl, _ = q_ref.shape
    nb = tl // BLOCK
    c2 = scale * LOG2E
    kbuf[:, :BLOCK, :] = kp_ref[...]
    kbuf[:, BLOCK:, :] = k_ref[...]
    vbuf[:, :BLOCK, :] = vp_ref[...]
    vbuf[:, BLOCK:, :] = v_ref[...]
    row = lax.broadcasted_iota(jnp.int32, (BLOCK, 2 * BLOCK), 0)
    col = lax.broadcasted_iota(jnp.int32, (BLOCK, 2 * BLOCK), 1)
    band = (col >= row) & (col <= row + BLOCK)
    lane = lax.broadcasted_iota(jnp.int32, (BLOCK, LANES), 1)
    ones = jnp.ones((2 * BLOCK, HEAD_DIM), BF16)
    heads = [slice(h * HEAD_DIM, (h + 1) * HEAD_DIM) for h in range(DIL_HEADS)]
    first_chunk = pl.program_id(2) == 0

    def locate(g, u):
        if nb % group == 0:
            gpp = nb // group
            p = g // gpp if planes > 1 else 0
            first = (g % gpp == 0) if u == 0 else None
            return p, (g % gpp) * group + u, first
        per = group // nb
        return g * per + u // nb, u % nb, (True if u % nb == 0 else None)

    def body(g, carry):
        work, masks = [], []
        for u in range(group):
            p, n, first = locate(g, u)
            start = n * BLOCK if isinstance(n, int) else pl.multiple_of(n * BLOCK, BLOCK)
            if first is None:
                masks.append(band)
            else:
                cond = first_chunk if first is True else jnp.logical_and(first_chunk, first)
                masks.append(band & (col >= jnp.where(cond, BLOCK, 0)))
            work += [(u, p, start, h) for h in range(DIL_HEADS)]

        def scores(u, p, start, h):
            q = q_ref[p, pl.ds(start, BLOCK), heads[h]]
            kc = kbuf[p, pl.ds(start, 2 * BLOCK), heads[h]]
            return lax.dot_general(q, kc, (((1,), (1,)), ((), ())), preferred_element_type=F32)

        ahead = 2 * DIL_HEADS
        pending = [scores(*w) for w in work[:ahead]]
        tiles = {}
        for i, (u, p, start, h) in enumerate(work):
            s = pending.pop(0)
            if i + ahead < len(work):
                pending.append(scores(*work[i + ahead]))
            s = jnp.where(masks[u], s, NEG)
            m = jnp.max(s, axis=-1, keepdims=True)
            e = jnp.exp2((s - m) * c2).astype(BF16)
            v_ext = jnp.concatenate([vbuf[p, pl.ds(start, 2 * BLOCK), heads[h]], ones], axis=1)
            pv = jnp.dot(e, v_ext, preferred_element_type=F32)
            denom = pv[:, HEAD_DIM:]
            o_ref[p, pl.ds(start, BLOCK), heads[h]] = pv[:, :HEAD_DIM] / denom
            tile = tiles.get(u, jnp.zeros((BLOCK, LANES), F32))
            tiles[u] = jnp.where(lane == h, m * scale + jnp.log(denom), tile)
            if h == DIL_HEADS - 1:
                lse_ref[p, pl.ds(start, BLOCK), :] = tiles.pop(u)
        return carry

    n_groups = planes * nb // group
    if n_groups == 1:
        body(0, 0)
    else:
        lax.fori_loop(0, n_groups, body, 0)


def _dil_branch(qkv, *, scale, rows=1024, group=4):
    b, dil, sub, w3 = qkv.shape
    w = w3 // 3
    tl = min(rows, sub)
    planes = min(rows // tl, dil)
    bpc = tl // BLOCK
    group = min(group, planes * bpc)
    assert (bpc % group == 0 or group % bpc == 0) and (planes * bpc) % group == 0
    own = lambda j: pl.BlockSpec((None, planes, tl, w), lambda bi, r, c: (bi, r, c, j))
    prev = lambda j: pl.BlockSpec((None, planes, BLOCK, w),
                                  lambda bi, r, c: (bi, r, jnp.maximum(c * bpc - 1, 0), j))
    return pl.pallas_call(
        functools.partial(_dil_kernel, scale=scale, group=group),
        grid=(b, dil // planes, sub // tl),
        in_specs=[own(0), own(1), own(2), prev(1), prev(2)],
        out_specs=[
            pl.BlockSpec((None, planes, tl, w), lambda bi, r, c: (bi, r, c, 0)),
            pl.BlockSpec((None, planes, tl, LANES), lambda bi, r, c: (bi, r, c, 0)),
        ],
        out_shape=[
            jax.ShapeDtypeStruct((b, dil, sub, w), F32),
            jax.ShapeDtypeStruct((b, dil, sub, LANES), F32),
        ],
        scratch_shapes=[pltpu.VMEM((planes, tl + BLOCK, w), BF16)] * 2,
        compiler_params=_params("parallel", "parallel", "parallel"),
        name=f"dilated_{dil}",
    )(qkv, qkv, qkv, qkv, qkv)


def _out_proj_kernel(x_ref, a_ref, b_ref, *rest, dils):
    nbr = len(dils)
    o_refs, l_refs = rest[:nbr], rest[nbr:2 * nbr]
    w_ref, y_ref = rest[2 * nbr:2 * nbr + 2]
    scr = rest[2 * nbr + 2:]
    tm = x_ref.shape[0]

    def natural(ref, dil, slot, j):
        sl = slice(j * LANES, (j + 1) * LANES)
        if dil == 1:
            return ref[0, :, sl]
        rows = tm // dil
        for r in range(dil):
            scr[slot][pl.ds(r, rows, stride=dil), :] = ref[r, :, sl]
        return scr[slot][...]

    lses = [natural(l_refs[i], dils[i], 2 * i, 0) for i in range(nbr)]
    m = functools.reduce(jnp.maximum, lses)
    es = [jnp.exp(l - m) for l in lses]
    tot = functools.reduce(lambda u, v: u + v, es)
    wts = [e / tot for e in es]
    parts = [a_ref[...], b_ref[...]]
    for h in range(DIL_HEADS):
        c = None
        for i in range(nbr):
            term = wts[i][:, h:h + 1] * natural(o_refs[i], dils[i], 2 * i + 1, h)
            c = term if c is None else c + term
        parts.append(c.astype(BF16))
    mixed = jnp.concatenate(parts, axis=-1)
    y_ref[...] = x_ref[...] + jnp.dot(mixed, w_ref[...], preferred_element_type=F32)


def _out_proj(x, a, bm, os_, ls_, w, *, tm=512):
    b, s, d = x.shape
    tm = min(tm, s)
    dils = tuple(o.shape[1] for o in os_)
    row = lambda arr: pl.BlockSpec((None, tm, arr.shape[2]), lambda bi, i: (bi, i, 0))
    plane = lambda arr: pl.BlockSpec((None, arr.shape[1], tm // arr.shape[1], arr.shape[3]),
                                     lambda bi, i: (bi, 0, i, 0))
    return pl.pallas_call(
        functools.partial(_out_proj_kernel, dils=dils),
        grid=(b, s // tm),
        in_specs=[row(x), row(a), row(bm)] + [plane(o) for o in os_] + [plane(l) for l in ls_]
                 + [pl.BlockSpec(w.shape, lambda bi, i: (0, 0))],
        out_specs=row(x),
        out_shape=jax.ShapeDtypeStruct((b, s, d), F32),
        scratch_shapes=[pltpu.VMEM((tm, LANES), F32)] * (2 * len(dils)),
        compiler_params=_params("parallel", "parallel"),
        name="out_proj",
    )(x, a, bm, *os_, *ls_, w)


def _rope_table(seq, dim, fill):
    half = dim // 2
    inv = 1.0 / (ROPE_THETA ** (jnp.arange(0, dim, 2, dtype=F32) / dim))
    ang = jnp.arange(seq, dtype=F32)[:, None] * inv[None, :]
    cos, sin = jnp.cos(ang), jnp.sin(ang)
    rest = LANES - dim
    zeros_h = jnp.zeros((seq, half), F32)
    zeros_r = jnp.zeros((seq, rest), F32)
    cos_t = jnp.concatenate([cos, cos, jnp.full((seq, rest), fill, F32)], axis=1)
    s_hi = jnp.concatenate([-sin, zeros_h, zeros_r], axis=1)
    s_lo = jnp.concatenate([zeros_h, sin, zeros_r], axis=1)
    return jnp.stack([cos_t, s_hi, s_lo], axis=0)


def kernel(x, ffn1_norm, ffn1_w_gate, ffn1_w_up, ffn1_w_down, mix_norm, w_in, fox_forget_bias, mla_q_norm, mla_kv_norm, mla_w_uq, mla_w_ukv, w_out, ffn2_norm, ffn2_w_gate, ffn2_w_up, ffn2_w_down, final_norm):
    b, s, d = x.shape
    depth = w_in.shape[0]
    t = b * s
    fw = FOX_HEADS * HEAD_DIM
    o_fl = 3 * fw
    o_cq = o_fl + FOX_HEADS
    o_kr = o_cq + MLA_Q_RANK + MLA_KV_RANK
    o_dq = o_kr + MLA_ROPE

    assert all(window == BLOCK * dil and s % (BLOCK * dil) == 0 for window, dil in DIL_BRANCHES)
    tab_mla = _rope_table(s, MLA_ROPE, 1.0)
    tab_dil = _rope_table(s, PARTIAL_ROPE, 1.0)

    pad_last = lambda w, n: jnp.pad(w, [(0, 0)] * (w.ndim - 1) + [(0, n - w.shape[-1])])
    w_fqk_all = w_in[:, :, :2 * fw].astype(BF16)
    w_fv_all = w_in[:, :, 2 * fw:o_fl].astype(BF16)
    w_fl_all = pad_last(w_in[:, :, o_fl:o_cq], LANES).astype(BF16)
    fbias_all = pad_last(fox_forget_bias.reshape(depth, 1, FOX_HEADS), LANES)
    w_c_all = jnp.concatenate([w_in[:, :, o_cq:o_kr], pad_last(w_in[:, :, o_kr:o_dq], LANES)], axis=-1).astype(BF16)
    w_dil_all = w_in[:, :, o_dq:].astype(BF16)
    wuq_all = mla_w_uq.reshape(depth, MLA_Q_RANK, MLA_HEADS, MLA_NOPE + MLA_ROPE)
    wuq_all = pad_last(wuq_all, 2 * LANES).reshape(depth, MLA_Q_RANK, MLA_HEADS * 2 * LANES).astype(BF16)
    wukv = mla_w_ukv.reshape(depth, MLA_KV_RANK, MLA_HEADS, MLA_NOPE + HEAD_DIM)
    w_k_all = wukv[..., :MLA_NOPE].reshape(depth, MLA_KV_RANK, MLA_HEADS * MLA_NOPE).astype(BF16)
    w_v_all = wukv[..., MLA_NOPE:].reshape(depth, MLA_KV_RANK, MLA_HEADS * HEAD_DIM).astype(BF16)
    w_out_all = w_out.astype(BF16)

    ffn1_w = [w.astype(BF16) for w in (ffn1_w_gate, ffn1_w_up, ffn1_w_down)]
    ffn2_w = [w.astype(BF16) for w in (ffn2_w_gate, ffn2_w_up, ffn2_w_down)]

    xf = x.reshape(t, d)
    for l in range(depth):
        last = l == depth - 1
        w_fqk, w_fv, w_fl, fbias = w_fqk_all[l], w_fv_all[l], w_fl_all[l], fbias_all[l]
        w_c, w_dil, wuq, w_k, w_v = w_c_all[l], w_dil_all[l], wuq_all[l], w_k_all[l], w_v_all[l]

        xf, hn = _ffn(xf, ffn1_norm[l], *(w[l] for w in ffn1_w), mix_norm[l], emit_norm=True)

        hn = hn.reshape(b, s, d)
        fox_scale = HEAD_DIM ** -0.5
        fqk, fvt, logf = _fox_proj(hn, w_fqk, w_fv, w_fl, fbias)
        logf = logf[:, :, :FOX_HEADS].transpose(0, 2, 1).reshape(b * FOX_HEADS, s)
        key_bias = _key_bias(logf, fox_scale).reshape(b, FOX_HEADS, 1, s)
        out_a = _attention(fqk, fqk, fvt, key_bias, dk=HEAD_DIM, q_off=0, k_off=FOX_HEADS, scale=fox_scale)

        q_b, k_b, vt_b = _mla_proj(hn, w_c, mla_q_norm[l].reshape(1, -1), mla_kv_norm[l].reshape(1, -1),
                                   wuq, w_k, w_v, tab_mla)
        out_b = _attention(q_b, k_b, vt_b, None, dk=2 * LANES, q_off=0, k_off=0,
                           scale=(MLA_NOPE + MLA_ROPE) ** -0.5)

        dqkvs = _dil_proj(hn, w_dil, tab_dil, dils=tuple(dil for _, dil in DIL_BRANCHES))
        outs, lses = [], []
        for dqkv in dqkvs:
            o, lse = _dil_branch(dqkv, scale=HEAD_DIM ** -0.5)
            outs.append(o)
            lses.append(lse)

        xf = _out_proj(xf.reshape(b, s, d), out_a, out_b, outs, lses, w_out_all[l]).reshape(t, d)

        xf = _ffn(xf, ffn2_norm[l], *(w[l] for w in ffn2_w), final_norm if last else None, final_norm=last)
    return xf.reshape(b, s, d)
```

```python
import functools

import jax
import jax.numpy as jnp
from jax import lax
from jax.experimental import pallas as pl
from jax.experimental.pallas import tpu as pltpu

F32 = jnp.float32
BF16 = jnp.bfloat16

LANES = 128
HEAD_DIM = 128
BLOCK = 128
EPS = 1e-6
ROPE_THETA = 500000.0
PARTIAL_ROPE = HEAD_DIM // 4
FOX_HEADS = 4
MLA_HEADS = 8
MLA_Q_RANK = 512
MLA_KV_RANK = 512
MLA_NOPE = 128
MLA_ROPE = 64
DIL_HEADS = 4
DIL_BRANCHES = ((128, 1), (512, 4), (2048, 16))
NEG = -1e30
LOG2E = 1.4426950408889634
EXP2_HEADROOM = 100.0
VMEM_LIMIT = 56 * 1024 * 1024
FFN_TILE = 512


def _params(*sem):
    return pltpu.CompilerParams(dimension_semantics=sem, vmem_limit_bytes=VMEM_LIMIT)


def _rms(x, g):
    return x * lax.rsqrt(jnp.mean(x * x, axis=-1, keepdims=True) + EPS) * g


def _apply_rope(x, cos, s_hi, s_lo, half):
    return x * cos + pltpu.roll(x, LANES - half, 1) * s_hi + pltpu.roll(x, half, 1) * s_lo


def _ffn_kernel(*refs, emit_norm, final_norm):
    x_hbm, g_ref, wg_ref, wu_ref, wd_ref = refs[:5]
    pos = 5
    g2_ref = None
    if emit_norm or final_norm:
        g2_ref = refs[pos]
        pos += 1
    o_ref = refs[pos]
    pos += 1
    hn_ref = None
    if emit_norm:
        hn_ref = refs[pos]
        pos += 1
    xn_ref, xbuf, sem = refs[pos:pos + 3]

    i, j = pl.program_id(0), pl.program_id(1)
    ni, nj = pl.num_programs(0), pl.num_programs(1)
    tm = xbuf.shape[0]

    def x_copy(tile):
        return pltpu.make_async_copy(x_hbm.at[pl.ds(pl.multiple_of(tile * tm, tm), tm), :], xbuf, sem)

    @pl.when(jnp.logical_and(i == 0, j == 0))
    def _():
        x_copy(0).start()

    @pl.when(j == 0)
    def _():
        x_copy(i).wait()
        x = xbuf[...]
        xn_ref[...] = _rms(x, g_ref[...]).astype(BF16)
        o_ref[...] = x

    @pl.when(jnp.logical_and(j == 1, i + 1 < ni))
    def _():
        x_copy(i + 1).start()

    xn = xn_ref[...]
    gate = jnp.dot(xn, wg_ref[...], preferred_element_type=F32)
    up = jnp.dot(xn, wu_ref[...], preferred_element_type=F32)
    h = (gate * jax.nn.sigmoid(gate) * up * 0.5).astype(BF16)
    o_ref[...] += jnp.dot(h, wd_ref[...], preferred_element_type=F32)

    if emit_norm or final_norm:
        @pl.when(j == nj - 1)
        def _():
            y = _rms(o_ref[...], g2_ref[...])
            if emit_norm:
                hn_ref[...] = y.astype(BF16)
            else:
                o_ref[...] = y


def _ffn(x, g, wg, wu, wd, g2=None, *, emit_norm=False, final_norm=False, tm=1024, tf=FFN_TILE):
    t, d = x.shape
    nj = wd.shape[0] // tf
    tm = min(tm, t)
    assert nj >= 2
    in_specs = [
        pl.BlockSpec(memory_space=pl.ANY),
        pl.BlockSpec((1, d), lambda i, j: (0, 0)),
        pl.BlockSpec((d, tf), lambda i, j: (0, j)),
        pl.BlockSpec((d, tf), lambda i, j: (0, j)),
        pl.BlockSpec((tf, d), lambda i, j: (j, 0)),
    ]
    args = [x, g.reshape(1, d), wg, wu, wd]
    if emit_norm or final_norm:
        in_specs.append(pl.BlockSpec((1, d), lambda i, j: (0, 0)))
        args.append(g2.reshape(1, d))
    out_shape = [jax.ShapeDtypeStruct((t, d), F32)]
    out_specs = [pl.BlockSpec((tm, d), lambda i, j: (i, 0))]
    if emit_norm:
        out_shape.append(jax.ShapeDtypeStruct((t, d), BF16))
        out_specs.append(pl.BlockSpec((tm, d), lambda i, j: (i, 0)))
    res = pl.pallas_call(
        functools.partial(_ffn_kernel, emit_norm=emit_norm, final_norm=final_norm),
        grid=(t // tm, nj),
        in_specs=in_specs,
        out_specs=out_specs,
        out_shape=out_shape,
        scratch_shapes=[pltpu.VMEM((tm, d), BF16), pltpu.VMEM((tm, d), F32), pltpu.SemaphoreType.DMA],
        compiler_params=_params("arbitrary", "arbitrary"),
        name="ffn",
    )(*args)
    return res if emit_norm else res[0]


def _store_heads_transposed(vt_ref, v):
    for h in range(vt_ref.shape[0]):
        vt_ref[h] = v[:, h * HEAD_DIM:(h + 1) * HEAD_DIM].T.astype(BF16)


def _fox_proj_kernel(hn_ref, wqk_ref, wv_ref, wl_ref, b_ref, qk_ref, vt_ref, lf_ref):
    hn = hn_ref[...]
    qk_ref[...] = jnp.dot(hn, wqk_ref[...], preferred_element_type=F32).astype(BF16)
    _store_heads_transposed(vt_ref, jnp.dot(hn, wv_ref[...], preferred_element_type=F32))
    z = jnp.dot(hn, wl_ref[...], preferred_element_type=F32) + b_ref[...]
    lf_ref[...] = jnp.minimum(z, 0.0) - jnp.log1p(jnp.exp(-jnp.abs(z)))


def _fox_proj(hn, wqk, wv, wl, bias, *, tm=512):
    b, s, d = hn.shape
    tm = min(tm, s)
    n = wqk.shape[1]
    heads = wv.shape[1] // HEAD_DIM
    full = lambda a: pl.BlockSpec(a.shape, lambda bi, i: (0,) * a.ndim)
    return pl.pallas_call(
        _fox_proj_kernel,
        grid=(b, s // tm),
        in_specs=[pl.BlockSpec((None, tm, d), lambda bi, i: (bi, i, 0)), full(wqk), full(wv), full(wl), full(bias)],
        out_specs=[
            pl.BlockSpec((None, tm, n), lambda bi, i: (bi, i, 0)),
            pl.BlockSpec((None, heads, HEAD_DIM, tm), lambda bi, i: (bi, 0, 0, i)),
            pl.BlockSpec((None, tm, LANES), lambda bi, i: (bi, i, 0)),
        ],
        out_shape=[
            jax.ShapeDtypeStruct((b, s, n), BF16),
            jax.ShapeDtypeStruct((b, heads, HEAD_DIM, s), BF16),
            jax.ShapeDtypeStruct((b, s, LANES), F32),
        ],
        compiler_params=_params("parallel", "parallel"),
        name="fox_proj",
    )(hn, wqk, wv, wl, bias)


def _mla_proj_kernel(hn_ref, wc_ref, gq_ref, gkv_ref, wuq_ref, wk_ref, wv_ref, tab_ref,
                     q_ref, k_ref, vt_ref):
    hn = hn_ref[...]
    c = jnp.dot(hn, wc_ref[...], preferred_element_type=F32)
    cqn = _rms(c[:, :MLA_Q_RANK], gq_ref[...]).astype(BF16)
    ckvn = _rms(c[:, MLA_Q_RANK:MLA_Q_RANK + MLA_KV_RANK], gkv_ref[...]).astype(BF16)
    cos, s_hi, s_lo = tab_ref[0], tab_ref[1], tab_ref[2]
    half = MLA_ROPE // 2
    kr = _apply_rope(c[:, MLA_Q_RANK + MLA_KV_RANK:], cos, s_hi, s_lo, half).astype(BF16)
    q = jnp.dot(cqn, wuq_ref[...], preferred_element_type=F32)
    kn = jnp.dot(ckvn, wk_ref[...], preferred_element_type=F32)
    _store_heads_transposed(vt_ref, jnp.dot(ckvn, wv_ref[...], preferred_element_type=F32))
    for h in range(MLA_HEADS):
        lo = 2 * h * LANES
        q_ref[:, lo:lo + LANES] = q[:, lo:lo + LANES].astype(BF16)
        q_ref[:, lo + LANES:lo + 2 * LANES] = _apply_rope(
            q[:, lo + LANES:lo + 2 * LANES], cos, s_hi, s_lo, half).astype(BF16)
        k_ref[:, lo:lo + LANES] = kn[:, h * LANES:(h + 1) * LANES].astype(BF16)
        k_ref[:, lo + LANES:lo + 2 * LANES] = kr


def _mla_proj(hn, wc, gq, gkv, wuq, wk, wv, tab, *, tm=512):
    b, s, d = hn.shape
    tm = min(tm, s)
    nq = wuq.shape[1]
    heads = wv.shape[1] // HEAD_DIM
    full = lambda a: pl.BlockSpec(a.shape, lambda bi, i: (0,) * a.ndim)
    return pl.pallas_call(
        _mla_proj_kernel,
        grid=(b, s // tm),
        in_specs=[
            pl.BlockSpec((None, tm, d), lambda bi, i: (bi, i, 0)),
            full(wc), full(gq), full(gkv), full(wuq), full(wk), full(wv),
            pl.BlockSpec((3, tm, LANES), lambda bi, i: (0, i, 0)),
        ],
        out_specs=[
            pl.BlockSpec((None, tm, nq), lambda bi, i: (bi, i, 0)),
            pl.BlockSpec((None, tm, nq), lambda bi, i: (bi, i, 0)),
            pl.BlockSpec((None, heads, HEAD_DIM, tm), lambda bi, i: (bi, 0, 0, i)),
        ],
        out_shape=[
            jax.ShapeDtypeStruct((b, s, nq), BF16),
            jax.ShapeDtypeStruct((b, s, nq), BF16),
            jax.ShapeDtypeStruct((b, heads, HEAD_DIM, s), BF16),
        ],
        compiler_params=_params("parallel", "parallel"),
        name="mla_proj",
    )(hn, wc, gq, gkv, wuq, wk, wv, tab)


def _dil_proj_kernel(hn_ref, w_ref, tab_ref, *rest, dils):
    out_refs, scr = rest[:len(dils)], rest[len(dils):]
    y = jnp.dot(hn_ref[...], w_ref[...], preferred_element_type=F32)
    cos, s_hi, s_lo = tab_ref[0], tab_ref[1], tab_ref[2]
    half = PARTIAL_ROPE // 2
    n_rot = 2 * DIL_HEADS
    tm = y.shape[0]
    n_slab = y.shape[1] // LANES
    for j in range(n_slab):
        v = y[:, j * LANES:(j + 1) * LANES]
        scr[j][...] = _apply_rope(v, cos, s_hi, s_lo, half) if j < n_rot else v
    for o_ref, dil in zip(out_refs, dils):
        rows = tm // dil
        for r in range(dil):
            for j in range(n_slab):
                o_ref[r, :, j * LANES:(j + 1) * LANES] = scr[j][pl.ds(r, rows, stride=dil), :].astype(BF16)


def _dil_proj(hn, w, tab, *, dils, tm=512):
    b, s, d = hn.shape
    tm = min(tm, s)
    n = w.shape[1]
    return pl.pallas_call(
        functools.partial(_dil_proj_kernel, dils=dils),
        grid=(b, s // tm),
        in_specs=[
            pl.BlockSpec((None, tm, d), lambda bi, i: (bi, i, 0)),
            pl.BlockSpec((d, n), lambda bi, i: (0, 0)),
            pl.BlockSpec((3, tm, LANES), lambda bi, i: (0, i, 0)),
        ],
        out_specs=[pl.BlockSpec((None, dil, tm // dil, n), lambda bi, i: (bi, 0, i, 0)) for dil in dils],
        out_shape=[jax.ShapeDtypeStruct((b, dil, s // dil, n), BF16) for dil in dils],
        scratch_shapes=[pltpu.VMEM((tm, LANES), F32)] * (n // LANES),
        compiler_params=_params("parallel", "parallel"),
        name="dil_proj",
    )(hn, w, tab)


def _cumsum_kernel(x_ref, o_ref, *, out_scale):
    x = x_ref[...]
    n = x.shape[-1]
    lane = lax.broadcasted_iota(jnp.int32, x.shape, 1)
    k = 1
    while k < n:
        x = x + jnp.where(lane >= k, pltpu.roll(x, k, 1), 0.0)
        k *= 2
    o_ref[...] = x * out_scale


def _key_bias(x, scale):
    return pl.pallas_call(
        functools.partial(_cumsum_kernel, out_scale=-1.0 / scale),
        out_shape=jax.ShapeDtypeStruct(x.shape, F32),
        compiler_params=pltpu.CompilerParams(vmem_limit_bytes=VMEM_LIMIT),
        name="forget_cumsum",
    )(x)


def _attn_kernel(*refs, tq, cq, tk, ahead, scale, has_bias):
    if has_bias:
        q_ref, k_ref, vt_ref, b_ref, o_ref, m_ref, acc_ref, worst_ref, brep_ref = refs
    else:
        q_ref, k_ref, vt_ref, o_ref, m_ref, acc_ref, worst_ref = refs
    qi = pl.program_id(2)
    n_chain = tq // cq
    dv = vt_ref.shape[0]
    c2 = scale * LOG2E

    if has_bias:
        @pl.when(qi == 0)
        def _():
            for j in range(brep_ref.shape[0] // tq):
                rows = jnp.broadcast_to(b_ref[:, j * tq:(j + 1) * tq], (LANES, tq))
                brep_ref[j * tq:(j + 1) * tq, :] = rows.T

    def scores(c, ks, size):
        q = q_ref[c * cq:(c + 1) * cq, :]
        k = k_ref[pl.ds(ks, size), :]
        return lax.dot_general(k, q, (((1,), (1,)), ((), ())), preferred_element_type=F32)

    def update(c, st, ks, size, diagonal, exact):
        if has_bias:
            st = st + jnp.tile(brep_ref[pl.ds(ks, size), :], (1, cq // LANES))
        if diagonal:
            key = lax.broadcasted_iota(jnp.int32, st.shape, 0)
            qry = lax.broadcasted_iota(jnp.int32, st.shape, 1)
            st = jnp.where(key <= qry, st, NEG)
        m_prev = m_ref[c]
        top = jnp.max(st, axis=0, keepdims=True)
        m_new = jnp.maximum(m_prev, top)
        alpha = jnp.exp2((m_prev - m_new) * c2)
        v_ext = jnp.concatenate([vt_ref[:, pl.ds(ks, size)], jnp.ones((dv, size), BF16)], axis=0)
        if exact:
            pt = jnp.exp2((st - m_new) * c2).astype(BF16)
            acc_ref[c] = alpha * acc_ref[c] + jnp.dot(v_ext, pt, preferred_element_type=F32)
        else:
            pt = jnp.exp2((st - m_prev) * c2).astype(BF16)
            acc_ref[c] = alpha * (acc_ref[c] + jnp.dot(v_ext, pt, preferred_element_type=F32))
            worst_ref[c] = jnp.maximum(worst_ref[c], (top - m_prev) * c2)
        m_ref[c] = m_new

    def run(work, ahead, exact):
        pending = [scores(c, ks, size) for c, ks, size, _ in work[:ahead]]
        for i, (c, ks, size, diagonal) in enumerate(work):
            st = pending.pop(0)
            if i + ahead < len(work):
                cn, ksn, sizen, _ = work[i + ahead]
                pending.append(scores(cn, ksn, sizen))
            update(c, st, ks, size, diagonal, exact)

    def attend(exact):
        m_ref[...] = jnp.full(m_ref.shape, NEG, F32)
        acc_ref[...] = jnp.zeros(acc_ref.shape, F32)
        run([(c, pl.multiple_of(qi * tq + j * cq, cq), cq, c == j)
             for j in range(n_chain) for c in range(j, n_chain)], 4, True)

        def body(j, carry):
            run([(c, pl.multiple_of(j * tq + part * tk, tk), tk, False)
                 for part in range(tq // tk) for c in range(n_chain)], ahead, exact)
            return carry

        lax.fori_loop(0, qi, body, 0)

    worst_ref[...] = jnp.zeros(worst_ref.shape, F32)
    attend(exact=False)

    @pl.when(jnp.max(worst_ref[...]) > EXP2_HEADROOM)
    def _():
        attend(exact=True)

    for c in range(n_chain):
        acc = acc_ref[c]
        o_ref[c * cq:(c + 1) * cq, :] = (acc[:dv] / acc[dv:]).T.astype(o_ref.dtype)


def _attention(q_arr, k_arr, vt_arr, bias, *, dk, q_off, k_off, scale, tq=1024, cq=256, tk=512, ahead=3):
    b, s, _ = q_arr.shape
    heads, dv = vt_arr.shape[1], vt_arr.shape[2]
    tq = min(tq, s)
    cq = min(cq, tq)
    n_chain = tq // cq
    in_specs = [
        pl.BlockSpec((None, tq, dk), lambda bi, h, qi: (bi, qi, q_off + h)),
        pl.BlockSpec((None, s, dk), lambda bi, h, qi: (bi, 0, k_off + h)),
        pl.BlockSpec((None, None, dv, s), lambda bi, h, qi: (bi, h, 0, 0)),
    ]
    args = [q_arr, k_arr, vt_arr]
    scratch = [pltpu.VMEM((n_chain, 1, cq), F32), pltpu.VMEM((n_chain, 2 * dv, cq), F32),
               pltpu.VMEM((n_chain, 1, cq), F32)]
    if bias is not None:
        in_specs.append(pl.BlockSpec((None, None, 1, s), lambda bi, h, qi: (bi, h, 0, 0)))
        args.append(bias)
        scratch.append(pltpu.VMEM((s, LANES), F32))
    return pl.pallas_call(
        functools.partial(_attn_kernel, tq=tq, cq=cq, tk=min(tk, tq), ahead=ahead, scale=scale,
                          has_bias=bias is not None),
        grid=(b, heads, s // tq),
        in_specs=in_specs,
        out_specs=pl.BlockSpec((None, tq, dv), lambda bi, h, qi: (bi, qi, h)),
        out_shape=jax.ShapeDtypeStruct((b, s, heads * dv), BF16),
        scratch_shapes=scratch,
        compiler_params=_params("arbitrary", "arbitrary", "arbitrary"),
        name="attn_bias" if bias is not None else "attn",
    )(*args)


def _dil_kernel(q_ref, k_ref, v_ref, kp_ref, vp_ref, o_ref, lse_ref, kbuf, vbuf, *, scale, group):
    planes, tl, _ = q_ref.shape
    nb = tl // BLOCK
    c2 = scale * LOG2E
    kbuf[:, :BLOCK, :] = kp_ref[...]
    kbuf[:, BLOCK:, :] = k_ref[...]
    vbuf[:, :BLOCK, :] = vp_ref[...]
    vbuf[:, BLOCK:, :] = v_ref[...]
    row = lax.broadcasted_iota(jnp.int32, (BLOCK, 2 * BLOCK), 0)
    col = lax.broadcasted_iota(jnp.int32, (BLOCK, 2 * BLOCK), 1)
    band = (col >= row) & (col <= row + BLOCK)
    lane = lax.broadcasted_iota(jnp.int32, (BLOCK, LANES), 1)
    ones = jnp.ones((2 * BLOCK, HEAD_DIM), BF16)
    heads = [slice(h * HEAD_DIM, (h + 1) * HEAD_DIM) for h in range(DIL_HEADS)]
    first_chunk = pl.program_id(2) == 0

    def locate(g, u):
        if nb % group == 0:
            gpp = nb // group
            p = g // gpp if planes > 1 else 0
            first = (g % gpp == 0) if u == 0 else None
            return p, (g % gpp) * group + u, first
        per = group // nb
        return g * per + u // nb, u % nb, (True if u % nb == 0 else None)

    def body(g, carry):
        work, masks = [], []
        for u in range(group):
            p, n, first = locate(g, u)
            start = n * BLOCK if isinstance(n, int) else pl.multiple_of(n * BLOCK, BLOCK)
            if first is None:
                masks.append(band)
            else:
                cond = first_chunk if first is True else jnp.logical_and(first_chunk, first)
                masks.append(band & (col >= jnp.where(cond, BLOCK, 0)))
            work += [(u, p, start, h) for h in range(DIL_HEADS)]

        def scores(u, p, start, h):
            q = q_ref[p, pl.ds(start, BLOCK), heads[h]]
            kc = kbuf[p, pl.ds(start, 2 * BLOCK), heads[h]]
            return lax.dot_general(q, kc, (((1,), (1,)), ((), ())), preferred_element_type=F32)

        ahead = 2 * DIL_HEADS
        pending = [scores(*w) for w in work[:ahead]]
        tiles = {}
        for i, (u, p, start, h) in enumerate(work):
            s = pending.pop(0)
            if i + ahead < len(work):
                pending.append(scores(*work[i + ahead]))
            s = jnp.where(masks[u], s, NEG)
            m = jnp.max(s, axis=-1, keepdims=True)
            e = jnp.exp2((s - m) * c2).astype(BF16)
            v_ext = jnp.concatenate([vbuf[p, pl.ds(start, 2 * BLOCK), heads[h]], ones], axis=1)
            pv = jnp.dot(e, v_ext, preferred_element_type=F32)
            denom = pv[:, HEAD_DIM:]
            o_ref[p, pl.ds(start, BLOCK), heads[h]] = pv[:, :HEAD_DIM] / denom
            tile = tiles.get(u, jnp.zeros((BLOCK, LANES), F32))
            tiles[u] = jnp.where(lane == h, m * scale + jnp.log(denom), tile)
            if h == DIL_HEADS - 1:
                lse_ref[p, pl.ds(start, BLOCK), :] = tiles.pop(u)
        return carry

    n_groups = planes * nb // group
    if n_groups == 1:
        body(0, 0)
    else:
        lax.fori_loop(0, n_groups, body, 0)


def _dil_branch(qkv, *, scale, rows=1024, group=4):
    b, dil, sub, w3 = qkv.shape
    w = w3 // 3
    tl = min(rows, sub)
    planes = min(rows // tl, dil)
    bpc = tl // BLOCK
    group = min(group, planes * bpc)
    assert (bpc % group == 0 or group % bpc == 0) and (planes * bpc) % group == 0
    own = lambda j: pl.BlockSpec((None, planes, tl, w), lambda bi, r, c: (bi, r, c, j))
    prev = lambda j: pl.BlockSpec((None, planes, BLOCK, w),
                                  lambda bi, r, c: (bi, r, jnp.maximum(c * bpc - 1, 0), j))
    return pl.pallas_call(
        functools.partial(_dil_kernel, scale=scale, group=group),
        grid=(b, dil // planes, sub // tl),
        in_specs=[own(0), own(1), own(2), prev(1), prev(2)],
        out_specs=[
            pl.BlockSpec((None, planes, tl, w), lambda bi, r, c: (bi, r, c, 0)),
            pl.BlockSpec((None, planes, tl, LANES), lambda bi, r, c: (bi, r, c, 0)),
        ],
        out_shape=[
            jax.ShapeDtypeStruct((b, dil, sub, w), F32),
            jax.ShapeDtypeStruct((b, dil, sub, LANES), F32),
        ],
        scratch_shapes=[pltpu.VMEM((planes, tl + BLOCK, w), BF16)] * 2,
        compiler_params=_params("parallel", "parallel", "parallel"),
        name=f"dilated_{dil}",
    )(qkv, qkv, qkv, qkv, qkv)


def _out_proj_kernel(x_ref, a_ref, b_ref, *rest, dils):
    nbr = len(dils)
    o_refs, l_refs = rest[:nbr], rest[nbr:2 * nbr]
    w_ref, y_ref = rest[2 * nbr:2 * nbr + 2]
    scr = rest[2 * nbr + 2:]
    tm = x_ref.shape[0]

    def natural(ref, dil, slot, j):
        sl = slice(j * LANES, (j + 1) * LANES)
        if dil == 1:
            return ref[0, :, sl]
        rows = tm // dil
        for r in range(dil):
            scr[slot][pl.ds(r, rows, stride=dil), :] = ref[r, :, sl]
        return scr[slot][...]

    lses = [natural(l_refs[i], dils[i], 2 * i, 0) for i in range(nbr)]
    m = functools.reduce(jnp.maximum, lses)
    es = [jnp.exp(l - m) for l in lses]
    tot = functools.reduce(lambda u, v: u + v, es)
    wts = [e / tot for e in es]
    parts = [a_ref[...], b_ref[...]]
    for h in range(DIL_HEADS):
        c = None
        for i in range(nbr):
            term = wts[i][:, h:h + 1] * natural(o_refs[i], dils[i], 2 * i + 1, h)
            c = term if c is None else c + term
        parts.append(c.astype(BF16))
    mixed = jnp.concatenate(parts, axis=-1)
    y_ref[...] = x_ref[...] + jnp.dot(mixed, w_ref[...], preferred_element_type=F32)


def _out_proj(x, a, bm, os_, ls_, w, *, tm=512):
    b, s, d = x.shape
    tm = min(tm, s)
    dils = tuple(o.shape[1] for o in os_)
    row = lambda arr: pl.BlockSpec((None, tm, arr.shape[2]), lambda bi, i: (bi, i, 0))
    plane = lambda arr: pl.BlockSpec((None, arr.shape[1], tm // arr.shape[1], arr.shape[3]),
                                     lambda bi, i: (bi, 0, i, 0))
    return pl.pallas_call(
        functools.partial(_out_proj_kernel, dils=dils),
        grid=(b, s // tm),
        in_specs=[row(x), row(a), row(bm)] + [plane(o) for o in os_] + [plane(l) for l in ls_]
                 + [pl.BlockSpec(w.shape, lambda bi, i: (0, 0))],
        out_specs=row(x),
        out_shape=jax.ShapeDtypeStruct((b, s, d), F32),
        scratch_shapes=[pltpu.VMEM((tm, LANES), F32)] * (2 * len(dils)),
        compiler_params=_params("parallel", "parallel"),
        name="out_proj",
    )(x, a, bm, *os_, *ls_, w)


def _rope_table(seq, dim, fill):
    half = dim // 2
    inv = 1.0 / (ROPE_THETA ** (jnp.arange(0, dim, 2, dtype=F32) / dim))
    ang = jnp.arange(seq, dtype=F32)[:, None] * inv[None, :]
    cos, sin = jnp.cos(ang), jnp.sin(ang)
    rest = LANES - dim
    zeros_h = jnp.zeros((seq, half), F32)
    zeros_r = jnp.zeros((seq, rest), F32)
    cos_t = jnp.concatenate([cos, cos, jnp.full((seq, rest), fill, F32)], axis=1)
    s_hi = jnp.concatenate([-sin, zeros_h, zeros_r], axis=1)
    s_lo = jnp.concatenate([zeros_h, sin, zeros_r], axis=1)
    return jnp.stack([cos_t, s_hi, s_lo], axis=0)


def kernel(x, ffn1_norm, ffn1_w_gate, ffn1_w_up, ffn1_w_down, mix_norm, w_in, fox_forget_bias, mla_q_norm, mla_kv_norm, mla_w_uq, mla_w_ukv, w_out, ffn2_norm, ffn2_w_gate, ffn2_w_up, ffn2_w_down, final_norm):
    b, s, d = x.shape
    depth = w_in.shape[0]
    t = b * s
    fw = FOX_HEADS * HEAD_DIM
    o_fl = 3 * fw
    o_cq = o_fl + FOX_HEADS
    o_kr = o_cq + MLA_Q_RANK + MLA_KV_RANK
    o_dq = o_kr + MLA_ROPE

    assert all(window == BLOCK * dil and s % (BLOCK * dil) == 0 for window, dil in DIL_BRANCHES)
    tab_mla = _rope_table(s, MLA_ROPE, 1.0)
    tab_dil = _rope_table(s, PARTIAL_ROPE, 1.0)

    pad_last = lambda w, n: jnp.pad(w, [(0, 0)] * (w.ndim - 1) + [(0, n - w.shape[-1])])
    w_fqk_all = w_in[:, :, :2 * fw].astype(BF16)
    w_fv_all = w_in[:, :, 2 * fw:o_fl].astype(BF16)
    w_fl_all = pad_last(w_in[:, :, o_fl:o_cq], LANES).astype(BF16)
    fbias_all = pad_last(fox_forget_bias.reshape(depth, 1, FOX_HEADS), LANES)
    w_c_all = jnp.concatenate([w_in[:, :, o_cq:o_kr], pad_last(w_in[:, :, o_kr:o_dq], LANES)], axis=-1).astype(BF16)
    w_dil_all = w_in[:, :, o_dq:].astype(BF16)
    wuq_all = mla_w_uq.reshape(depth, MLA_Q_RANK, MLA_HEADS, MLA_NOPE + MLA_ROPE)
    wuq_all = pad_last(wuq_all, 2 * LANES).reshape(depth, MLA_Q_RANK, MLA_HEADS * 2 * LANES).astype(BF16)
    wukv = mla_w_ukv.reshape(depth, MLA_KV_RANK, MLA_HEADS, MLA_NOPE + HEAD_DIM)
    w_k_all = wukv[..., :MLA_NOPE].reshape(depth, MLA_KV_RANK, MLA_HEADS * MLA_NOPE).astype(BF16)
    w_v_all = wukv[..., MLA_NOPE:].reshape(depth, MLA_KV_RANK, MLA_HEADS * HEAD_DIM).astype(BF16)
    w_out_all = w_out.astype(BF16)

    ffn1_w = [w.astype(BF16) for w in (ffn1_w_gate, ffn1_w_up, ffn1_w_down)]
    ffn2_w = [w.astype(BF16) for w in (ffn2_w_gate, ffn2_w_up, ffn2_w_down)]

    xf = x.reshape(t, d)
    for l in range(depth):
        last = l == depth - 1
        w_fqk, w_fv, w_fl, fbias = w_fqk_all[l], w_fv_all[l], w_fl_all[l], fbias_all[l]
        w_c, w_dil, wuq, w_k, w_v = w_c_all[l], w_dil_all[l], wuq_all[l], w_k_all[l], w_v_all[l]

        xf, hn = _ffn(xf, ffn1_norm[l], *(w[l] for w in ffn1_w), mix_norm[l], emit_norm=True)

        hn = hn.reshape(b, s, d)
        fox_scale = HEAD_DIM ** -0.5
        fqk, fvt, logf = _fox_proj(hn, w_fqk, w_fv, w_fl, fbias)
        logf = logf[:, :, :FOX_HEADS].transpose(0, 2, 1).reshape(b * FOX_HEADS, s)
        key_bias = _key_bias(logf, fox_scale).reshape(b, FOX_HEADS, 1, s)
        out_a = _attention(fqk, fqk, fvt, key_bias, dk=HEAD_DIM, q_off=0, k_off=FOX_HEADS, scale=fox_scale)

        q_b, k_b, vt_b = _mla_proj(hn, w_c, mla_q_norm[l].reshape(1, -1), mla_kv_norm[l].reshape(1, -1),
                                   wuq, w_k, w_v, tab_mla)
        out_b = _attention(q_b, k_b, vt_b, None, dk=2 * LANES, q_off=0, k_off=0,
                           scale=(MLA_NOPE + MLA_ROPE) ** -0.5)

        dqkvs = _dil_proj(hn, w_dil, tab_dil, dils=tuple(dil for _, dil in DIL_BRANCHES))
        outs, lses = [], []
        for dqkv in dqkvs:
            o, lse = _dil_branch(dqkv, scale=HEAD_DIM ** -0.5)
            outs.append(o)
            lses.append(lse)

        xf = _out_proj(xf.reshape(b, s, d), out_a, out_b, outs, lses, w_out_all[l]).reshape(t, d)

        xf = _ffn(xf, ffn2_norm[l], *(w[l] for w in ffn2_w), final_norm if last else None, final_norm=last)
    return xf.reshape(b, s, d)
```

```python
import functools

import jax
import jax.numpy as jnp
from jax import lax
from jax.experimental import pallas as pl
from jax.experimental.pallas import tpu as pltpu

F32 = jnp.float32
BF16 = jnp.bfloat16

LANES = 128
HEAD_DIM = 128
BLOCK = 128
EPS = 1e-6
ROPE_THETA = 500000.0
PARTIAL_ROPE = HEAD_DIM // 4
FOX_HEADS = 4
MLA_HEADS = 8
MLA_Q_RANK = 512
MLA_KV_RANK = 512
MLA_NOPE = 128
MLA_ROPE = 64
DIL_HEADS = 4
DIL_BRANCHES = ((128, 1), (512, 4), (2048, 16))
NEG = -1e30
LOG2E = 1.4426950408889634
EXP2_HEADROOM = 100.0
VMEM_LIMIT = 56 * 1024 * 1024
FFN_TILE = 512


def _params(*sem):
    return pltpu.CompilerParams(dimension_semantics=sem, vmem_limit_bytes=VMEM_LIMIT)


def _rms(x, g):
    return x * lax.rsqrt(jnp.mean(x * x, axis=-1, keepdims=True) + EPS) * g


def _apply_rope(x, cos, s_hi, s_lo, half):
    return x * cos + pltpu.roll(x, LANES - half, 1) * s_hi + pltpu.roll(x, half, 1) * s_lo


def _ffn_kernel(*refs, emit_norm, final_norm):
    x_hbm, g_ref, wg_ref, wu_ref, wd_ref = refs[:5]
    pos = 5
    g2_ref = None
    if emit_norm or final_norm:
        g2_ref = refs[pos]
        pos += 1
    o_ref = refs[pos]
    pos += 1
    hn_ref = None
    if emit_norm:
        hn_ref = refs[pos]
        pos += 1
    xn_ref, xbuf, sem = refs[pos:pos + 3]

    i, j = pl.program_id(0), pl.program_id(1)
    ni, nj = pl.num_programs(0), pl.num_programs(1)
    tm = xbuf.shape[0]

    def x_copy(tile):
        return pltpu.make_async_copy(x_hbm.at[pl.ds(pl.multiple_of(tile * tm, tm), tm), :], xbuf, sem)

    @pl.when(jnp.logical_and(i == 0, j == 0))
    def _():
        x_copy(0).start()

    @pl.when(j == 0)
    def _():
        x_copy(i).wait()
        x = xbuf[...]
        xn_ref[...] = _rms(x, g_ref[...]).astype(BF16)
        o_ref[...] = x

    @pl.when(jnp.logical_and(j == 1, i + 1 < ni))
    def _():
        x_copy(i + 1).start()

    xn = xn_ref[...]
    gate = jnp.dot(xn, wg_ref[...], preferred_element_type=F32)
    up = jnp.dot(xn, wu_ref[...], preferred_element_type=F32)
    h = (gate * jax.nn.sigmoid(gate) * up * 0.5).astype(BF16)
    o_ref[...] += jnp.dot(h, wd_ref[...], preferred_element_type=F32)

    if emit_norm or final_norm:
        @pl.when(j == nj - 1)
        def _():
            y = _rms(o_ref[...], g2_ref[...])
            if emit_norm:
                hn_ref[...] = y.astype(BF16)
            else:
                o_ref[...] = y


def _ffn(x, g, wg, wu, wd, g2=None, *, emit_norm=False, final_norm=False, tm=1024, tf=FFN_TILE):
    t, d = x.shape
    nj = wd.shape[0] // tf
    tm = min(tm, t)
    assert nj >= 2
    in_specs = [
        pl.BlockSpec(memory_space=pl.ANY),
        pl.BlockSpec((1, d), lambda i, j: (0, 0)),
        pl.BlockSpec((d, tf), lambda i, j: (0, j)),
        pl.BlockSpec((d, tf), lambda i, j: (0, j)),
        pl.BlockSpec((tf, d), lambda i, j: (j, 0)),
    ]
    args = [x, g.reshape(1, d), wg, wu, wd]
    if emit_norm or final_norm:
        in_specs.append(pl.BlockSpec((1, d), lambda i, j: (0, 0)))
        args.append(g2.reshape(1, d))
    out_shape = [jax.ShapeDtypeStruct((t, d), F32)]
    out_specs = [pl.BlockSpec((tm, d), lambda i, j: (i, 0))]
    if emit_norm:
        out_shape.append(jax.ShapeDtypeStruct((t, d), BF16))
        out_specs.append(pl.BlockSpec((tm, d), lambda i, j: (i, 0)))
    res = pl.pallas_call(
        functools.partial(_ffn_kernel, emit_norm=emit_norm, final_norm=final_norm),
        grid=(t // tm, nj),
        in_specs=in_specs,
        out_specs=out_specs,
        out_shape=out_shape,
        scratch_shapes=[pltpu.VMEM((tm, d), BF16), pltpu.VMEM((tm, d), F32), pltpu.SemaphoreType.DMA],
        compiler_params=_params("arbitrary", "arbitrary"),
        name="ffn",
    )(*args)
    return res if emit_norm else res[0]


def _store_heads_transposed(vt_ref, v):
    for h in range(vt_ref.shape[0]):
        vt_ref[h] = v[:, h * HEAD_DIM:(h + 1) * HEAD_DIM].T.astype(BF16)


def _fox_proj_kernel(hn_ref, wqk_ref, wv_ref, wl_ref, b_ref, qk_ref, vt_ref, lf_ref):
    hn = hn_ref[...]
    v = jnp.dot(hn, wv_ref[...], preferred_element_type=F32)
    z = jnp.dot(hn, wl_ref[...], preferred_element_type=F32) + b_ref[...]
    qk = jnp.dot(hn, wqk_ref[...], preferred_element_type=F32)
    _store_heads_transposed(vt_ref, v)
    lf_ref[...] = jnp.minimum(z, 0.0) - jnp.log1p(jnp.exp(-jnp.abs(z)))
    qk_ref[...] = qk.astype(BF16)


def _fox_proj(hn, wqk, wv, wl, bias, *, tm=512):
    b, s, d = hn.shape
    tm = min(tm, s)
    n = wqk.shape[1]
    heads = wv.shape[1] // HEAD_DIM
    full = lambda a: pl.BlockSpec(a.shape, lambda bi, i: (0,) * a.ndim)
    return pl.pallas_call(
        _fox_proj_kernel,
        grid=(b, s // tm),
        in_specs=[pl.BlockSpec((None, tm, d), lambda bi, i: (bi, i, 0)), full(wqk), full(wv), full(wl), full(bias)],
        out_specs=[
            pl.BlockSpec((None, tm, n), lambda bi, i: (bi, i, 0)),
            pl.BlockSpec((None, heads, HEAD_DIM, tm), lambda bi, i: (bi, 0, 0, i)),
            pl.BlockSpec((None, tm, LANES), lambda bi, i: (bi, i, 0)),
        ],
        out_shape=[
            jax.ShapeDtypeStruct((b, s, n), BF16),
            jax.ShapeDtypeStruct((b, heads, HEAD_DIM, s), BF16),
            jax.ShapeDtypeStruct((b, s, LANES), F32),
        ],
        compiler_params=_params("parallel", "parallel"),
        name="fox_proj",
    )(hn, wqk, wv, wl, bias)


def _mla_proj_kernel(hn_ref, wc_ref, gq_ref, gkv_ref, wuq_ref, wk_ref, wv_ref, tab_ref,
                     q_ref, k_ref, vt_ref):
    hn = hn_ref[...]
    o_kv, o_kr = MLA_Q_RANK, MLA_Q_RANK + MLA_KV_RANK
    cq = jnp.dot(hn, wc_ref[:, :o_kv], preferred_element_type=F32)
    ckv = jnp.dot(hn, wc_ref[:, o_kv:o_kr], preferred_element_type=F32)
    cqn = _rms(cq, gq_ref[...]).astype(BF16)
    kr_raw = jnp.dot(hn, wc_ref[:, o_kr:], preferred_element_type=F32)
    q = jnp.dot(cqn, wuq_ref[...], preferred_element_type=F32)
    ckvn = _rms(ckv, gkv_ref[...]).astype(BF16)
    cos, s_hi, s_lo = tab_ref[0], tab_ref[1], tab_ref[2]
    half = MLA_ROPE // 2
    kr = _apply_rope(kr_raw, cos, s_hi, s_lo, half).astype(BF16)
    v = jnp.dot(ckvn, wv_ref[...], preferred_element_type=F32)
    kn = jnp.dot(ckvn, wk_ref[...], preferred_element_type=F32)
    _store_heads_transposed(vt_ref, v)
    for h in range(MLA_HEADS):
        lo = 2 * h * LANES
        q_ref[:, lo:lo + LANES] = q[:, lo:lo + LANES].astype(BF16)
        q_ref[:, lo + LANES:lo + 2 * LANES] = _apply_rope(
            q[:, lo + LANES:lo + 2 * LANES], cos, s_hi, s_lo, half).astype(BF16)
        k_ref[:, lo:lo + LANES] = kn[:, h * LANES:(h + 1) * LANES].astype(BF16)
        k_ref[:, lo + LANES:lo + 2 * LANES] = kr


def _mla_proj(hn, wc, gq, gkv, wuq, wk, wv, tab, *, tm=512):
    b, s, d = hn.shape
    tm = min(tm, s)
    nq = wuq.shape[1]
    heads = wv.shape[1] // HEAD_DIM
    full = lambda a: pl.BlockSpec(a.shape, lambda bi, i: (0,) * a.ndim)
    return pl.pallas_call(
        _mla_proj_kernel,
        grid=(b, s // tm),
        in_specs=[
            pl.BlockSpec((None, tm, d), lambda bi, i: (bi, i, 0)),
            full(wc), full(gq), full(gkv), full(wuq), full(wk), full(wv),
            pl.BlockSpec((3, tm, LANES), lambda bi, i: (0, i, 0)),
        ],
        out_specs=[
            pl.BlockSpec((None, tm, nq), lambda bi, i: (bi, i, 0)),
            pl.BlockSpec((None, tm, nq), lambda bi, i: (bi, i, 0)),
            pl.BlockSpec((None, heads, HEAD_DIM, tm), lambda bi, i: (bi, 0, 0, i)),
        ],
        out_shape=[
            jax.ShapeDtypeStruct((b, s, nq), BF16),
            jax.ShapeDtypeStruct((b, s, nq), BF16),
            jax.ShapeDtypeStruct((b, heads, HEAD_DIM, s), BF16),
        ],
        compiler_params=_params("parallel", "parallel"),
        name="mla_proj",
    )(hn, wc, gq, gkv, wuq, wk, wv, tab)


def _dil_proj_kernel(hn_ref, w_ref, tab_ref, *rest, dils):
    out_refs, scr = rest[:len(dils)], rest[len(dils):]
    hn = hn_ref[...]
    cos, s_hi, s_lo = tab_ref[0], tab_ref[1], tab_ref[2]
    half = PARTIAL_ROPE // 2
    n_rot = 2 * DIL_HEADS
    tm = hn.shape[0]
    group = DIL_HEADS * LANES

    def project(g):
        return jnp.dot(hn, w_ref[:, g * group:(g + 1) * group], preferred_element_type=F32)

    def emit(g, y):
        for jj in range(DIL_HEADS):
            j = g * DIL_HEADS + jj
            v = y[:, jj * LANES:(jj + 1) * LANES]
            scr[j][...] = _apply_rope(v, cos, s_hi, s_lo, half) if j < n_rot else v
            for o_ref, dil in zip(out_refs, dils):
                rows = tm // dil
                for r in range(dil):
                    o_ref[r, :, j * LANES:(j + 1) * LANES] = scr[j][pl.ds(r, rows, stride=dil), :].astype(BF16)

    ys = [project(0)]
    for g in range(3):
        if g + 1 < 3:
            ys.append(project(g + 1))
        emit(g, ys[g])


def _dil_proj(hn, w, tab, *, dils, tm=512):
    b, s, d = hn.shape
    tm = min(tm, s)
    n = w.shape[1]
    return pl.pallas_call(
        functools.partial(_dil_proj_kernel, dils=dils),
        grid=(b, s // tm),
        in_specs=[
            pl.BlockSpec((None, tm, d), lambda bi, i: (bi, i, 0)),
            pl.BlockSpec((d, n), lambda bi, i: (0, 0)),
            pl.BlockSpec((3, tm, LANES), lambda bi, i: (0, i, 0)),
        ],
        out_specs=[pl.BlockSpec((None, dil, tm // dil, n), lambda bi, i: (bi, 0, i, 0)) for dil in dils],
        out_shape=[jax.ShapeDtypeStruct((b, dil, s // dil, n), BF16) for dil in dils],
        scratch_shapes=[pltpu.VMEM((tm, LANES), F32)] * (n // LANES),
        compiler_params=_params("parallel", "parallel"),
        name="dil_proj",
    )(hn, w, tab)


def _cumsum_kernel(x_ref, o_ref, *, out_scale):
    x = x_ref[...]
    n = x.shape[-1]
    lane = lax.broadcasted_iota(jnp.int32, x.shape, 1)
    k = 1
    while k < n:
        x = x + jnp.where(lane >= k, pltpu.roll(x, k, 1), 0.0)
        k *= 2
    o_ref[...] = x * out_scale


def _key_bias(x, scale):
    return pl.pallas_call(
        functools.partial(_cumsum_kernel, out_scale=-1.0 / scale),
        out_shape=jax.ShapeDtypeStruct(x.shape, F32),
        compiler_params=pltpu.CompilerParams(vmem_limit_bytes=VMEM_LIMIT),
        name="forget_cumsum",
    )(x)


def _attn_kernel(*refs, tq, cq, tk, ahead, scale, has_bias):
    if has_bias:
        q_ref, k_ref, vt_ref, b_ref, o_ref, m_ref, acc_ref, worst_ref, brep_ref = refs
    else:
        q_ref, k_ref, vt_ref, o_ref, m_ref, acc_ref, worst_ref = refs
    qi = pl.program_id(2)
    n_chain = tq // cq
    dv = vt_ref.shape[0]
    c2 = scale * LOG2E

    if has_bias:
        @pl.when(qi == 0)
        def _():
            for j in range(brep_ref.shape[0] // tq):
                rows = jnp.broadcast_to(b_ref[:, j * tq:(j + 1) * tq], (LANES, tq))
                brep_ref[j * tq:(j + 1) * tq, :] = rows.T

    def scores(c, ks, size):
        q = q_ref[c * cq:(c + 1) * cq, :]
        k = k_ref[pl.ds(ks, size), :]
        return lax.dot_general(k, q, (((1,), (1,)), ((), ())), preferred_element_type=F32)

    def update(c, st, ks, size, diagonal, exact):
        if has_bias:
            st = st + jnp.tile(brep_ref[pl.ds(ks, size), :], (1, cq // LANES))
        if diagonal:
            key = lax.broadcasted_iota(jnp.int32, st.shape, 0)
            qry = lax.broadcasted_iota(jnp.int32, st.shape, 1)
            st = jnp.where(key <= qry, st, NEG)
        m_prev = m_ref[c]
        top = jnp.max(st, axis=0, keepdims=True)
        m_new = jnp.maximum(m_prev, top)
        alpha = jnp.exp2((m_prev - m_new) * c2)
        v_ext = jnp.concatenate([vt_ref[:, pl.ds(ks, size)], jnp.ones((dv, size), BF16)], axis=0)
        if exact:
            pt = jnp.exp2((st - m_new) * c2).astype(BF16)
            acc_ref[c] = alpha * acc_ref[c] + jnp.dot(v_ext, pt, preferred_element_type=F32)
        else:
            pt = jnp.exp2((st - m_prev) * c2).astype(BF16)
            acc_ref[c] = alpha * (acc_ref[c] + jnp.dot(v_ext, pt, preferred_element_type=F32))
            worst_ref[c] = jnp.maximum(worst_ref[c], (top - m_prev) * c2)
        m_ref[c] = m_new

    def run(work, ahead):
        pending = [scores(*w[:3]) for w in work[:ahead]]
        for i, (c, ks, size, diagonal, exact) in enumerate(work):
            st = pending.pop(0)
            if i + ahead < len(work):
                pending.append(scores(*work[i + ahead][:3]))
            update(c, st, ks, size, diagonal, exact)

    def attend(exact):
        m_ref[...] = jnp.full(m_ref.shape, NEG, F32)
        acc_ref[...] = jnp.zeros(acc_ref.shape, F32)
        run([(c, pl.multiple_of(qi * tq + j * cq, cq), cq, c == j, exact or j == 0)
             for j in range(n_chain) for c in range(j, n_chain)], 4)

        def body(j, carry):
            run([(c, pl.multiple_of(j * tq + part * tk, tk), tk, False, exact)
                 for part in range(tq // tk) for c in range(n_chain)], ahead)
            return carry

        lax.fori_loop(0, qi, body, 0)

    worst_ref[...] = jnp.zeros(worst_ref.shape, F32)
    attend(exact=False)

    @pl.when(jnp.max(worst_ref[...]) > EXP2_HEADROOM)
    def _():
        attend(exact=True)

    for c in range(n_chain):
        acc = acc_ref[c]
        o_ref[c * cq:(c + 1) * cq, :] = (acc[:dv] / acc[dv:]).T.astype(o_ref.dtype)


def _attention(q_arr, k_arr, vt_arr, bias, *, dk, q_off, k_off, scale, tq=1024, cq=256, tk=512, ahead=3):
    b, s, _ = q_arr.shape
    heads, dv = vt_arr.shape[1], vt_arr.shape[2]
    tq = min(tq, s)
    cq = min(cq, tq)
    n_chain = tq // cq
    in_specs = [
        pl.BlockSpec((None, tq, dk), lambda bi, h, qi: (bi, qi, q_off + h)),
        pl.BlockSpec((None, s, dk), lambda bi, h, qi: (bi, 0, k_off + h)),
        pl.BlockSpec((None, None, dv, s), lambda bi, h, qi: (bi, h, 0, 0)),
    ]
    args = [q_arr, k_arr, vt_arr]
    stat = pltpu.VMEM((n_chain, 1, cq), F32)
    scratch = [stat, pltpu.VMEM((n_chain, 2 * dv, cq), F32), stat]
    if bias is not None:
        in_specs.append(pl.BlockSpec((None, None, 1, s), lambda bi, h, qi: (bi, h, 0, 0)))
        args.append(bias)
        scratch.append(pltpu.VMEM((s, LANES), F32))
    return pl.pallas_call(
        functools.partial(_attn_kernel, tq=tq, cq=cq, tk=min(tk, tq), ahead=ahead, scale=scale,
                          has_bias=bias is not None),
        grid=(b, heads, s // tq),
        in_specs=in_specs,
        out_specs=pl.BlockSpec((None, tq, dv), lambda bi, h, qi: (bi, qi, h)),
        out_shape=jax.ShapeDtypeStruct((b, s, heads * dv), BF16),
        scratch_shapes=scratch,
        compiler_params=_params("arbitrary", "arbitrary", "arbitrary"),
        name="attn_bias" if bias is not None else "attn",
    )(*args)


def _dil_kernel(q_ref, k_ref, v_ref, kp_ref, vp_ref, o_ref, lse_ref, kbuf, vbuf, *, scale, group):
    planes, tl, _ = q_ref.shape
    nb = tl // BLOCK
    c2 = scale * LOG2E
    kbuf[:, :BLOCK, :] = kp_ref[...]
    kbuf[:, BLOCK:, :] = k_ref[...]
    vbuf[:, :BLOCK, :] = vp_ref[...]
    vbuf[:, BLOCK:, :] = v_ref[...]
    row = lax.broadcasted_iota(jnp.int32, (BLOCK, 2 * BLOCK), 0)
    col = lax.broadcasted_iota(jnp.int32, (BLOCK, 2 * BLOCK), 1)
    band = (col >= row) & (col <= row + BLOCK)
    lane = lax.broadcasted_iota(jnp.int32, (BLOCK, LANES), 1)
    ones = jnp.ones((2 * BLOCK, HEAD_DIM), BF16)
    heads = [slice(h * HEAD_DIM, (h + 1) * HEAD_DIM) for h in range(DIL_HEADS)]
    first_chunk = pl.program_id(2) == 0

    def locate(g, u):
        if nb % group == 0:
            gpp = nb // group
            p = g // gpp if planes > 1 else 0
            first = (g % gpp == 0) if u == 0 else None
            return p, (g % gpp) * group + u, first
        per = group // nb
        return g * per + u // nb, u % nb, (True if u % nb == 0 else None)

    def body(g, carry):
        work, masks = [], []
        for u in range(group):
            p, n, first = locate(g, u)
            start = n * BLOCK if isinstance(n, int) else pl.multiple_of(n * BLOCK, BLOCK)
            if first is None:
                masks.append(band)
            else:
                cond = first_chunk if first is True else jnp.logical_and(first_chunk, first)
                masks.append(band & (col >= jnp.where(cond, BLOCK, 0)))
            work += [(u, p, start, h) for h in range(DIL_HEADS)]

        def scores(u, p, start, h):
            q = q_ref[p, pl.ds(start, BLOCK), heads[h]]
            kc = kbuf[p, pl.ds(start, 2 * BLOCK), heads[h]]
            return lax.dot_general(q, kc, (((1,), (1,)), ((), ())), preferred_element_type=F32)

        ahead = 2 * DIL_HEADS
        pending = [scores(*w) for w in work[:ahead]]
        tiles = {}
        for i, (u, p, start, h) in enumerate(work):
            s = pending.pop(0)
            if i + ahead < len(work):
                pending.append(scores(*work[i + ahead]))
            s = jnp.where(masks[u], s, NEG)
            m = jnp.max(s, axis=-1, keepdims=True)
            e = jnp.exp2((s - m) * c2).astype(BF16)
            v_ext = jnp.concatenate([vbuf[p, pl.ds(start, 2 * BLOCK), heads[h]], ones], axis=1)
            pv = jnp.dot(e, v_ext, preferred_element_type=F32)
            denom = pv[:, HEAD_DIM:]
            o_ref[p, pl.ds(start, BLOCK), heads[h]] = pv[:, :HEAD_DIM] / denom
            tile = tiles.get(u, jnp.zeros((BLOCK, LANES), F32))
            tiles[u] = jnp.where(lane == h, m * scale + jnp.log(denom), tile)
            if h == DIL_HEADS - 1:
                lse_ref[p, pl.ds(start, BLOCK), :] = tiles.pop(u)
        return carry

    n_groups = planes * nb // group
    if n_groups == 1:
        body(0, 0)
    else:
        lax.fori_loop(0, n_groups, body, 0)


def _dil_branch(qkv, *, scale, rows=1024, group=4):
    b, dil, sub, w3 = qkv.shape
    w = w3 // 3
    tl = min(rows, sub)
    planes = min(rows // tl, dil)
    bpc = tl // BLOCK
    group = min(group, planes * bpc)
    assert (bpc % group == 0 or group % bpc == 0) and (planes * bpc) % group == 0
    own = lambda j: pl.BlockSpec((None, planes, tl, w), lambda bi, r, c: (bi, r, c, j))
    prev = lambda j: pl.BlockSpec((None, planes, BLOCK, w),
                                  lambda bi, r, c: (bi, r, jnp.maximum(c * bpc - 1, 0), j))
    return pl.pallas_call(
        functools.partial(_dil_kernel, scale=scale, group=group),
        grid=(b, dil // planes, sub // tl),
        in_specs=[own(0), own(1), own(2), prev(1), prev(2)],
        out_specs=[
            pl.BlockSpec((None, planes, tl, w), lambda bi, r, c: (bi, r, c, 0)),
            pl.BlockSpec((None, planes, tl, LANES), lambda bi, r, c: (bi, r, c, 0)),
        ],
        out_shape=[
            jax.ShapeDtypeStruct((b, dil, sub, w), F32),
            jax.ShapeDtypeStruct((b, dil, sub, LANES), F32),
        ],
        scratch_shapes=[pltpu.VMEM((planes, tl + BLOCK, w), BF16)] * 2,
        compiler_params=_params("parallel", "parallel", "parallel"),
        name=f"dilated_{dil}",
    )(qkv, qkv, qkv, qkv, qkv)


def _out_proj_kernel(x_ref, a_ref, b_ref, *rest, dils):
    nbr = len(dils)
    o_refs, l_refs = rest[:nbr], rest[nbr:2 * nbr]
    w_ref, y_ref = rest[2 * nbr:2 * nbr + 2]
    scr = rest[2 * nbr + 2:]
    tm = x_ref.shape[0]

    def natural(ref, dil, slot, j):
        sl = slice(j * LANES, (j + 1) * LANES)
        if dil == 1:
            return ref[0, :, sl]
        rows = tm // dil
        for r in range(dil):
            scr[slot][pl.ds(r, rows, stride=dil), :] = ref[r, :, sl]
        return scr[slot][...]

    lses = [natural(l_refs[i], dils[i], 2 * i, 0) for i in range(nbr)]
    m = functools.reduce(jnp.maximum, lses)
    es = [jnp.exp(l - m) for l in lses]
    tot = functools.reduce(lambda u, v: u + v, es)
    wts = [e / tot for e in es]
    parts = [a_ref[...], b_ref[...]]
    for h in range(DIL_HEADS):
        c = None
        for i in range(nbr):
            term = wts[i][:, h:h + 1] * natural(o_refs[i], dils[i], 2 * i + 1, h)
            c = term if c is None else c + term
        parts.append(c.astype(BF16))
    mixed = jnp.concatenate(parts, axis=-1)
    y_ref[...] = x_ref[...] + jnp.dot(mixed, w_ref[...], preferred_element_type=F32)


def _out_proj(x, a, bm, os_, ls_, w, *, tm=512):
    b, s, d = x.shape
    tm = min(tm, s)
    dils = tuple(o.shape[1] for o in os_)
    row = lambda arr: pl.BlockSpec((None, tm, arr.shape[2]), lambda bi, i: (bi, i, 0))
    plane = lambda arr: pl.BlockSpec((None, arr.shape[1], tm // arr.shape[1], arr.shape[3]),
                                     lambda bi, i: (bi, 0, i, 0))
    return pl.pallas_call(
        functools.partial(_out_proj_kernel, dils=dils),
        grid=(b, s // tm),
        in_specs=[row(x), row(a), row(bm)] + [plane(o) for o in os_] + [plane(l) for l in ls_]
                 + [pl.BlockSpec(w.shape, lambda bi, i: (0, 0))],
        out_specs=row(x),
        out_shape=jax.ShapeDtypeStruct((b, s, d), F32),
        scratch_shapes=[pltpu.VMEM((tm, LANES), F32)] * (2 * len(dils)),
        compiler_params=_params("parallel", "parallel"),
        name="out_proj",
    )(x, a, bm, *os_, *ls_, w)


def _rope_table(seq, dim, fill):
    half = dim // 2
    inv = 1.0 / (ROPE_THETA ** (jnp.arange(0, dim, 2, dtype=F32) / dim))
    ang = jnp.arange(seq, dtype=F32)[:, None] * inv[None, :]
    cos, sin = jnp.cos(ang), jnp.sin(ang)
    rest = LANES - dim
    zeros_h = jnp.zeros((seq, half), F32)
    zeros_r = jnp.zeros((seq, rest), F32)
    cos_t = jnp.concatenate([cos, cos, jnp.full((seq, rest), fill, F32)], axis=1)
    s_hi = jnp.concatenate([-sin, zeros_h, zeros_r], axis=1)
    s_lo = jnp.concatenate([zeros_h, sin, zeros_r], axis=1)
    return jnp.stack([cos_t, s_hi, s_lo], axis=0)


_FW = FOX_HEADS * HEAD_DIM
_O_FL = 3 * _FW
_O_CQ = _O_FL + FOX_HEADS
_O_KR = _O_CQ + MLA_Q_RANK + MLA_KV_RANK
_O_DQ = _O_KR + MLA_ROPE


def _split_w_in_kernel(w_ref, fqk_ref, fv_ref, fl_ref, c_ref, dil_ref):
    w = w_ref[...]
    lane = lax.broadcasted_iota(jnp.int32, (w.shape[0], LANES), 1)
    fqk_ref[...] = w[:, :2 * _FW].astype(BF16)
    fv_ref[...] = w[:, 2 * _FW:_O_FL].astype(BF16)
    fl_ref[...] = jnp.where(lane < FOX_HEADS, w[:, _O_FL:_O_FL + LANES], 0.0).astype(BF16)
    c_ref[:, :_O_KR - _O_CQ] = w[:, _O_CQ:_O_KR].astype(BF16)
    c_ref[:, _O_KR - _O_CQ:] = jnp.where(lane < MLA_ROPE, w[:, _O_KR:_O_KR + LANES], 0.0).astype(BF16)
    dil_ref[...] = w[:, _O_DQ:].astype(BF16)


def _split_w_in(w_in, *, rows=256):
    depth, d, n_in = w_in.shape
    n_c = _O_KR - _O_CQ + LANES
    widths = (2 * _FW, _FW, LANES, n_c, n_in - _O_DQ)
    return pl.pallas_call(
        _split_w_in_kernel,
        grid=(depth, d // rows),
        in_specs=[pl.BlockSpec((None, rows, n_in), lambda l, i: (l, i, 0))],
        out_specs=[pl.BlockSpec((None, rows, n), lambda l, i: (l, i, 0)) for n in widths],
        out_shape=[jax.ShapeDtypeStruct((depth, d, n), BF16) for n in widths],
        compiler_params=_params("parallel", "parallel"),
        name="split_w_in",
    )(w_in)


def kernel(x, ffn1_norm, ffn1_w_gate, ffn1_w_up, ffn1_w_down, mix_norm, w_in, fox_forget_bias, mla_q_norm, mla_kv_norm, mla_w_uq, mla_w_ukv, w_out, ffn2_norm, ffn2_w_gate, ffn2_w_up, ffn2_w_down, final_norm):
    b, s, d = x.shape
    depth = w_in.shape[0]
    t = b * s

    assert all(window == BLOCK * dil and s % (BLOCK * dil) == 0 for window, dil in DIL_BRANCHES)
    tab_mla = _rope_table(s, MLA_ROPE, 1.0)
    tab_dil = _rope_table(s, PARTIAL_ROPE, 1.0)

    pad_last = lambda w, n: jnp.pad(w, [(0, 0)] * (w.ndim - 1) + [(0, n - w.shape[-1])])
    w_fqk_all, w_fv_all, w_fl_all, w_c_all, w_dil_all = _split_w_in(w_in)
    fbias_all = pad_last(fox_forget_bias.reshape(depth, 1, FOX_HEADS), LANES)
    wuq_all = mla_w_uq.reshape(depth, MLA_Q_RANK, MLA_HEADS, MLA_NOPE + MLA_ROPE)
    wuq_all = pad_last(wuq_all, 2 * LANES).reshape(depth, MLA_Q_RANK, MLA_HEADS * 2 * LANES).astype(BF16)
    wukv = mla_w_ukv.reshape(depth, MLA_KV_RANK, MLA_HEADS, MLA_NOPE + HEAD_DIM)
    w_k_all = wukv[..., :MLA_NOPE].reshape(depth, MLA_KV_RANK, MLA_HEADS * MLA_NOPE).astype(BF16)
    w_v_all = wukv[..., MLA_NOPE:].reshape(depth, MLA_KV_RANK, MLA_HEADS * HEAD_DIM).astype(BF16)
    w_out_all = w_out.astype(BF16)

    ffn1_w = [w.astype(BF16) for w in (ffn1_w_gate, ffn1_w_up, ffn1_w_down)]
    ffn2_w = [w.astype(BF16) for w in (ffn2_w_gate, ffn2_w_up, ffn2_w_down)]

    xf = x.reshape(t, d)
    for l in range(depth):
        last = l == depth - 1
        w_fqk, w_fv, w_fl, fbias = w_fqk_all[l], w_fv_all[l], w_fl_all[l], fbias_all[l]
        w_c, w_dil, wuq, w_k, w_v = w_c_all[l], w_dil_all[l], wuq_all[l], w_k_all[l], w_v_all[l]

        xf, hn = _ffn(xf, ffn1_norm[l], *(w[l] for w in ffn1_w), mix_norm[l], emit_norm=True)

        hn = hn.reshape(b, s, d)
        fox_scale = HEAD_DIM ** -0.5
        fqk, fvt, logf = _fox_proj(hn, w_fqk, w_fv, w_fl, fbias)
        logf = logf[:, :, :FOX_HEADS].transpose(0, 2, 1).reshape(b * FOX_HEADS, s)
        key_bias = _key_bias(logf, fox_scale).reshape(b, FOX_HEADS, 1, s)
        out_a = _attention(fqk, fqk, fvt, key_bias, dk=HEAD_DIM, q_off=0, k_off=FOX_HEADS, scale=fox_scale)

        q_b, k_b, vt_b = _mla_proj(hn, w_c, mla_q_norm[l].reshape(1, -1), mla_kv_norm[l].reshape(1, -1),
                                   wuq, w_k, w_v, tab_mla)
        out_b = _attention(q_b, k_b, vt_b, None, dk=2 * LANES, q_off=0, k_off=0,
                           scale=(MLA_NOPE + MLA_ROPE) ** -0.5)

        dqkvs = _dil_proj(hn, w_dil, tab_dil, dils=tuple(dil for _, dil in DIL_BRANCHES))
        outs, lses = [], []
        for dqkv in dqkvs:
            o, lse = _dil_branch(dqkv, scale=HEAD_DIM ** -0.5)
            outs.append(o)
            lses.append(lse)

        xf = _out_proj(xf.reshape(b, s, d), out_a, out_b, outs, lses, w_out_all[l]).reshape(t, d)

        xf = _ffn(xf, ffn2_norm[l], *(w[l] for w in ffn2_w), final_norm if last else None, final_norm=last)
    return xf.reshape(b, s, d)
```

```python
import functools

import jax
import jax.numpy as jnp
from jax import lax
from jax.experimental import pallas as pl
from jax.experimental.pallas import tpu as pltpu

F32 = jnp.float32
BF16 = jnp.bfloat16

LANES = 128
HEAD_DIM = 128
BLOCK = 128
EPS = 1e-6
ROPE_THETA = 500000.0
PARTIAL_ROPE = HEAD_DIM // 4
FOX_HEADS = 4
MLA_HEADS = 8
MLA_Q_RANK = 512
MLA_KV_RANK = 512
MLA_NOPE = 128
MLA_ROPE = 64
DIL_HEADS = 4
DIL_BRANCHES = ((128, 1), (512, 4), (2048, 16))
NEG = -1e30
LOG2E = 1.4426950408889634
EXP2_HEADROOM = 100.0
VMEM_LIMIT = 56 * 1024 * 1024
FFN_TILE = 512


def _params(*sem):
    return pltpu.CompilerParams(dimension_semantics=sem, vmem_limit_bytes=VMEM_LIMIT)


def _layer_block(stacked, layer):
    return pl.BlockSpec((None,) + stacked.shape[1:], lambda *_: (layer,) + (0,) * (stacked.ndim - 1))


def _rms(x, g):
    return x * lax.rsqrt(jnp.mean(x * x, axis=-1, keepdims=True) + EPS) * g


def _apply_rope(x, cos, s_hi, s_lo, half):
    return x * cos + pltpu.roll(x, LANES - half, 1) * s_hi + pltpu.roll(x, half, 1) * s_lo


def _ffn_kernel(*refs, emit_norm, final_norm):
    x_hbm, g_ref, wg_ref, wu_ref, wd_ref = refs[:5]
    pos = 5
    g2_ref = None
    if emit_norm or final_norm:
        g2_ref = refs[pos]
        pos += 1
    o_ref = refs[pos]
    pos += 1
    hn_ref = None
    if emit_norm:
        hn_ref = refs[pos]
        pos += 1
    xn_ref, xbuf, sem = refs[pos:pos + 3]

    i, j = pl.program_id(0), pl.program_id(1)
    ni, nj = pl.num_programs(0), pl.num_programs(1)
    tm = xbuf.shape[0]

    def x_copy(tile):
        return pltpu.make_async_copy(x_hbm.at[pl.ds(pl.multiple_of(tile * tm, tm), tm), :], xbuf, sem)

    @pl.when(jnp.logical_and(i == 0, j == 0))
    def _():
        x_copy(0).start()

    @pl.when(j == 0)
    def _():
        x_copy(i).wait()
        x = xbuf[...]
        xn_ref[...] = _rms(x, g_ref[...]).astype(BF16)
        o_ref[...] = x

    @pl.when(jnp.logical_and(j == 1, i + 1 < ni))
    def _():
        x_copy(i + 1).start()

    xn = xn_ref[...]
    gate = jnp.dot(xn, wg_ref[...], preferred_element_type=F32)
    up = jnp.dot(xn, wu_ref[...], preferred_element_type=F32)
    h = (gate * jax.nn.sigmoid(gate) * up * 0.5).astype(BF16)
    o_ref[...] += jnp.dot(h, wd_ref[...], preferred_element_type=F32)

    if emit_norm or final_norm:
        @pl.when(j == nj - 1)
        def _():
            y = _rms(o_ref[...], g2_ref[...])
            if emit_norm:
                hn_ref[...] = y.astype(BF16)
            else:
                o_ref[...] = y


def _ffn(x, g, wg, wu, wd, g2=None, *, layer, emit_norm=False, final_norm=False, tm=1024, tf=FFN_TILE):
    t, d = x.shape
    nj = wd.shape[1] // tf
    tm = min(tm, t)
    assert nj >= 2
    in_specs = [
        pl.BlockSpec(memory_space=pl.ANY),
        pl.BlockSpec((1, d), lambda i, j: (0, 0)),
        pl.BlockSpec((None, d, tf), lambda i, j: (layer, 0, j)),
        pl.BlockSpec((None, d, tf), lambda i, j: (layer, 0, j)),
        pl.BlockSpec((None, tf, d), lambda i, j: (layer, j, 0)),
    ]
    args = [x, g.reshape(1, d), wg, wu, wd]
    if emit_norm or final_norm:
        in_specs.append(pl.BlockSpec((1, d), lambda i, j: (0, 0)))
        args.append(g2.reshape(1, d))
    out_shape = [jax.ShapeDtypeStruct((t, d), F32)]
    out_specs = [pl.BlockSpec((tm, d), lambda i, j: (i, 0))]
    if emit_norm:
        out_shape.append(jax.ShapeDtypeStruct((t, d), BF16))
        out_specs.append(pl.BlockSpec((tm, d), lambda i, j: (i, 0)))
    res = pl.pallas_call(
        functools.partial(_ffn_kernel, emit_norm=emit_norm, final_norm=final_norm),
        grid=(t // tm, nj),
        in_specs=in_specs,
        out_specs=out_specs,
        out_shape=out_shape,
        scratch_shapes=[pltpu.VMEM((tm, d), BF16), pltpu.VMEM((tm, d), F32), pltpu.SemaphoreType.DMA],
        compiler_params=_params("arbitrary", "arbitrary"),
        name="ffn",
    )(*args)
    return res if emit_norm else res[0]


def _store_heads_transposed(vt_ref, v):
    for h in range(vt_ref.shape[0]):
        vt_ref[h] = v[:, h * HEAD_DIM:(h + 1) * HEAD_DIM].T.astype(BF16)


def _fox_proj_kernel(hn_ref, wqk_ref, wv_ref, wl_ref, b_ref, qk_ref, vt_ref, lf_ref):
    hn = hn_ref[...]
    v = jnp.dot(hn, wv_ref[...], preferred_element_type=F32)
    z = jnp.dot(hn, wl_ref[...], preferred_element_type=F32) + b_ref[...]
    qk = jnp.dot(hn, wqk_ref[...], preferred_element_type=F32)
    _store_heads_transposed(vt_ref, v)
    lf_ref[...] = jnp.minimum(z, 0.0) - jnp.log1p(jnp.exp(-jnp.abs(z)))
    qk_ref[...] = qk.astype(BF16)


def _fox_proj(hn, wqk, wv, wl, bias, *, layer, tm=512):
    b, s, d = hn.shape
    tm = min(tm, s)
    n = wqk.shape[2]
    heads = wv.shape[2] // HEAD_DIM
    full = lambda a: _layer_block(a, layer)
    return pl.pallas_call(
        _fox_proj_kernel,
        grid=(b, s // tm),
        in_specs=[pl.BlockSpec((None, tm, d), lambda bi, i: (bi, i, 0)), full(wqk), full(wv), full(wl), full(bias)],
        out_specs=[
            pl.BlockSpec((None, tm, n), lambda bi, i: (bi, i, 0)),
            pl.BlockSpec((None, heads, HEAD_DIM, tm), lambda bi, i: (bi, 0, 0, i)),
            pl.BlockSpec((None, tm, LANES), lambda bi, i: (bi, i, 0)),
        ],
        out_shape=[
            jax.ShapeDtypeStruct((b, s, n), BF16),
            jax.ShapeDtypeStruct((b, heads, HEAD_DIM, s), BF16),
            jax.ShapeDtypeStruct((b, s, LANES), F32),
        ],
        compiler_params=_params("parallel", "parallel"),
        name="fox_proj",
    )(hn, wqk, wv, wl, bias)


def _mla_proj_kernel(hn_ref, wc_ref, gq_ref, gkv_ref, wuq_ref, wk_ref, wv_ref, tab_ref,
                     q_ref, k_ref, vt_ref):
    hn = hn_ref[...]
    o_kv, o_kr = MLA_Q_RANK, MLA_Q_RANK + MLA_KV_RANK
    cq = jnp.dot(hn, wc_ref[:, :o_kv], preferred_element_type=F32)
    ckv = jnp.dot(hn, wc_ref[:, o_kv:o_kr], preferred_element_type=F32)
    cqn = _rms(cq, gq_ref[...]).astype(BF16)
    kr_raw = jnp.dot(hn, wc_ref[:, o_kr:], preferred_element_type=F32)
    q = jnp.dot(cqn, wuq_ref[...], preferred_element_type=F32)
    ckvn = _rms(ckv, gkv_ref[...]).astype(BF16)
    cos, s_hi, s_lo = tab_ref[0], tab_ref[1], tab_ref[2]
    half = MLA_ROPE // 2
    kr = _apply_rope(kr_raw, cos, s_hi, s_lo, half).astype(BF16)
    v = jnp.dot(ckvn, wv_ref[...], preferred_element_type=F32)
    kn = jnp.dot(ckvn, wk_ref[...], preferred_element_type=F32)
    _store_heads_transposed(vt_ref, v)
    for h in range(MLA_HEADS):
        lo = 2 * h * LANES
        q_ref[:, lo:lo + LANES] = q[:, lo:lo + LANES].astype(BF16)
        q_ref[:, lo + LANES:lo + 2 * LANES] = _apply_rope(
            q[:, lo + LANES:lo + 2 * LANES], cos, s_hi, s_lo, half).astype(BF16)
        k_ref[:, lo:lo + LANES] = kn[:, h * LANES:(h + 1) * LANES].astype(BF16)
        k_ref[:, lo + LANES:lo + 2 * LANES] = kr


def _mla_proj(hn, wc, gq, gkv, wuq, wk, wv, tab, *, layer, tm=512):
    b, s, d = hn.shape
    tm = min(tm, s)
    nq = wuq.shape[2]
    heads = wv.shape[2] // HEAD_DIM
    full = lambda a: _layer_block(a, layer)
    return pl.pallas_call(
        _mla_proj_kernel,
        grid=(b, s // tm),
        in_specs=[
            pl.BlockSpec((None, tm, d), lambda bi, i: (bi, i, 0)),
            full(wc), full(gq), full(gkv), full(wuq), full(wk), full(wv),
            pl.BlockSpec((3, tm, LANES), lambda bi, i: (0, i, 0)),
        ],
        out_specs=[
            pl.BlockSpec((None, tm, nq), lambda bi, i: (bi, i, 0)),
            pl.BlockSpec((None, tm, nq), lambda bi, i: (bi, i, 0)),
            pl.BlockSpec((None, heads, HEAD_DIM, tm), lambda bi, i: (bi, 0, 0, i)),
        ],
        out_shape=[
            jax.ShapeDtypeStruct((b, s, nq), BF16),
            jax.ShapeDtypeStruct((b, s, nq), BF16),
            jax.ShapeDtypeStruct((b, heads, HEAD_DIM, s), BF16),
        ],
        compiler_params=_params("parallel", "parallel"),
        name="mla_proj",
    )(hn, wc, gq, gkv, wuq, wk, wv, tab)


def _dil_proj_kernel(hn_ref, w_ref, tab_ref, *rest, dils):
    out_refs, scr = rest[:len(dils)], rest[len(dils):]
    hn = hn_ref[...]
    cos, s_hi, s_lo = tab_ref[0], tab_ref[1], tab_ref[2]
    half = PARTIAL_ROPE // 2
    n_rot = 2 * DIL_HEADS
    tm = hn.shape[0]
    group = DIL_HEADS * LANES

    def project(g):
        return jnp.dot(hn, w_ref[:, g * group:(g + 1) * group], preferred_element_type=F32)

    def emit(g, y):
        for jj in range(DIL_HEADS):
            j = g * DIL_HEADS + jj
            v = y[:, jj * LANES:(jj + 1) * LANES]
            scr[j][...] = _apply_rope(v, cos, s_hi, s_lo, half) if j < n_rot else v
            for o_ref, dil in zip(out_refs, dils):
                rows = tm // dil
                for r in range(dil):
                    o_ref[r, :, j * LANES:(j + 1) * LANES] = scr[j][pl.ds(r, rows, stride=dil), :].astype(BF16)

    ys = [project(0)]
    for g in range(3):
        if g + 1 < 3:
            ys.append(project(g + 1))
        emit(g, ys[g])


def _dil_proj(hn, w, tab, *, layer, dils, tm=512):
    b, s, d = hn.shape
    tm = min(tm, s)
    n = w.shape[2]
    return pl.pallas_call(
        functools.partial(_dil_proj_kernel, dils=dils),
        grid=(b, s // tm),
        in_specs=[
            pl.BlockSpec((None, tm, d), lambda bi, i: (bi, i, 0)),
            _layer_block(w, layer),
            pl.BlockSpec((3, tm, LANES), lambda bi, i: (0, i, 0)),
        ],
        out_specs=[pl.BlockSpec((None, dil, tm // dil, n), lambda bi, i: (bi, 0, i, 0)) for dil in dils],
        out_shape=[jax.ShapeDtypeStruct((b, dil, s // dil, n), BF16) for dil in dils],
        scratch_shapes=[pltpu.VMEM((tm, LANES), F32)] * (n // LANES),
        compiler_params=_params("parallel", "parallel"),
        name="dil_proj",
    )(hn, w, tab)


def _cumsum_kernel(x_ref, o_ref, *, out_scale):
    x = x_ref[...]
    n = x.shape[-1]
    lane = lax.broadcasted_iota(jnp.int32, x.shape, 1)
    k = 1
    while k < n:
        x = x + jnp.where(lane >= k, pltpu.roll(x, k, 1), 0.0)
        k *= 2
    o_ref[...] = x * out_scale


def _key_bias(x, scale):
    return pl.pallas_call(
        functools.partial(_cumsum_kernel, out_scale=-1.0 / scale),
        out_shape=jax.ShapeDtypeStruct(x.shape, F32),
        compiler_params=pltpu.CompilerParams(vmem_limit_bytes=VMEM_LIMIT),
        name="forget_cumsum",
    )(x)


def _attn_kernel(*refs, tq, cq, tk, ahead, scale, has_bias):
    if has_bias:
        q_ref, k_ref, vt_ref, b_ref, o_ref, m_ref, acc_ref, worst_ref, brep_ref = refs
    else:
        q_ref, k_ref, vt_ref, o_ref, m_ref, acc_ref, worst_ref = refs
    qi = pl.program_id(2)
    n_chain = tq // cq
    dv = vt_ref.shape[0]
    c2 = scale * LOG2E

    if has_bias:
        @pl.when(qi == 0)
        def _():
            for j in range(brep_ref.shape[0] // tq):
                rows = jnp.broadcast_to(b_ref[:, j * tq:(j + 1) * tq], (LANES, tq))
                brep_ref[j * tq:(j + 1) * tq, :] = rows.T

    def scores(c, ks, size):
        q = q_ref[c * cq:(c + 1) * cq, :]
        k = k_ref[pl.ds(ks, size), :]
        return lax.dot_general(k, q, (((1,), (1,)), ((), ())), preferred_element_type=F32)

    def update(c, st, ks, size, diagonal, exact):
        if has_bias:
            st = st + jnp.tile(brep_ref[pl.ds(ks, size), :], (1, cq // LANES))
        if diagonal:
            key = lax.broadcasted_iota(jnp.int32, st.shape, 0)
            qry = lax.broadcasted_iota(jnp.int32, st.shape, 1)
            st = jnp.where(key <= qry, st, NEG)
        m_prev = m_ref[c]
        top = jnp.max(st, axis=0, keepdims=True)
        m_new = jnp.maximum(m_prev, top)
        alpha = jnp.exp2((m_prev - m_new) * c2)
        v_ext = jnp.concatenate([vt_ref[:, pl.ds(ks, size)], jnp.ones((dv, size), BF16)], axis=0)
        if exact:
            pt = jnp.exp2((st - m_new) * c2).astype(BF16)
            acc_ref[c] = alpha * acc_ref[c] + jnp.dot(v_ext, pt, preferred_element_type=F32)
        else:
            pt = jnp.exp2((st - m_prev) * c2).astype(BF16)
            acc_ref[c] = alpha * (acc_ref[c] + jnp.dot(v_ext, pt, preferred_element_type=F32))
            worst_ref[c] = jnp.maximum(worst_ref[c], (top - m_prev) * c2)
        m_ref[c] = m_new

    def run(work, ahead):
        pending = [scores(*w[:3]) for w in work[:ahead]]
        for i, (c, ks, size, diagonal, exact) in enumerate(work):
            st = pending.pop(0)
            if i + ahead < len(work):
                pending.append(scores(*work[i + ahead][:3]))
            update(c, st, ks, size, diagonal, exact)

    def attend(exact):
        m_ref[...] = jnp.full(m_ref.shape, NEG, F32)
        acc_ref[...] = jnp.zeros(acc_ref.shape, F32)
        run([(c, pl.multiple_of(qi * tq + j * cq, cq), cq, c == j, True)
             for j in range(n_chain) for c in range(j, n_chain)], 4)

        def body(j, carry):
            run([(c, pl.multiple_of(j * tq + part * tk, tk), tk, False, exact)
                 for part in range(tq // tk) for c in range(n_chain)], ahead)
            return carry

        lax.fori_loop(0, qi, body, 0)

    worst_ref[...] = jnp.zeros(worst_ref.shape, F32)
    attend(exact=False)

    @pl.when(jnp.max(worst_ref[...]) > EXP2_HEADROOM)
    def _():
        attend(exact=True)

    for c in range(n_chain):
        acc = acc_ref[c]
        o_ref[c * cq:(c + 1) * cq, :] = (acc[:dv] / acc[dv:]).T.astype(o_ref.dtype)


def _attention(q_arr, k_arr, vt_arr, bias, *, dk, q_off, k_off, scale, tq=1024, cq=256, tk=512, ahead=3):
    b, s, _ = q_arr.shape
    heads, dv = vt_arr.shape[1], vt_arr.shape[2]
    tq = min(tq, s)
    cq = min(cq, tq)
    n_chain = tq // cq
    in_specs = [
        pl.BlockSpec((None, tq, dk), lambda bi, h, qi: (bi, qi, q_off + h)),
        pl.BlockSpec((None, s, dk), lambda bi, h, qi: (bi, 0, k_off + h)),
        pl.BlockSpec((None, None, dv, s), lambda bi, h, qi: (bi, h, 0, 0)),
    ]
    args = [q_arr, k_arr, vt_arr]
    stat = pltpu.VMEM((n_chain, 1, cq), F32)
    scratch = [stat, pltpu.VMEM((n_chain, 2 * dv, cq), F32), stat]
    if bias is not None:
        in_specs.append(pl.BlockSpec((None, None, 1, s), lambda bi, h, qi: (bi, h, 0, 0)))
        args.append(bias)
        scratch.append(pltpu.VMEM((s, LANES), F32))
    return pl.pallas_call(
        functools.partial(_attn_kernel, tq=tq, cq=cq, tk=min(tk, tq), ahead=ahead, scale=scale,
                          has_bias=bias is not None),
        grid=(b, heads, s // tq),
        in_specs=in_specs,
        out_specs=pl.BlockSpec((None, tq, dv), lambda bi, h, qi: (bi, qi, h)),
        out_shape=jax.ShapeDtypeStruct((b, s, heads * dv), BF16),
        scratch_shapes=scratch,
        compiler_params=_params("arbitrary", "arbitrary", "arbitrary"),
        name="attn_bias" if bias is not None else "attn",
    )(*args)


def _dil_kernel(q_ref, k_ref, v_ref, kp_ref, vp_ref, o_ref, lse_ref, kbuf, vbuf, *, scale, group):
    planes, tl, _ = q_ref.shape
    nb = tl // BLOCK
    c2 = scale * LOG2E
    kbuf[:, :BLOCK, :] = kp_ref[...]
    kbuf[:, BLOCK:, :] = k_ref[...]
    vbuf[:, :BLOCK, :] = vp_ref[...]
    vbuf[:, BLOCK:, :] = v_ref[...]
    row = lax.broadcasted_iota(jnp.int32, (BLOCK, 2 * BLOCK), 0)
    col = lax.broadcasted_iota(jnp.int32, (BLOCK, 2 * BLOCK), 1)
    band = (col >= row) & (col <= row + BLOCK)
    lane = lax.broadcasted_iota(jnp.int32, (BLOCK, LANES), 1)
    ones = jnp.ones((2 * BLOCK, HEAD_DIM), BF16)
    heads = [slice(h * HEAD_DIM, (h + 1) * HEAD_DIM) for h in range(DIL_HEADS)]
    first_chunk = pl.program_id(2) == 0

    def locate(g, u):
        if nb % group == 0:
            gpp = nb // group
            p = g // gpp if planes > 1 else 0
            first = (g % gpp == 0) if u == 0 else None
            return p, (g % gpp) * group + u, first
        per = group // nb
        return g * per + u // nb, u % nb, (True if u % nb == 0 else None)

    def body(g, carry):
        work, masks = [], []
        for u in range(group):
            p, n, first = locate(g, u)
            start = n * BLOCK if isinstance(n, int) else pl.multiple_of(n * BLOCK, BLOCK)
            if first is None:
                masks.append(band)
            else:
                cond = first_chunk if first is True else jnp.logical_and(first_chunk, first)
                masks.append(band & (col >= jnp.where(cond, BLOCK, 0)))
            work += [(u, p, start, h) for h in range(DIL_HEADS)]

        def scores(u, p, start, h):
            q = q_ref[p, pl.ds(start, BLOCK), heads[h]]
            kc = kbuf[p, pl.ds(start, 2 * BLOCK), heads[h]]
            return lax.dot_general(q, kc, (((1,), (1,)), ((), ())), preferred_element_type=F32)

        ahead = 2 * DIL_HEADS
        pending = [scores(*w) for w in work[:ahead]]
        tiles = {}
        for i, (u, p, start, h) in enumerate(work):
            s = pending.pop(0)
            if i + ahead < len(work):
                pending.append(scores(*work[i + ahead]))
            s = jnp.where(masks[u], s, NEG)
            m = jnp.max(s, axis=-1, keepdims=True)
            e = jnp.exp2((s - m) * c2).astype(BF16)
            v_ext = jnp.concatenate([vbuf[p, pl.ds(start, 2 * BLOCK), heads[h]], ones], axis=1)
            pv = jnp.dot(e, v_ext, preferred_element_type=F32)
            denom = pv[:, HEAD_DIM:]
            o_ref[p, pl.ds(start, BLOCK), heads[h]] = pv[:, :HEAD_DIM] / denom
            tile = tiles.get(u, jnp.zeros((BLOCK, LANES), F32))
            tiles[u] = jnp.where(lane == h, m * scale + jnp.log(denom), tile)
            if h == DIL_HEADS - 1:
                lse_ref[p, pl.ds(start, BLOCK), :] = tiles.pop(u)
        return carry

    n_groups = planes * nb // group
    if n_groups == 1:
        body(0, 0)
    else:
        lax.fori_loop(0, n_groups, body, 0)


def _dil_branch(qkv, *, scale, rows=1024, group=4):
    b, dil, sub, w3 = qkv.shape
    w = w3 // 3
    tl = min(rows, sub)
    planes = min(rows // tl, dil)
    bpc = tl // BLOCK
    group = min(group, planes * bpc)
    assert (bpc % group == 0 or group % bpc == 0) and (planes * bpc) % group == 0
    own = lambda j: pl.BlockSpec((None, planes, tl, w), lambda bi, r, c: (bi, r, c, j))
    prev = lambda j: pl.BlockSpec((None, planes, BLOCK, w),
                                  lambda bi, r, c: (bi, r, jnp.maximum(c * bpc - 1, 0), j))
    return pl.pallas_call(
        functools.partial(_dil_kernel, scale=scale, group=group),
        grid=(b, dil // planes, sub // tl),
        in_specs=[own(0), own(1), own(2), prev(1), prev(2)],
        out_specs=[
            pl.BlockSpec((None, planes, tl, w), lambda bi, r, c: (bi, r, c, 0)),
            pl.BlockSpec((None, planes, tl, LANES), lambda bi, r, c: (bi, r, c, 0)),
        ],
        out_shape=[
            jax.ShapeDtypeStruct((b, dil, sub, w), F32),
            jax.ShapeDtypeStruct((b, dil, sub, LANES), F32),
        ],
        scratch_shapes=[pltpu.VMEM((planes, tl + BLOCK, w), BF16)] * 2,
        compiler_params=_params("parallel", "parallel", "parallel"),
        name=f"dilated_{dil}",
    )(qkv, qkv, qkv, qkv, qkv)


def _out_proj_kernel(x_ref, a_ref, b_ref, *rest, dils):
    nbr = len(dils)
    o_refs, l_refs = rest[:nbr], rest[nbr:2 * nbr]
    w_ref, y_ref = rest[2 * nbr:2 * nbr + 2]
    scr = rest[2 * nbr + 2:]
    tm = x_ref.shape[0]

    def natural(ref, dil, slot, j):
        sl = slice(j * LANES, (j + 1) * LANES)
        if dil == 1:
            return ref[0, :, sl]
        rows = tm // dil
        for r in range(dil):
            scr[slot][pl.ds(r, rows, stride=dil), :] = ref[r, :, sl]
        return scr[slot][...]

    lses = [natural(l_refs[i], dils[i], 2 * i, 0) for i in range(nbr)]
    m = functools.reduce(jnp.maximum, lses)
    es = [jnp.exp(l - m) for l in lses]
    tot = functools.reduce(lambda u, v: u + v, es)
    wts = [e / tot for e in es]
    parts = [a_ref[...], b_ref[...]]
    for h in range(DIL_HEADS):
        c = None
        for i in range(nbr):
            term = wts[i][:, h:h + 1] * natural(o_refs[i], dils[i], 2 * i + 1, h)
            c = term if c is None else c + term
        parts.append(c.astype(BF16))
    mixed = jnp.concatenate(parts, axis=-1)
    y_ref[...] = x_ref[...] + jnp.dot(mixed, w_ref[...], preferred_element_type=F32)


def _out_proj(x, a, bm, os_, ls_, w, *, layer, tm=512):
    b, s, d = x.shape
    tm = min(tm, s)
    dils = tuple(o.shape[1] for o in os_)
    row = lambda arr: pl.BlockSpec((None, tm, arr.shape[2]), lambda bi, i: (bi, i, 0))
    plane = lambda arr: pl.BlockSpec((None, arr.shape[1], tm // arr.shape[1], arr.shape[3]),
                                     lambda bi, i: (bi, 0, i, 0))
    return pl.pallas_call(
        functools.partial(_out_proj_kernel, dils=dils),
        grid=(b, s // tm),
        in_specs=[row(x), row(a), row(bm)] + [plane(o) for o in os_] + [plane(l) for l in ls_]
                 + [_layer_block(w, layer)],
        out_specs=row(x),
        out_shape=jax.ShapeDtypeStruct((b, s, d), F32),
        scratch_shapes=[pltpu.VMEM((tm, LANES), F32)] * (2 * len(dils)),
        compiler_params=_params("parallel", "parallel"),
        name="out_proj",
    )(x, a, bm, *os_, *ls_, w)


def _rope_table(seq, dim, fill):
    half = dim // 2
    inv = 1.0 / (ROPE_THETA ** (jnp.arange(0, dim, 2, dtype=F32) / dim))
    ang = jnp.arange(seq, dtype=F32)[:, None] * inv[None, :]
    cos, sin = jnp.cos(ang), jnp.sin(ang)
    rest = LANES - dim
    zeros_h = jnp.zeros((seq, half), F32)
    zeros_r = jnp.zeros((seq, rest), F32)
    cos_t = jnp.concatenate([cos, cos, jnp.full((seq, rest), fill, F32)], axis=1)
    s_hi = jnp.concatenate([-sin, zeros_h, zeros_r], axis=1)
    s_lo = jnp.concatenate([zeros_h, sin, zeros_r], axis=1)
    return jnp.stack([cos_t, s_hi, s_lo], axis=0)


_FW = FOX_HEADS * HEAD_DIM
_O_FL = 3 * _FW
_O_CQ = _O_FL + FOX_HEADS
_O_KR = _O_CQ + MLA_Q_RANK + MLA_KV_RANK
_O_DQ = _O_KR + MLA_ROPE


def _split_w_in_kernel(w_ref, fqk_ref, fv_ref, fl_ref, c_ref, dil_ref):
    w = w_ref[...]
    lane = lax.broadcasted_iota(jnp.int32, (w.shape[0], LANES), 1)
    fqk_ref[...] = w[:, :2 * _FW].astype(BF16)
    fv_ref[...] = w[:, 2 * _FW:_O_FL].astype(BF16)
    fl_ref[...] = jnp.where(lane < FOX_HEADS, w[:, _O_FL:_O_FL + LANES], 0.0).astype(BF16)
    c_ref[:, :_O_KR - _O_CQ] = w[:, _O_CQ:_O_KR].astype(BF16)
    c_ref[:, _O_KR - _O_CQ:] = jnp.where(lane < MLA_ROPE, w[:, _O_KR:_O_KR + LANES], 0.0).astype(BF16)
    dil_ref[...] = w[:, _O_DQ:].astype(BF16)


def _split_w_in(w_in, *, rows=256):
    depth, d, n_in = w_in.shape
    n_c = _O_KR - _O_CQ + LANES
    widths = (2 * _FW, _FW, LANES, n_c, n_in - _O_DQ)
    return pl.pallas_call(
        _split_w_in_kernel,
        grid=(depth, d // rows),
        in_specs=[pl.BlockSpec((None, rows, n_in), lambda l, i: (l, i, 0))],
        out_specs=[pl.BlockSpec((None, rows, n), lambda l, i: (l, i, 0)) for n in widths],
        out_shape=[jax.ShapeDtypeStruct((depth, d, n), BF16) for n in widths],
        compiler_params=_params("parallel", "parallel"),
        name="split_w_in",
    )(w_in)


def kernel(x, ffn1_norm, ffn1_w_gate, ffn1_w_up, ffn1_w_down, mix_norm, w_in, fox_forget_bias, mla_q_norm, mla_kv_norm, mla_w_uq, mla_w_ukv, w_out, ffn2_norm, ffn2_w_gate, ffn2_w_up, ffn2_w_down, final_norm):
    b, s, d = x.shape
    depth = w_in.shape[0]
    t = b * s

    assert all(window == BLOCK * dil and s % (BLOCK * dil) == 0 for window, dil in DIL_BRANCHES)
    tab_mla = _rope_table(s, MLA_ROPE, 1.0)
    tab_dil = _rope_table(s, PARTIAL_ROPE, 1.0)

    pad_last = lambda w, n: jnp.pad(w, [(0, 0)] * (w.ndim - 1) + [(0, n - w.shape[-1])])
    w_fqk_all, w_fv_all, w_fl_all, w_c_all, w_dil_all = _split_w_in(w_in)
    fbias_all = pad_last(fox_forget_bias.reshape(depth, 1, FOX_HEADS), LANES)
    wuq_all = mla_w_uq.reshape(depth, MLA_Q_RANK, MLA_HEADS, MLA_NOPE + MLA_ROPE)
    wuq_all = pad_last(wuq_all, 2 * LANES).reshape(depth, MLA_Q_RANK, MLA_HEADS * 2 * LANES).astype(BF16)
    wukv = mla_w_ukv.reshape(depth, MLA_KV_RANK, MLA_HEADS, MLA_NOPE + HEAD_DIM)
    w_k_all = wukv[..., :MLA_NOPE].reshape(depth, MLA_KV_RANK, MLA_HEADS * MLA_NOPE).astype(BF16)
    w_v_all = wukv[..., MLA_NOPE:].reshape(depth, MLA_KV_RANK, MLA_HEADS * HEAD_DIM).astype(BF16)
    w_out_all = w_out.astype(BF16)

    ffn1_w = [w.astype(BF16) for w in (ffn1_w_gate, ffn1_w_up, ffn1_w_down)]
    ffn2_w = [w.astype(BF16) for w in (ffn2_w_gate, ffn2_w_up, ffn2_w_down)]

    gq_all = mla_q_norm.reshape(depth, 1, MLA_Q_RANK)
    gkv_all = mla_kv_norm.reshape(depth, 1, MLA_KV_RANK)

    xf = x.reshape(t, d)
    for l in range(depth):
        last = l == depth - 1

        xf, hn = _ffn(xf, ffn1_norm[l], *ffn1_w, mix_norm[l], layer=l, emit_norm=True)

        hn = hn.reshape(b, s, d)
        fox_scale = HEAD_DIM ** -0.5
        fqk, fvt, logf = _fox_proj(hn, w_fqk_all, w_fv_all, w_fl_all, fbias_all, layer=l)
        logf = logf[:, :, :FOX_HEADS].transpose(0, 2, 1).reshape(b * FOX_HEADS, s)
        key_bias = _key_bias(logf, fox_scale).reshape(b, FOX_HEADS, 1, s)
        out_a = _attention(fqk, fqk, fvt, key_bias, dk=HEAD_DIM, q_off=0, k_off=FOX_HEADS, scale=fox_scale)

        q_b, k_b, vt_b = _mla_proj(hn, w_c_all, gq_all, gkv_all, wuq_all, w_k_all, w_v_all, tab_mla, layer=l)
        out_b = _attention(q_b, k_b, vt_b, None, dk=2 * LANES, q_off=0, k_off=0,
                           scale=(MLA_NOPE + MLA_ROPE) ** -0.5)

        dqkvs = _dil_proj(hn, w_dil_all, tab_dil, layer=l, dils=tuple(dil for _, dil in DIL_BRANCHES))
        outs, lses = [], []
        for dqkv in dqkvs:
            o, lse = _dil_branch(dqkv, scale=HEAD_DIM ** -0.5)
            outs.append(o)
            lses.append(lse)

        xf = _out_proj(xf.reshape(b, s, d), out_a, out_b, outs, lses, w_out_all, layer=l).reshape(t, d)

        xf = _ffn(xf, ffn2_norm[l], *ffn2_w, final_norm if last else None, layer=l, final_norm=last)
    return xf.reshape(b, s, d)
```

```python
import functools

import jax
import jax.numpy as jnp
from jax import lax
from jax.experimental import pallas as pl
from jax.experimental.pallas import tpu as pltpu

F32 = jnp.float32
BF16 = jnp.bfloat16

LANES = 128
HEAD_DIM = 128
BLOCK = 128
EPS = 1e-6
ROPE_THETA = 500000.0
PARTIAL_ROPE = HEAD_DIM // 4
FOX_HEADS = 4
MLA_HEADS = 8
MLA_Q_RANK = 512
MLA_KV_RANK = 512
MLA_NOPE = 128
MLA_ROPE = 64
DIL_HEADS = 4
DIL_BRANCHES = ((128, 1), (512, 4), (2048, 16))
NEG = -1e30
LOG2E = 1.4426950408889634
EXP2_HEADROOM = 100.0
VMEM_LIMIT = 56 * 1024 * 1024
FFN_TILE = 512


def _params(*sem):
    return pltpu.CompilerParams(dimension_semantics=sem, vmem_limit_bytes=VMEM_LIMIT)


def _layer_block(stacked, layer):
    return pl.BlockSpec((None,) + stacked.shape[1:], lambda *_: (layer,) + (0,) * (stacked.ndim - 1))


def _rms(x, g):
    return x * lax.rsqrt(jnp.mean(x * x, axis=-1, keepdims=True) + EPS) * g


def _apply_rope(x, cos, s_hi, s_lo, half):
    return x * cos + pltpu.roll(x, LANES - half, 1) * s_hi + pltpu.roll(x, half, 1) * s_lo


def _ffn_kernel(*refs, emit_norm, final_norm):
    x_hbm, g_ref, wg_ref, wu_ref, wd_ref = refs[:5]
    pos = 5
    g2_ref = None
    if emit_norm or final_norm:
        g2_ref = refs[pos]
        pos += 1
    o_ref = refs[pos]
    pos += 1
    hn_ref = None
    if emit_norm:
        hn_ref = refs[pos]
        pos += 1
    xn_ref, xbuf, sem = refs[pos:pos + 3]

    i, j = pl.program_id(0), pl.program_id(1)
    ni, nj = pl.num_programs(0), pl.num_programs(1)
    tm = xbuf.shape[0]

    def x_copy(tile):
        return pltpu.make_async_copy(x_hbm.at[pl.ds(pl.multiple_of(tile * tm, tm), tm), :], xbuf, sem)

    @pl.when(jnp.logical_and(i == 0, j == 0))
    def _():
        x_copy(0).start()

    @pl.when(j == 0)
    def _():
        x_copy(i).wait()
        x = xbuf[...]
        xn_ref[...] = _rms(x, g_ref[...]).astype(BF16)
        o_ref[...] = x

    @pl.when(jnp.logical_and(j == 1, i + 1 < ni))
    def _():
        x_copy(i + 1).start()

    xn = xn_ref[...]
    gate = jnp.dot(xn, wg_ref[...], preferred_element_type=F32)
    up = jnp.dot(xn, wu_ref[...], preferred_element_type=F32)
    h = (gate * jax.nn.sigmoid(gate) * up * 0.5).astype(BF16)
    o_ref[...] += jnp.dot(h, wd_ref[...], preferred_element_type=F32)

    if emit_norm or final_norm:
        @pl.when(j == nj - 1)
        def _():
            y = _rms(o_ref[...], g2_ref[...])
            if emit_norm:
                hn_ref[...] = y.astype(BF16)
            else:
                o_ref[...] = y


def _ffn(x, g, wg, wu, wd, g2=None, *, layer, emit_norm=False, final_norm=False, tm=1024, tf=FFN_TILE):
    t, d = x.shape
    nj = wd.shape[1] // tf
    tm = min(tm, t)
    assert nj >= 2
    in_specs = [
        pl.BlockSpec(memory_space=pl.ANY),
        pl.BlockSpec((1, d), lambda i, j: (0, 0)),
        pl.BlockSpec((None, d, tf), lambda i, j: (layer, 0, j)),
        pl.BlockSpec((None, d, tf), lambda i, j: (layer, 0, j)),
        pl.BlockSpec((None, tf, d), lambda i, j: (layer, j, 0)),
    ]
    args = [x, g.reshape(1, d), wg, wu, wd]
    if emit_norm or final_norm:
        in_specs.append(pl.BlockSpec((1, d), lambda i, j: (0, 0)))
        args.append(g2.reshape(1, d))
    out_shape = [jax.ShapeDtypeStruct((t, d), F32)]
    out_specs = [pl.BlockSpec((tm, d), lambda i, j: (i, 0))]
    if emit_norm:
        out_shape.append(jax.ShapeDtypeStruct((t, d), BF16))
        out_specs.append(pl.BlockSpec((tm, d), lambda i, j: (i, 0)))
    res = pl.pallas_call(
        functools.partial(_ffn_kernel, emit_norm=emit_norm, final_norm=final_norm),
        grid=(t // tm, nj),
        in_specs=in_specs,
        out_specs=out_specs,
        out_shape=out_shape,
        scratch_shapes=[pltpu.VMEM((tm, d), BF16), pltpu.VMEM((tm, d), F32), pltpu.SemaphoreType.DMA],
        compiler_params=_params("arbitrary", "arbitrary"),
        name="ffn",
    )(*args)
    return res if emit_norm else res[0]


def _store_heads_transposed(vt_ref, v):
    for h in range(vt_ref.shape[0]):
        vt_ref[h] = v[:, h * HEAD_DIM:(h + 1) * HEAD_DIM].T.astype(BF16)


def _fox_proj_kernel(hn_ref, wqk_ref, wv_ref, wl_ref, b_ref, qk_ref, vt_ref, lf_ref):
    hn = hn_ref[...]
    v = jnp.dot(hn, wv_ref[...], preferred_element_type=F32)
    z = jnp.dot(hn, wl_ref[...], preferred_element_type=F32) + b_ref[...]
    qk = jnp.dot(hn, wqk_ref[...], preferred_element_type=F32)
    _store_heads_transposed(vt_ref, v)
    lf_ref[...] = jnp.minimum(z, 0.0) - jnp.log1p(jnp.exp(-jnp.abs(z)))
    qk_ref[...] = qk.astype(BF16)


def _fox_proj(hn, wqk, wv, wl, bias, *, layer, tm=512):
    b, s, d = hn.shape
    tm = min(tm, s)
    n = wqk.shape[2]
    heads = wv.shape[2] // HEAD_DIM
    full = lambda a: _layer_block(a, layer)
    return pl.pallas_call(
        _fox_proj_kernel,
        grid=(b, s // tm),
        in_specs=[pl.BlockSpec((None, tm, d), lambda bi, i: (bi, i, 0)), full(wqk), full(wv), full(wl), full(bias)],
        out_specs=[
            pl.BlockSpec((None, tm, n), lambda bi, i: (bi, i, 0)),
            pl.BlockSpec((None, heads, HEAD_DIM, tm), lambda bi, i: (bi, 0, 0, i)),
            pl.BlockSpec((None, tm, LANES), lambda bi, i: (bi, i, 0)),
        ],
        out_shape=[
            jax.ShapeDtypeStruct((b, s, n), BF16),
            jax.ShapeDtypeStruct((b, heads, HEAD_DIM, s), BF16),
            jax.ShapeDtypeStruct((b, s, LANES), F32),
        ],
        compiler_params=_params("parallel", "parallel"),
        name="fox_proj",
    )(hn, wqk, wv, wl, bias)


def _mla_proj_kernel(hn_ref, wc_ref, gq_ref, gkv_ref, wuq_ref, wk_ref, wv_ref, tab_ref,
                     q_ref, k_ref, vt_ref):
    hn = hn_ref[...]
    o_kv, o_kr = MLA_Q_RANK, MLA_Q_RANK + MLA_KV_RANK
    cq = jnp.dot(hn, wc_ref[:, :o_kv], preferred_element_type=F32)
    ckv = jnp.dot(hn, wc_ref[:, o_kv:o_kr], preferred_element_type=F32)
    cqn = _rms(cq, gq_ref[...]).astype(BF16)
    kr_raw = jnp.dot(hn, wc_ref[:, o_kr:], preferred_element_type=F32)
    q = jnp.dot(cqn, wuq_ref[...], preferred_element_type=F32)
    ckvn = _rms(ckv, gkv_ref[...]).astype(BF16)
    cos, s_hi, s_lo = tab_ref[0], tab_ref[1], tab_ref[2]
    half = MLA_ROPE // 2
    kr = _apply_rope(kr_raw, cos, s_hi, s_lo, half).astype(BF16)
    v = jnp.dot(ckvn, wv_ref[...], preferred_element_type=F32)
    kn = jnp.dot(ckvn, wk_ref[...], preferred_element_type=F32)
    _store_heads_transposed(vt_ref, v)
    for h in range(MLA_HEADS):
        lo = 2 * h * LANES
        q_ref[:, lo:lo + LANES] = q[:, lo:lo + LANES].astype(BF16)
        q_ref[:, lo + LANES:lo + 2 * LANES] = _apply_rope(
            q[:, lo + LANES:lo + 2 * LANES], cos, s_hi, s_lo, half).astype(BF16)
        k_ref[:, lo:lo + LANES] = kn[:, h * LANES:(h + 1) * LANES].astype(BF16)
        k_ref[:, lo + LANES:lo + 2 * LANES] = kr


def _mla_proj(hn, wc, gq, gkv, wuq, wk, wv, tab, *, layer, tm=512):
    b, s, d = hn.shape
    tm = min(tm, s)
    nq = wuq.shape[2]
    heads = wv.shape[2] // HEAD_DIM
    full = lambda a: _layer_block(a, layer)
    return pl.pallas_call(
        _mla_proj_kernel,
        grid=(b, s // tm),
        in_specs=[
            pl.BlockSpec((None, tm, d), lambda bi, i: (bi, i, 0)),
            full(wc), full(gq), full(gkv), full(wuq), full(wk), full(wv),
            pl.BlockSpec((3, tm, LANES), lambda bi, i: (0, i, 0)),
        ],
        out_specs=[
            pl.BlockSpec((None, tm, nq), lambda bi, i: (bi, i, 0)),
            pl.BlockSpec((None, tm, nq), lambda bi, i: (bi, i, 0)),
            pl.BlockSpec((None, heads, HEAD_DIM, tm), lambda bi, i: (bi, 0, 0, i)),
        ],
        out_shape=[
            jax.ShapeDtypeStruct((b, s, nq), BF16),
            jax.ShapeDtypeStruct((b, s, nq), BF16),
            jax.ShapeDtypeStruct((b, heads, HEAD_DIM, s), BF16),
        ],
        compiler_params=_params("parallel", "parallel"),
        name="mla_proj",
    )(hn, wc, gq, gkv, wuq, wk, wv, tab)


def _dil_proj_kernel(hn_ref, w_ref, tab_ref, *rest, dils):
    out_refs, scr = rest[:len(dils)], rest[len(dils):]
    hn = hn_ref[...]
    cos, s_hi, s_lo = tab_ref[0], tab_ref[1], tab_ref[2]
    half = PARTIAL_ROPE // 2
    n_rot = 2 * DIL_HEADS
    tm = hn.shape[0]
    group = DIL_HEADS * LANES

    def project(g):
        return jnp.dot(hn, w_ref[:, g * group:(g + 1) * group], preferred_element_type=F32)

    def emit(g, y):
        for jj in range(DIL_HEADS):
            j = g * DIL_HEADS + jj
            v = y[:, jj * LANES:(jj + 1) * LANES]
            scr[j][...] = _apply_rope(v, cos, s_hi, s_lo, half) if j < n_rot else v
            for o_ref, dil in zip(out_refs, dils):
                rows = tm // dil
                for r in range(dil):
                    o_ref[r, :, j * LANES:(j + 1) * LANES] = scr[j][pl.ds(r, rows, stride=dil), :].astype(BF16)

    ys = [project(0)]
    for g in range(3):
        if g + 1 < 3:
            ys.append(project(g + 1))
        emit(g, ys[g])


def _dil_proj(hn, w, tab, *, layer, dils, tm=512):
    b, s, d = hn.shape
    tm = min(tm, s)
    n = w.shape[2]
    return pl.pallas_call(
        functools.partial(_dil_proj_kernel, dils=dils),
        grid=(b, s // tm),
        in_specs=[
            pl.BlockSpec((None, tm, d), lambda bi, i: (bi, i, 0)),
            _layer_block(w, layer),
            pl.BlockSpec((3, tm, LANES), lambda bi, i: (0, i, 0)),
        ],
        out_specs=[pl.BlockSpec((None, dil, tm // dil, n), lambda bi, i: (bi, 0, i, 0)) for dil in dils],
        out_shape=[jax.ShapeDtypeStruct((b, dil, s // dil, n), BF16) for dil in dils],
        scratch_shapes=[pltpu.VMEM((tm, LANES), F32)] * (n // LANES),
        compiler_params=_params("parallel", "parallel"),
        name="dil_proj",
    )(hn, w, tab)


def _cumsum_kernel(x_ref, o_ref, *, out_scale):
    x = x_ref[...]
    n = x.shape[-1]
    lane = lax.broadcasted_iota(jnp.int32, x.shape, 1)
    k = 1
    while k < n:
        x = x + jnp.where(lane >= k, pltpu.roll(x, k, 1), 0.0)
        k *= 2
    o_ref[...] = x * out_scale


def _key_bias(x, scale):
    return pl.pallas_call(
        functools.partial(_cumsum_kernel, out_scale=-1.0 / scale),
        out_shape=jax.ShapeDtypeStruct(x.shape, F32),
        compiler_params=pltpu.CompilerParams(vmem_limit_bytes=VMEM_LIMIT),
        name="forget_cumsum",
    )(x)


def _attn_kernel(*refs, tq, cq, tk, ahead, scale, has_bias):
    if has_bias:
        q_ref, k_ref, vt_ref, b_ref, o_ref, m_ref, acc_ref, worst_ref, brep_ref = refs
    else:
        q_ref, k_ref, vt_ref, o_ref, m_ref, acc_ref, worst_ref = refs
    qi = pl.program_id(2)
    n_chain = tq // cq
    dv = vt_ref.shape[0]
    c2 = scale * LOG2E

    if has_bias:
        @pl.when(qi == 0)
        def _():
            for j in range(brep_ref.shape[0] // tq):
                rows = jnp.broadcast_to(b_ref[:, j * tq:(j + 1) * tq], (LANES, tq))
                brep_ref[j * tq:(j + 1) * tq, :] = rows.T

    def scores(c, ks, size):
        q = q_ref[c * cq:(c + 1) * cq, :]
        k = k_ref[pl.ds(ks, size), :]
        return lax.dot_general(k, q, (((1,), (1,)), ((), ())), preferred_element_type=F32)

    def update(c, st, ks, size, exact):
        if has_bias:
            st = st + jnp.tile(brep_ref[pl.ds(ks, size), :], (1, cq // LANES))
        m_prev = m_ref[c]
        top = jnp.max(st, axis=0, keepdims=True)
        m_new = jnp.maximum(m_prev, top)
        alpha = jnp.exp2((m_prev - m_new) * c2)
        v_ext = jnp.concatenate([vt_ref[:, pl.ds(ks, size)], jnp.ones((dv, size), BF16)], axis=0)
        if exact:
            pt = jnp.exp2((st - m_new) * c2).astype(BF16)
            acc_ref[c] = alpha * acc_ref[c] + jnp.dot(v_ext, pt, preferred_element_type=F32)
        else:
            pt = jnp.exp2((st - m_prev) * c2).astype(BF16)
            acc_ref[c] = alpha * (acc_ref[c] + jnp.dot(v_ext, pt, preferred_element_type=F32))
            worst_ref[c] = jnp.maximum(worst_ref[c], (top - m_prev) * c2)
        m_ref[c] = m_new

    def run(work, ahead):
        pending = [scores(*w[:3]) for w in work[:ahead]]
        for i, (c, ks, size, exact) in enumerate(work):
            st = pending.pop(0)
            if i + ahead < len(work):
                pending.append(scores(*work[i + ahead][:3]))
            update(c, st, ks, size, exact)

    def diagonal_tile():
        work = [(c, pl.multiple_of(qi * tq + j * cq, cq), c == j)
                for j in range(n_chain) for c in range(j, n_chain)]
        look = 4
        pending = [scores(c, ks, cq) for c, ks, _ in work[:look]]
        partial = [[] for _ in range(n_chain)]
        for i, (c, ks, on_diagonal) in enumerate(work):
            st = pending.pop(0)
            if i + look < len(work):
                pending.append(scores(work[i + look][0], work[i + look][1], cq))
            if has_bias:
                st = st + jnp.tile(brep_ref[pl.ds(ks, cq), :], (1, cq // LANES))
            if on_diagonal:
                key = lax.broadcasted_iota(jnp.int32, st.shape, 0)
                qry = lax.broadcasted_iota(jnp.int32, st.shape, 1)
                st = jnp.where(key <= qry, st, NEG)
            top = jnp.max(st, axis=0, keepdims=True)
            pt = jnp.exp2((st - top) * c2).astype(BF16)
            v_ext = jnp.concatenate([vt_ref[:, pl.ds(ks, cq)], jnp.ones((dv, cq), BF16)], axis=0)
            partial[c].append((top, jnp.dot(v_ext, pt, preferred_element_type=F32)))
        for c in range(n_chain):
            m_all = functools.reduce(jnp.maximum, [m for m, _ in partial[c]])
            acc = None
            for m, pv in partial[c]:
                term = jnp.exp2((m - m_all) * c2) * pv
                acc = term if acc is None else acc + term
            m_ref[c] = m_all
            acc_ref[c] = acc

    def attend(exact):
        diagonal_tile()

        def body(j, carry):
            run([(c, pl.multiple_of(j * tq + part * tk, tk), tk, exact)
                 for part in range(tq // tk) for c in range(n_chain)], ahead)
            return carry

        lax.fori_loop(0, qi, body, 0)

    worst_ref[...] = jnp.zeros(worst_ref.shape, F32)
    attend(exact=False)

    @pl.when(jnp.max(worst_ref[...]) > EXP2_HEADROOM)
    def _():
        attend(exact=True)

    for c in range(n_chain):
        acc = acc_ref[c]
        o_ref[c * cq:(c + 1) * cq, :] = (acc[:dv] / acc[dv:]).T.astype(o_ref.dtype)


def _attention(q_arr, k_arr, vt_arr, bias, *, dk, q_off, k_off, scale, tq=1024, cq=256, tk=1024, ahead=2):
    b, s, _ = q_arr.shape
    heads, dv = vt_arr.shape[1], vt_arr.shape[2]
    tq = min(tq, s)
    cq = min(cq, tq)
    n_chain = tq // cq
    in_specs = [
        pl.BlockSpec((None, tq, dk), lambda bi, h, qi: (bi, qi, q_off + h)),
        pl.BlockSpec((None, s, dk), lambda bi, h, qi: (bi, 0, k_off + h)),
        pl.BlockSpec((None, None, dv, s), lambda bi, h, qi: (bi, h, 0, 0)),
    ]
    args = [q_arr, k_arr, vt_arr]
    stat = pltpu.VMEM((n_chain, 1, cq), F32)
    scratch = [stat, pltpu.VMEM((n_chain, 2 * dv, cq), F32), stat]
    if bias is not None:
        in_specs.append(pl.BlockSpec((None, None, 1, s), lambda bi, h, qi: (bi, h, 0, 0)))
        args.append(bias)
        scratch.append(pltpu.VMEM((s, LANES), F32))
    return pl.pallas_call(
        functools.partial(_attn_kernel, tq=tq, cq=cq, tk=min(tk, tq), ahead=ahead, scale=scale,
                          has_bias=bias is not None),
        grid=(b, heads, s // tq),
        in_specs=in_specs,
        out_specs=pl.BlockSpec((None, tq, dv), lambda bi, h, qi: (bi, qi, h)),
        out_shape=jax.ShapeDtypeStruct((b, s, heads * dv), BF16),
        scratch_shapes=scratch,
        compiler_params=_params("arbitrary", "arbitrary", "arbitrary"),
        name="attn_bias" if bias is not None else "attn",
    )(*args)


def _dil_kernel(q_ref, k_ref, v_ref, kp_ref, vp_ref, o_ref, lse_ref, kbuf, vbuf, *, scale, group):
    planes, tl, _ = q_ref.shape
    nb = tl // BLOCK
    c2 = scale * LOG2E
    kbuf[:, :BLOCK, :] = kp_ref[...]
    kbuf[:, BLOCK:, :] = k_ref[...]
    vbuf[:, :BLOCK, :] = vp_ref[...]
    vbuf[:, BLOCK:, :] = v_ref[...]
    row = lax.broadcasted_iota(jnp.int32, (BLOCK, 2 * BLOCK), 0)
    col = lax.broadcasted_iota(jnp.int32, (BLOCK, 2 * BLOCK), 1)
    band = (col >= row) & (col <= row + BLOCK)
    lane = lax.broadcasted_iota(jnp.int32, (BLOCK, LANES), 1)
    ones = jnp.ones((2 * BLOCK, HEAD_DIM), BF16)
    heads = [slice(h * HEAD_DIM, (h + 1) * HEAD_DIM) for h in range(DIL_HEADS)]
    first_chunk = pl.program_id(2) == 0

    def locate(g, u):
        if nb % group == 0:
            gpp = nb // group
            p = g // gpp if planes > 1 else 0
            first = (g % gpp == 0) if u == 0 else None
            return p, (g % gpp) * group + u, first
        per = group // nb
        return g * per + u // nb, u % nb, (True if u % nb == 0 else None)

    def body(g, carry):
        work, masks = [], []
        for u in range(group):
            p, n, first = locate(g, u)
            start = n * BLOCK if isinstance(n, int) else pl.multiple_of(n * BLOCK, BLOCK)
            if first is None:
                masks.append(band)
            else:
                cond = first_chunk if first is True else jnp.logical_and(first_chunk, first)
                masks.append(band & (col >= jnp.where(cond, BLOCK, 0)))
            work += [(u, p, start, h) for h in range(DIL_HEADS)]

        def scores(u, p, start, h):
            q = q_ref[p, pl.ds(start, BLOCK), heads[h]]
            kc = kbuf[p, pl.ds(start, 2 * BLOCK), heads[h]]
            return lax.dot_general(q, kc, (((1,), (1,)), ((), ())), preferred_element_type=F32)

        ahead = 2 * DIL_HEADS
        pending = [scores(*w) for w in work[:ahead]]
        tiles = {}
        for i, (u, p, start, h) in enumerate(work):
            s = pending.pop(0)
            if i + ahead < len(work):
                pending.append(scores(*work[i + ahead]))
            s = jnp.where(masks[u], s, NEG)
            m = jnp.max(s, axis=-1, keepdims=True)
            e = jnp.exp2((s - m) * c2).astype(BF16)
            v_ext = jnp.concatenate([vbuf[p, pl.ds(start, 2 * BLOCK), heads[h]], ones], axis=1)
            pv = jnp.dot(e, v_ext, preferred_element_type=F32)
            denom = pv[:, HEAD_DIM:]
            o_ref[p, pl.ds(start, BLOCK), heads[h]] = pv[:, :HEAD_DIM] / denom
            tile = tiles.get(u, jnp.zeros((BLOCK, LANES), F32))
            tiles[u] = jnp.where(lane == h, m * scale + jnp.log(denom), tile)
            if h == DIL_HEADS - 1:
                lse_ref[p, pl.ds(start, BLOCK), :] = tiles.pop(u)
        return carry

    n_groups = planes * nb // group
    if n_groups == 1:
        body(0, 0)
    else:
        lax.fori_loop(0, n_groups, body, 0)


def _dil_branch(qkv, *, scale, rows=1024, group=4):
    b, dil, sub, w3 = qkv.shape
    w = w3 // 3
    tl = min(rows, sub)
    planes = min(rows // tl, dil)
    bpc = tl // BLOCK
    group = min(group, planes * bpc)
    assert (bpc % group == 0 or group % bpc == 0) and (planes * bpc) % group == 0
    own = lambda j: pl.BlockSpec((None, planes, tl, w), lambda bi, r, c: (bi, r, c, j))
    prev = lambda j: pl.BlockSpec((None, planes, BLOCK, w),
                                  lambda bi, r, c: (bi, r, jnp.maximum(c * bpc - 1, 0), j))
    return pl.pallas_call(
        functools.partial(_dil_kernel, scale=scale, group=group),
        grid=(b, dil // planes, sub // tl),
        in_specs=[own(0), own(1), own(2), prev(1), prev(2)],
        out_specs=[
            pl.BlockSpec((None, planes, tl, w), lambda bi, r, c: (bi, r, c, 0)),
            pl.BlockSpec((None, planes, tl, LANES), lambda bi, r, c: (bi, r, c, 0)),
        ],
        out_shape=[
            jax.ShapeDtypeStruct((b, dil, sub, w), F32),
            jax.ShapeDtypeStruct((b, dil, sub, LANES), F32),
        ],
        scratch_shapes=[pltpu.VMEM((planes, tl + BLOCK, w), BF16)] * 2,
        compiler_params=_params("parallel", "parallel", "parallel"),
        name=f"dilated_{dil}",
    )(qkv, qkv, qkv, qkv, qkv)


def _out_proj_kernel(x_ref, a_ref, b_ref, *rest, dils):
    nbr = len(dils)
    o_refs, l_refs = rest[:nbr], rest[nbr:2 * nbr]
    w_ref, y_ref = rest[2 * nbr:2 * nbr + 2]
    scr = rest[2 * nbr + 2:]
    tm = x_ref.shape[0]

    def natural(ref, dil, slot, j):
        sl = slice(j * LANES, (j + 1) * LANES)
        if dil == 1:
            return ref[0, :, sl]
        rows = tm // dil
        for r in range(dil):
            scr[slot][pl.ds(r, rows, stride=dil), :] = ref[r, :, sl]
        return scr[slot][...]

    lses = [natural(l_refs[i], dils[i], 2 * i, 0) for i in range(nbr)]
    m = functools.reduce(jnp.maximum, lses)
    es = [jnp.exp(l - m) for l in lses]
    tot = functools.reduce(lambda u, v: u + v, es)
    wts = [e / tot for e in es]
    parts = [a_ref[...], b_ref[...]]
    for h in range(DIL_HEADS):
        c = None
        for i in range(nbr):
            term = wts[i][:, h:h + 1] * natural(o_refs[i], dils[i], 2 * i + 1, h)
            c = term if c is None else c + term
        parts.append(c.astype(BF16))
    mixed = jnp.concatenate(parts, axis=-1)
    y_ref[...] = x_ref[...] + jnp.dot(mixed, w_ref[...], preferred_element_type=F32)


def _out_proj(x, a, bm, os_, ls_, w, *, layer, tm=512):
    b, s, d = x.shape
    tm = min(tm, s)
    dils = tuple(o.shape[1] for o in os_)
    row = lambda arr: pl.BlockSpec((None, tm, arr.shape[2]), lambda bi, i: (bi, i, 0))
    plane = lambda arr: pl.BlockSpec((None, arr.shape[1], tm // arr.shape[1], arr.shape[3]),
                                     lambda bi, i: (bi, 0, i, 0))
    return pl.pallas_call(
        functools.partial(_out_proj_kernel, dils=dils),
        grid=(b, s // tm),
        in_specs=[row(x), row(a), row(bm)] + [plane(o) for o in os_] + [plane(l) for l in ls_]
                 + [_layer_block(w, layer)],
        out_specs=row(x),
        out_shape=jax.ShapeDtypeStruct((b, s, d), F32),
        scratch_shapes=[pltpu.VMEM((tm, LANES), F32)] * (2 * len(dils)),
        compiler_params=_params("parallel", "parallel"),
        name="out_proj",
    )(x, a, bm, *os_, *ls_, w)


def _rope_table(seq, dim, fill):
    half = dim // 2
    inv = 1.0 / (ROPE_THETA ** (jnp.arange(0, dim, 2, dtype=F32) / dim))
    ang = jnp.arange(seq, dtype=F32)[:, None] * inv[None, :]
    cos, sin = jnp.cos(ang), jnp.sin(ang)
    rest = LANES - dim
    zeros_h = jnp.zeros((seq, half), F32)
    zeros_r = jnp.zeros((seq, rest), F32)
    cos_t = jnp.concatenate([cos, cos, jnp.full((seq, rest), fill, F32)], axis=1)
    s_hi = jnp.concatenate([-sin, zeros_h, zeros_r], axis=1)
    s_lo = jnp.concatenate([zeros_h, sin, zeros_r], axis=1)
    return jnp.stack([cos_t, s_hi, s_lo], axis=0)


_FW = FOX_HEADS * HEAD_DIM
_O_FL = 3 * _FW
_O_CQ = _O_FL + FOX_HEADS
_O_KR = _O_CQ + MLA_Q_RANK + MLA_KV_RANK
_O_DQ = _O_KR + MLA_ROPE


def _split_w_in_kernel(w_ref, fqk_ref, fv_ref, fl_ref, c_ref, dil_ref):
    w = w_ref[...]
    lane = lax.broadcasted_iota(jnp.int32, (w.shape[0], LANES), 1)
    fqk_ref[...] = w[:, :2 * _FW].astype(BF16)
    fv_ref[...] = w[:, 2 * _FW:_O_FL].astype(BF16)
    fl_ref[...] = jnp.where(lane < FOX_HEADS, w[:, _O_FL:_O_FL + LANES], 0.0).astype(BF16)
    c_ref[:, :_O_KR - _O_CQ] = w[:, _O_CQ:_O_KR].astype(BF16)
    c_ref[:, _O_KR - _O_CQ:] = jnp.where(lane < MLA_ROPE, w[:, _O_KR:_O_KR + LANES], 0.0).astype(BF16)
    dil_ref[...] = w[:, _O_DQ:].astype(BF16)


def _split_w_in(w_in, *, rows=256):
    depth, d, n_in = w_in.shape
    n_c = _O_KR - _O_CQ + LANES
    widths = (2 * _FW, _FW, LANES, n_c, n_in - _O_DQ)
    return pl.pallas_call(
        _split_w_in_kernel,
        grid=(depth, d // rows),
        in_specs=[pl.BlockSpec((None, rows, n_in), lambda l, i: (l, i, 0))],
        out_specs=[pl.BlockSpec((None, rows, n), lambda l, i: (l, i, 0)) for n in widths],
        out_shape=[jax.ShapeDtypeStruct((depth, d, n), BF16) for n in widths],
        compiler_params=_params("parallel", "parallel"),
        name="split_w_in",
    )(w_in)


def kernel(x, ffn1_norm, ffn1_w_gate, ffn1_w_up, ffn1_w_down, mix_norm, w_in, fox_forget_bias, mla_q_norm, mla_kv_norm, mla_w_uq, mla_w_ukv, w_out, ffn2_norm, ffn2_w_gate, ffn2_w_up, ffn2_w_down, final_norm):
    b, s, d = x.shape
    depth = w_in.shape[0]
    t = b * s

    assert all(window == BLOCK * dil and s % (BLOCK * dil) == 0 for window, dil in DIL_BRANCHES)
    tab_mla = _rope_table(s, MLA_ROPE, 1.0)
    tab_dil = _rope_table(s, PARTIAL_ROPE, 1.0)

    pad_last = lambda w, n: jnp.pad(w, [(0, 0)] * (w.ndim - 1) + [(0, n - w.shape[-1])])
    w_fqk_all, w_fv_all, w_fl_all, w_c_all, w_dil_all = _split_w_in(w_in)
    fbias_all = pad_last(fox_forget_bias.reshape(depth, 1, FOX_HEADS), LANES)
    wuq_all = mla_w_uq.reshape(depth, MLA_Q_RANK, MLA_HEADS, MLA_NOPE + MLA_ROPE)
    wuq_all = pad_last(wuq_all, 2 * LANES).reshape(depth, MLA_Q_RANK, MLA_HEADS * 2 * LANES).astype(BF16)
    wukv = mla_w_ukv.reshape(depth, MLA_KV_RANK, MLA_HEADS, MLA_NOPE + HEAD_DIM)
    w_k_all = wukv[..., :MLA_NOPE].reshape(depth, MLA_KV_RANK, MLA_HEADS * MLA_NOPE).astype(BF16)
    w_v_all = wukv[..., MLA_NOPE:].reshape(depth, MLA_KV_RANK, MLA_HEADS * HEAD_DIM).astype(BF16)
    w_out_all = w_out.astype(BF16)

    ffn1_w = [w.astype(BF16) for w in (ffn1_w_gate, ffn1_w_up, ffn1_w_down)]
    ffn2_w = [w.astype(BF16) for w in (ffn2_w_gate, ffn2_w_up, ffn2_w_down)]

    gq_all = mla_q_norm.reshape(depth, 1, MLA_Q_RANK)
    gkv_all = mla_kv_norm.reshape(depth, 1, MLA_KV_RANK)

    xf = x.reshape(t, d)
    for l in range(depth):
        last = l == depth - 1

        xf, hn = _ffn(xf, ffn1_norm[l], *ffn1_w, mix_norm[l], layer=l, emit_norm=True)

        hn = hn.reshape(b, s, d)
        fox_scale = HEAD_DIM ** -0.5
        fqk, fvt, logf = _fox_proj(hn, w_fqk_all, w_fv_all, w_fl_all, fbias_all, layer=l)
        logf = logf[:, :, :FOX_HEADS].transpose(0, 2, 1).reshape(b * FOX_HEADS, s)
        key_bias = _key_bias(logf, fox_scale).reshape(b, FOX_HEADS, 1, s)
        out_a = _attention(fqk, fqk, fvt, key_bias, dk=HEAD_DIM, q_off=0, k_off=FOX_HEADS, scale=fox_scale)

        q_b, k_b, vt_b = _mla_proj(hn, w_c_all, gq_all, gkv_all, wuq_all, w_k_all, w_v_all, tab_mla, layer=l)
        out_b = _attention(q_b, k_b, vt_b, None, dk=2 * LANES, q_off=0, k_off=0,
                           scale=(MLA_NOPE + MLA_ROPE) ** -0.5)

        dqkvs = _dil_proj(hn, w_dil_all, tab_dil, layer=l, dils=tuple(dil for _, dil in DIL_BRANCHES))
        outs, lses = [], []
        for dqkv in dqkvs:
            o, lse = _dil_branch(dqkv, scale=HEAD_DIM ** -0.5)
            outs.append(o)
            lses.append(lse)

        xf = _out_proj(xf.reshape(b, s, d), out_a, out_b, outs, lses, w_out_all, layer=l).reshape(t, d)

        xf = _ffn(xf, ffn2_norm[l], *ffn2_w, final_norm if last else None, layer=l, final_norm=last)
    return xf.reshape(b, s, d)
```

```python
import functools

import jax
import jax.numpy as jnp
from jax import lax
from jax.experimental import pallas as pl
from jax.experimental.pallas import tpu as pltpu

F32 = jnp.float32
BF16 = jnp.bfloat16

LANES = 128
HEAD_DIM = 128
BLOCK = 128
EPS = 1e-6
ROPE_THETA = 500000.0
PARTIAL_ROPE = HEAD_DIM // 4
FOX_HEADS = 4
MLA_HEADS = 8
MLA_Q_RANK = 512
MLA_KV_RANK = 512
MLA_NOPE = 128
MLA_ROPE = 64
DIL_HEADS = 4
DIL_BRANCHES = ((128, 1), (512, 4), (2048, 16))
NEG = -1e30
LOG2E = 1.4426950408889634
EXP2_HEADROOM = 100.0
VMEM_LIMIT = 56 * 1024 * 1024
FFN_TILE = 512


def _params(*sem):
    return pltpu.CompilerParams(dimension_semantics=sem, vmem_limit_bytes=VMEM_LIMIT)


def _layer_block(stacked, layer):
    return pl.BlockSpec((None,) + stacked.shape[1:], lambda *_: (layer,) + (0,) * (stacked.ndim - 1))


def _rms(x, g):
    return x * lax.rsqrt(jnp.mean(x * x, axis=-1, keepdims=True) + EPS) * g


def _apply_rope(x, cos, s_hi, s_lo, half):
    return x * cos + pltpu.roll(x, LANES - half, 1) * s_hi + pltpu.roll(x, half, 1) * s_lo


def _ffn_kernel(*refs, emit_norm, final_norm):
    x_hbm, g_ref, wg_ref, wu_ref, wd_ref = refs[:5]
    pos = 5
    g2_ref = None
    if emit_norm or final_norm:
        g2_ref = refs[pos]
        pos += 1
    o_ref = refs[pos]
    pos += 1
    hn_ref = None
    if emit_norm:
        hn_ref = refs[pos]
        pos += 1
    xn_ref, xbuf, sem = refs[pos:pos + 3]

    i, j = pl.program_id(0), pl.program_id(1)
    ni, nj = pl.num_programs(0), pl.num_programs(1)
    tm = xbuf.shape[0]

    def x_copy(tile):
        return pltpu.make_async_copy(x_hbm.at[pl.ds(pl.multiple_of(tile * tm, tm), tm), :], xbuf, sem)

    @pl.when(jnp.logical_and(i == 0, j == 0))
    def _():
        x_copy(0).start()

    def ff_tile():
        xn = xn_ref[...]
        gate = jnp.dot(xn, wg_ref[...], preferred_element_type=F32)
        up = jnp.dot(xn, wu_ref[...], preferred_element_type=F32)
        h = (gate * jax.nn.sigmoid(gate) * up * 0.5).astype(BF16)
        return jnp.dot(h, wd_ref[...], preferred_element_type=F32)

    @pl.when(j == 0)
    def _():
        x_copy(i).wait()
        xn_ref[...] = _rms(xbuf[...], g_ref[...]).astype(BF16)
        o_ref[...] = xbuf[...] + ff_tile()

    @pl.when(jnp.logical_and(j == 1, i + 1 < ni))
    def _():
        x_copy(i + 1).start()

    @pl.when(j > 0)
    def _():
        o_ref[...] += ff_tile()

    if emit_norm or final_norm:
        @pl.when(j == nj - 1)
        def _():
            y = _rms(o_ref[...], g2_ref[...])
            if emit_norm:
                hn_ref[...] = y.astype(BF16)
            else:
                o_ref[...] = y


def _ffn(x, g, wg, wu, wd, g2=None, *, layer, emit_norm=False, final_norm=False, tm=1024, tf=FFN_TILE):
    t, d = x.shape
    nj = wd.shape[1] // tf
    tm = min(tm, t)
    assert nj >= 2
    in_specs = [
        pl.BlockSpec(memory_space=pl.ANY),
        pl.BlockSpec((1, d), lambda i, j: (0, 0)),
        pl.BlockSpec((None, d, tf), lambda i, j: (layer, 0, j)),
        pl.BlockSpec((None, d, tf), lambda i, j: (layer, 0, j)),
        pl.BlockSpec((None, tf, d), lambda i, j: (layer, j, 0)),
    ]
    args = [x, g.reshape(1, d), wg, wu, wd]
    if emit_norm or final_norm:
        in_specs.append(pl.BlockSpec((1, d), lambda i, j: (0, 0)))
        args.append(g2.reshape(1, d))
    out_shape = [jax.ShapeDtypeStruct((t, d), F32)]
    out_specs = [pl.BlockSpec((tm, d), lambda i, j: (i, 0))]
    if emit_norm:
        out_shape.append(jax.ShapeDtypeStruct((t, d), BF16))
        out_specs.append(pl.BlockSpec((tm, d), lambda i, j: (i, 0)))
    res = pl.pallas_call(
        functools.partial(_ffn_kernel, emit_norm=emit_norm, final_norm=final_norm),
        grid=(t // tm, nj),
        in_specs=in_specs,
        out_specs=out_specs,
        out_shape=out_shape,
        scratch_shapes=[pltpu.VMEM((tm, d), BF16), pltpu.VMEM((tm, d), F32), pltpu.SemaphoreType.DMA],
        compiler_params=_params("arbitrary", "arbitrary"),
        name="ffn",
    )(*args)
    return res if emit_norm else res[0]


def _store_heads_transposed(vt_ref, v):
    for h in range(vt_ref.shape[0]):
        vt_ref[h] = v[:, h * HEAD_DIM:(h + 1) * HEAD_DIM].T.astype(BF16)


def _fox_proj_kernel(hn_ref, wqk_ref, wv_ref, wl_ref, b_ref, qk_ref, vt_ref, lf_ref):
    hn = hn_ref[...]
    v = jnp.dot(hn, wv_ref[...], preferred_element_type=F32)
    z = jnp.dot(hn, wl_ref[...], preferred_element_type=F32) + b_ref[...]
    qk = jnp.dot(hn, wqk_ref[...], preferred_element_type=F32)
    _store_heads_transposed(vt_ref, v)
    lf_ref[...] = jnp.minimum(z, 0.0) - jnp.log1p(jnp.exp(-jnp.abs(z)))
    qk_ref[...] = qk.astype(BF16)


def _fox_proj(hn, wqk, wv, wl, bias, *, layer, tm=512):
    b, s, d = hn.shape
    tm = min(tm, s)
    n = wqk.shape[2]
    heads = wv.shape[2] // HEAD_DIM
    full = lambda a: _layer_block(a, layer)
    return pl.pallas_call(
        _fox_proj_kernel,
        grid=(b, s // tm),
        in_specs=[pl.BlockSpec((None, tm, d), lambda bi, i: (bi, i, 0)), full(wqk), full(wv), full(wl), full(bias)],
        out_specs=[
            pl.BlockSpec((None, tm, n), lambda bi, i: (bi, i, 0)),
            pl.BlockSpec((None, heads, HEAD_DIM, tm), lambda bi, i: (bi, 0, 0, i)),
            pl.BlockSpec((None, tm, LANES), lambda bi, i: (bi, i, 0)),
        ],
        out_shape=[
            jax.ShapeDtypeStruct((b, s, n), BF16),
            jax.ShapeDtypeStruct((b, heads, HEAD_DIM, s), BF16),
            jax.ShapeDtypeStruct((b, s, LANES), F32),
        ],
        compiler_params=_params("parallel", "parallel"),
        name="fox_proj",
    )(hn, wqk, wv, wl, bias)


def _mla_proj_kernel(hn_ref, wc_ref, gq_ref, gkv_ref, wuq_ref, wk_ref, wv_ref, tab_ref,
                     q_ref, k_ref, vt_ref):
    hn = hn_ref[...]
    o_kv, o_kr = MLA_Q_RANK, MLA_Q_RANK + MLA_KV_RANK
    cq = jnp.dot(hn, wc_ref[:, :o_kv], preferred_element_type=F32)
    ckv = jnp.dot(hn, wc_ref[:, o_kv:o_kr], preferred_element_type=F32)
    cqn = _rms(cq, gq_ref[...]).astype(BF16)
    kr_raw = jnp.dot(hn, wc_ref[:, o_kr:], preferred_element_type=F32)
    q = jnp.dot(cqn, wuq_ref[...], preferred_element_type=F32)
    ckvn = _rms(ckv, gkv_ref[...]).astype(BF16)
    cos, s_hi, s_lo = tab_ref[0], tab_ref[1], tab_ref[2]
    half = MLA_ROPE // 2
    kr = _apply_rope(kr_raw, cos, s_hi, s_lo, half).astype(BF16)
    v = jnp.dot(ckvn, wv_ref[...], preferred_element_type=F32)
    kn = jnp.dot(ckvn, wk_ref[...], preferred_element_type=F32)
    _store_heads_transposed(vt_ref, v)
    for h in range(MLA_HEADS):
        lo = 2 * h * LANES
        q_ref[:, lo:lo + LANES] = q[:, lo:lo + LANES].astype(BF16)
        q_ref[:, lo + LANES:lo + 2 * LANES] = _apply_rope(
            q[:, lo + LANES:lo + 2 * LANES], cos, s_hi, s_lo, half).astype(BF16)
        k_ref[:, lo:lo + LANES] = kn[:, h * LANES:(h + 1) * LANES].astype(BF16)
        k_ref[:, lo + LANES:lo + 2 * LANES] = kr


def _mla_proj(hn, wc, gq, gkv, wuq, wk, wv, tab, *, layer, tm=512):
    b, s, d = hn.shape
    tm = min(tm, s)
    nq = wuq.shape[2]
    heads = wv.shape[2] // HEAD_DIM
    full = lambda a: _layer_block(a, layer)
    return pl.pallas_call(
        _mla_proj_kernel,
        grid=(b, s // tm),
        in_specs=[
            pl.BlockSpec((None, tm, d), lambda bi, i: (bi, i, 0)),
            full(wc), full(gq), full(gkv), full(wuq), full(wk), full(wv),
            pl.BlockSpec((3, tm, LANES), lambda bi, i: (0, i, 0)),
        ],
        out_specs=[
            pl.BlockSpec((None, tm, nq), lambda bi, i: (bi, i, 0)),
            pl.BlockSpec((None, tm, nq), lambda bi, i: (bi, i, 0)),
            pl.BlockSpec((None, heads, HEAD_DIM, tm), lambda bi, i: (bi, 0, 0, i)),
        ],
        out_shape=[
            jax.ShapeDtypeStruct((b, s, nq), BF16),
            jax.ShapeDtypeStruct((b, s, nq), BF16),
            jax.ShapeDtypeStruct((b, heads, HEAD_DIM, s), BF16),
        ],
        compiler_params=_params("parallel", "parallel"),
        name="mla_proj",
    )(hn, wc, gq, gkv, wuq, wk, wv, tab)


def _dil_proj_kernel(hn_ref, w_ref, tab_ref, *rest, dils):
    out_refs, scr = rest[:len(dils)], rest[len(dils):]
    hn = hn_ref[...]
    cos, s_hi, s_lo = tab_ref[0], tab_ref[1], tab_ref[2]
    half = PARTIAL_ROPE // 2
    n_rot = 2 * DIL_HEADS
    tm = hn.shape[0]
    groups = [(0, DIL_HEADS), (DIL_HEADS, n_rot), (n_rot, n_rot + DIL_HEADS // 2),
              (n_rot + DIL_HEADS // 2, n_rot + DIL_HEADS)]

    def project(lo, hi):
        return jnp.dot(hn, w_ref[:, lo * LANES:hi * LANES], preferred_element_type=F32)

    def emit(lo, hi, y):
        for j in range(lo, hi):
            v = y[:, (j - lo) * LANES:(j - lo + 1) * LANES]
            scr[j][...] = _apply_rope(v, cos, s_hi, s_lo, half) if j < n_rot else v
            for o_ref, dil in zip(out_refs, dils):
                rows = tm // dil
                for r in range(dil):
                    o_ref[r, :, j * LANES:(j + 1) * LANES] = scr[j][pl.ds(r, rows, stride=dil), :].astype(BF16)

    ys = [project(*groups[0])]
    for g, (lo, hi) in enumerate(groups):
        if g + 1 < len(groups):
            ys.append(project(*groups[g + 1]))
        emit(lo, hi, ys[g])


def _dil_proj(hn, w, tab, *, layer, dils, tm=512):
    b, s, d = hn.shape
    tm = min(tm, s)
    n = w.shape[2]
    return pl.pallas_call(
        functools.partial(_dil_proj_kernel, dils=dils),
        grid=(b, s // tm),
        in_specs=[
            pl.BlockSpec((None, tm, d), lambda bi, i: (bi, i, 0)),
            _layer_block(w, layer),
            pl.BlockSpec((3, tm, LANES), lambda bi, i: (0, i, 0)),
        ],
        out_specs=[pl.BlockSpec((None, dil, tm // dil, n), lambda bi, i: (bi, 0, i, 0)) for dil in dils],
        out_shape=[jax.ShapeDtypeStruct((b, dil, s // dil, n), BF16) for dil in dils],
        scratch_shapes=[pltpu.VMEM((tm, LANES), F32)] * (n // LANES),
        compiler_params=_params("parallel", "parallel"),
        name="dil_proj",
    )(hn, w, tab)


def _cumsum_kernel(x_ref, o_ref, *, out_scale):
    x = x_ref[...]
    n = x.shape[-1]
    lane = lax.broadcasted_iota(jnp.int32, x.shape, 1)
    k = 1
    while k < n:
        x = x + jnp.where(lane >= k, pltpu.roll(x, k, 1), 0.0)
        k *= 2
    o_ref[...] = x * out_scale


def _key_bias(x, scale):
    return pl.pallas_call(
        functools.partial(_cumsum_kernel, out_scale=-1.0 / scale),
        out_shape=jax.ShapeDtypeStruct(x.shape, F32),
        compiler_params=pltpu.CompilerParams(vmem_limit_bytes=VMEM_LIMIT),
        name="forget_cumsum",
    )(x)


def _attn_kernel(*refs, tq, cq, tk, ahead, scale, has_bias):
    if has_bias:
        q_ref, k_ref, vt_ref, b_ref, o_ref, m_ref, acc_ref, worst_ref, brep_ref = refs
    else:
        q_ref, k_ref, vt_ref, o_ref, m_ref, acc_ref, worst_ref = refs
    qi = pl.program_id(2)
    n_chain = tq // cq
    dv = vt_ref.shape[0]
    c2 = scale * LOG2E

    if has_bias:
        @pl.when(qi == 0)
        def _():
            for j in range(brep_ref.shape[0] // tq):
                rows = jnp.broadcast_to(b_ref[:, j * tq:(j + 1) * tq], (LANES, tq))
                brep_ref[j * tq:(j + 1) * tq, :] = rows.T

    def scores(c, ks, size):
        q = q_ref[c * cq:(c + 1) * cq, :]
        k = k_ref[pl.ds(ks, size), :]
        return lax.dot_general(k, q, (((1,), (1,)), ((), ())), preferred_element_type=F32)

    def update(c, st, ks, size, exact):
        if has_bias:
            st = st + jnp.tile(brep_ref[pl.ds(ks, size), :], (1, cq // LANES))
        m_prev = m_ref[c]
        top = jnp.max(st, axis=0, keepdims=True)
        m_new = jnp.maximum(m_prev, top)
        alpha = jnp.exp2((m_prev - m_new) * c2)
        v_ext = jnp.concatenate([vt_ref[:, pl.ds(ks, size)], jnp.ones((dv, size), BF16)], axis=0)
        if exact:
            pt = jnp.exp2((st - m_new) * c2).astype(BF16)
            acc_ref[c] = alpha * acc_ref[c] + jnp.dot(v_ext, pt, preferred_element_type=F32)
        else:
            pt = jnp.exp2((st - m_prev) * c2).astype(BF16)
            acc_ref[c] = alpha * (acc_ref[c] + jnp.dot(v_ext, pt, preferred_element_type=F32))
            worst_ref[c] = jnp.maximum(worst_ref[c], (top - m_prev) * c2)
        m_ref[c] = m_new

    def run(work, ahead):
        pending = [scores(*w[:3]) for w in work[:ahead]]
        for i, (c, ks, size, exact) in enumerate(work):
            st = pending.pop(0)
            if i + ahead < len(work):
                pending.append(scores(*work[i + ahead][:3]))
            update(c, st, ks, size, exact)

    def diagonal_tile():
        work = [(c, pl.multiple_of(qi * tq + j * cq, cq), c == j)
                for j in range(n_chain) for c in range(j, n_chain)]
        look = 4
        pending = [scores(c, ks, cq) for c, ks, _ in work[:look]]
        partial = [[] for _ in range(n_chain)]
        for i, (c, ks, on_diagonal) in enumerate(work):
            st = pending.pop(0)
            if i + look < len(work):
                pending.append(scores(work[i + look][0], work[i + look][1], cq))
            if has_bias:
                st = st + jnp.tile(brep_ref[pl.ds(ks, cq), :], (1, cq // LANES))
            if on_diagonal:
                key = lax.broadcasted_iota(jnp.int32, st.shape, 0)
                qry = lax.broadcasted_iota(jnp.int32, st.shape, 1)
                st = jnp.where(key <= qry, st, NEG)
            top = jnp.max(st, axis=0, keepdims=True)
            pt = jnp.exp2((st - top) * c2).astype(BF16)
            v_ext = jnp.concatenate([vt_ref[:, pl.ds(ks, cq)], jnp.ones((dv, cq), BF16)], axis=0)
            partial[c].append((top, jnp.dot(v_ext, pt, preferred_element_type=F32)))
        for c in range(n_chain):
            m_all = functools.reduce(jnp.maximum, [m for m, _ in partial[c]])
            acc = None
            for m, pv in partial[c]:
                term = jnp.exp2((m - m_all) * c2) * pv
                acc = term if acc is None else acc + term
            m_ref[c] = m_all
            acc_ref[c] = acc

    def attend(exact):
        diagonal_tile()

        def body(j, carry):
            run([(c, pl.multiple_of(j * tq + part * tk, tk), tk, exact)
                 for part in range(tq // tk) for c in range(n_chain)], ahead)
            return carry

        lax.fori_loop(0, qi, body, 0)

    worst_ref[...] = jnp.zeros(worst_ref.shape, F32)
    attend(exact=False)

    @pl.when(jnp.max(worst_ref[...]) > EXP2_HEADROOM)
    def _():
        attend(exact=True)

    for c in range(n_chain):
        acc = acc_ref[c]
        o_ref[c * cq:(c + 1) * cq, :] = (acc[:dv] / acc[dv:]).T.astype(o_ref.dtype)


def _attention(q_arr, k_arr, vt_arr, bias, *, dk, q_off, k_off, scale, tq=1024, cq=256, tk=1024, ahead=2):
    b, s, _ = q_arr.shape
    heads, dv = vt_arr.shape[1], vt_arr.shape[2]
    tq = min(tq, s)
    cq = min(cq, tq)
    n_chain = tq // cq
    in_specs = [
        pl.BlockSpec((None, tq, dk), lambda bi, h, qi: (bi, qi, q_off + h)),
        pl.BlockSpec((None, s, dk), lambda bi, h, qi: (bi, 0, k_off + h)),
        pl.BlockSpec((None, None, dv, s), lambda bi, h, qi: (bi, h, 0, 0)),
    ]
    args = [q_arr, k_arr, vt_arr]
    stat = pltpu.VMEM((n_chain, 1, cq), F32)
    scratch = [stat, pltpu.VMEM((n_chain, 2 * dv, cq), F32), stat]
    if bias is not None:
        in_specs.append(pl.BlockSpec((None, None, 1, s), lambda bi, h, qi: (bi, h, 0, 0)))
        args.append(bias)
        scratch.append(pltpu.VMEM((s, LANES), F32))
    return pl.pallas_call(
        functools.partial(_attn_kernel, tq=tq, cq=cq, tk=min(tk, tq), ahead=ahead, scale=scale,
                          has_bias=bias is not None),
        grid=(b, heads, s // tq),
        in_specs=in_specs,
        out_specs=pl.BlockSpec((None, tq, dv), lambda bi, h, qi: (bi, qi, h)),
        out_shape=jax.ShapeDtypeStruct((b, s, heads * dv), BF16),
        scratch_shapes=scratch,
        compiler_params=_params("arbitrary", "arbitrary", "arbitrary"),
        name="attn_bias" if bias is not None else "attn",
    )(*args)


def _dil_kernel(q_ref, k_ref, v_ref, kp_ref, vp_ref, o_ref, lse_ref, kbuf, vbuf, *, scale, group):
    planes, tl, _ = q_ref.shape
    nb = tl // BLOCK
    c2 = scale * LOG2E
    kbuf[:, :BLOCK, :] = kp_ref[...]
    kbuf[:, BLOCK:, :] = k_ref[...]
    vbuf[:, :BLOCK, :] = vp_ref[...]
    vbuf[:, BLOCK:, :] = v_ref[...]
    row = lax.broadcasted_iota(jnp.int32, (BLOCK, 2 * BLOCK), 0)
    col = lax.broadcasted_iota(jnp.int32, (BLOCK, 2 * BLOCK), 1)
    band = (col >= row) & (col <= row + BLOCK)
    lane = lax.broadcasted_iota(jnp.int32, (BLOCK, LANES), 1)
    ones = jnp.ones((2 * BLOCK, HEAD_DIM), BF16)
    heads = [slice(h * HEAD_DIM, (h + 1) * HEAD_DIM) for h in range(DIL_HEADS)]
    first_chunk = pl.program_id(2) == 0

    def locate(g, u):
        if nb % group == 0:
            gpp = nb // group
            p = g // gpp if planes > 1 else 0
            first = (g % gpp == 0) if u == 0 else None
            return p, (g % gpp) * group + u, first
        per = group // nb
        return g * per + u // nb, u % nb, (True if u % nb == 0 else None)

    def body(g, carry):
        work, masks = [], []
        for u in range(group):
            p, n, first = locate(g, u)
            start = n * BLOCK if isinstance(n, int) else pl.multiple_of(n * BLOCK, BLOCK)
            if first is None:
                masks.append(band)
            else:
                cond = first_chunk if first is True else jnp.logical_and(first_chunk, first)
                masks.append(band & (col >= jnp.where(cond, BLOCK, 0)))
            work += [(u, p, start, h) for h in range(DIL_HEADS)]

        def scores(u, p, start, h):
            q = q_ref[p, pl.ds(start, BLOCK), heads[h]]
            kc = kbuf[p, pl.ds(start, 2 * BLOCK), heads[h]]
            return lax.dot_general(q, kc, (((1,), (1,)), ((), ())), preferred_element_type=F32)

        ahead = 2 * DIL_HEADS
        pending = [scores(*w) for w in work[:ahead]]
        tiles = {}
        for i, (u, p, start, h) in enumerate(work):
            s = pending.pop(0)
            if i + ahead < len(work):
                pending.append(scores(*work[i + ahead]))
            s = jnp.where(masks[u], s, NEG)
            m = jnp.max(s, axis=-1, keepdims=True)
            e = jnp.exp2((s - m) * c2).astype(BF16)
            v_ext = jnp.concatenate([vbuf[p, pl.ds(start, 2 * BLOCK), heads[h]], ones], axis=1)
            pv = jnp.dot(e, v_ext, preferred_element_type=F32)
            denom = pv[:, HEAD_DIM:]
            o_ref[p, pl.ds(start, BLOCK), heads[h]] = pv[:, :HEAD_DIM] / denom
            tile = tiles.get(u, jnp.zeros((BLOCK, LANES), F32))
            tiles[u] = jnp.where(lane == h, m * scale + jnp.log(denom), tile)
            if h == DIL_HEADS - 1:
                lse_ref[p, pl.ds(start, BLOCK), :] = tiles.pop(u)
        return carry

    n_groups = planes * nb // group
    if n_groups == 1:
        body(0, 0)
    else:
        lax.fori_loop(0, n_groups, body, 0)


def _dil_branch(qkv, *, scale, rows=1024, group=4):
    b, dil, sub, w3 = qkv.shape
    w = w3 // 3
    tl = min(rows, sub)
    planes = min(rows // tl, dil)
    bpc = tl // BLOCK
    group = min(group, planes * bpc)
    assert (bpc % group == 0 or group % bpc == 0) and (planes * bpc) % group == 0
    own = lambda j: pl.BlockSpec((None, planes, tl, w), lambda bi, r, c: (bi, r, c, j))
    prev = lambda j: pl.BlockSpec((None, planes, BLOCK, w),
                                  lambda bi, r, c: (bi, r, jnp.maximum(c * bpc - 1, 0), j))
    return pl.pallas_call(
        functools.partial(_dil_kernel, scale=scale, group=group),
        grid=(b, dil // planes, sub // tl),
        in_specs=[own(0), own(1), own(2), prev(1), prev(2)],
        out_specs=[
            pl.BlockSpec((None, planes, tl, w), lambda bi, r, c: (bi, r, c, 0)),
            pl.BlockSpec((None, planes, tl, LANES), lambda bi, r, c: (bi, r, c, 0)),
        ],
        out_shape=[
            jax.ShapeDtypeStruct((b, dil, sub, w), F32),
            jax.ShapeDtypeStruct((b, dil, sub, LANES), F32),
        ],
        scratch_shapes=[pltpu.VMEM((planes, tl + BLOCK, w), BF16)] * 2,
        compiler_params=_params("parallel", "parallel", "parallel"),
        name=f"dilated_{dil}",
    )(qkv, qkv, qkv, qkv, qkv)


def _out_proj_kernel(x_ref, a_ref, b_ref, *rest, dils):
    nbr = len(dils)
    o_refs, l_refs = rest[:nbr], rest[nbr:2 * nbr]
    w_ref, y_ref = rest[2 * nbr:2 * nbr + 2]
    scr = rest[2 * nbr + 2:]
    tm = x_ref.shape[0]

    def natural(ref, dil, slot, j):
        sl = slice(j * LANES, (j + 1) * LANES)
        if dil == 1:
            return ref[0, :, sl]
        rows = tm // dil
        for r in range(dil):
            scr[slot][pl.ds(r, rows, stride=dil), :] = ref[r, :, sl]
        return scr[slot][...]

    ab = jnp.concatenate([a_ref[...], b_ref[...]], axis=-1)
    n_ab = ab.shape[1]
    y = x_ref[...] + jnp.dot(ab, w_ref[:n_ab, :], preferred_element_type=F32)

    lses = [natural(l_refs[i], dils[i], 2 * i, 0) for i in range(nbr)]
    m = functools.reduce(jnp.maximum, lses)
    es = [jnp.exp(l - m) for l in lses]
    tot = functools.reduce(lambda u, v: u + v, es)
    wts = [e / tot for e in es]
    parts = []
    for h in range(DIL_HEADS):
        c = None
        for i in range(nbr):
            term = wts[i][:, h:h + 1] * natural(o_refs[i], dils[i], 2 * i + 1, h)
            c = term if c is None else c + term
        parts.append(c.astype(BF16))
    mixed = jnp.concatenate(parts, axis=-1)
    y_ref[...] = y + jnp.dot(mixed, w_ref[n_ab:, :], preferred_element_type=F32)


def _out_proj(x, a, bm, os_, ls_, w, *, layer, tm=512):
    b, s, d = x.shape
    tm = min(tm, s)
    dils = tuple(o.shape[1] for o in os_)
    row = lambda arr: pl.BlockSpec((None, tm, arr.shape[2]), lambda bi, i: (bi, i, 0))
    plane = lambda arr: pl.BlockSpec((None, arr.shape[1], tm // arr.shape[1], arr.shape[3]),
                                     lambda bi, i: (bi, 0, i, 0))
    return pl.pallas_call(
        functools.partial(_out_proj_kernel, dils=dils),
        grid=(b, s // tm),
        in_specs=[row(x), row(a), row(bm)] + [plane(o) for o in os_] + [plane(l) for l in ls_]
                 + [_layer_block(w, layer)],
        out_specs=row(x),
        out_shape=jax.ShapeDtypeStruct((b, s, d), F32),
        scratch_shapes=[pltpu.VMEM((tm, LANES), F32)] * (2 * len(dils)),
        compiler_params=_params("parallel", "parallel"),
        name="out_proj",
    )(x, a, bm, *os_, *ls_, w)


def _rope_table(seq, dim, fill):
    half = dim // 2
    inv = 1.0 / (ROPE_THETA ** (jnp.arange(0, dim, 2, dtype=F32) / dim))
    ang = jnp.arange(seq, dtype=F32)[:, None] * inv[None, :]
    cos, sin = jnp.cos(ang), jnp.sin(ang)
    rest = LANES - dim
    zeros_h = jnp.zeros((seq, half), F32)
    zeros_r = jnp.zeros((seq, rest), F32)
    cos_t = jnp.concatenate([cos, cos, jnp.full((seq, rest), fill, F32)], axis=1)
    s_hi = jnp.concatenate([-sin, zeros_h, zeros_r], axis=1)
    s_lo = jnp.concatenate([zeros_h, sin, zeros_r], axis=1)
    return jnp.stack([cos_t, s_hi, s_lo], axis=0)


_FW = FOX_HEADS * HEAD_DIM
_O_FL = 3 * _FW
_O_CQ = _O_FL + FOX_HEADS
_O_KR = _O_CQ + MLA_Q_RANK + MLA_KV_RANK
_O_DQ = _O_KR + MLA_ROPE


def _split_w_in_kernel(w_ref, fqk_ref, fv_ref, fl_ref, c_ref, dil_ref):
    w = w_ref[...]
    lane = lax.broadcasted_iota(jnp.int32, (w.shape[0], LANES), 1)
    fqk_ref[...] = w[:, :2 * _FW].astype(BF16)
    fv_ref[...] = w[:, 2 * _FW:_O_FL].astype(BF16)
    fl_ref[...] = jnp.where(lane < FOX_HEADS, w[:, _O_FL:_O_FL + LANES], 0.0).astype(BF16)
    c_ref[:, :_O_KR - _O_CQ] = w[:, _O_CQ:_O_KR].astype(BF16)
    c_ref[:, _O_KR - _O_CQ:] = jnp.where(lane < MLA_ROPE, w[:, _O_KR:_O_KR + LANES], 0.0).astype(BF16)
    dil_ref[...] = w[:, _O_DQ:].astype(BF16)


def _split_w_in(w_in, *, rows=256):
    depth, d, n_in = w_in.shape
    n_c = _O_KR - _O_CQ + LANES
    widths = (2 * _FW, _FW, LANES, n_c, n_in - _O_DQ)
    return pl.pallas_call(
        _split_w_in_kernel,
        grid=(depth, d // rows),
        in_specs=[pl.BlockSpec((None, rows, n_in), lambda l, i: (l, i, 0))],
        out_specs=[pl.BlockSpec((None, rows, n), lambda l, i: (l, i, 0)) for n in widths],
        out_shape=[jax.ShapeDtypeStruct((depth, d, n), BF16) for n in widths],
        compiler_params=_params("parallel", "parallel"),
        name="split_w_in",
    )(w_in)


def kernel(x, ffn1_norm, ffn1_w_gate, ffn1_w_up, ffn1_w_down, mix_norm, w_in, fox_forget_bias, mla_q_norm, mla_kv_norm, mla_w_uq, mla_w_ukv, w_out, ffn2_norm, ffn2_w_gate, ffn2_w_up, ffn2_w_down, final_norm):
    b, s, d = x.shape
    depth = w_in.shape[0]
    t = b * s

    assert all(window == BLOCK * dil and s % (BLOCK * dil) == 0 for window, dil in DIL_BRANCHES)
    tab_mla = _rope_table(s, MLA_ROPE, 1.0)
    tab_dil = _rope_table(s, PARTIAL_ROPE, 1.0)

    pad_last = lambda w, n: jnp.pad(w, [(0, 0)] * (w.ndim - 1) + [(0, n - w.shape[-1])])
    w_fqk_all, w_fv_all, w_fl_all, w_c_all, w_dil_all = _split_w_in(w_in)
    fbias_all = pad_last(fox_forget_bias.reshape(depth, 1, FOX_HEADS), LANES)
    wuq_all = mla_w_uq.reshape(depth, MLA_Q_RANK, MLA_HEADS, MLA_NOPE + MLA_ROPE)
    wuq_all = pad_last(wuq_all, 2 * LANES).reshape(depth, MLA_Q_RANK, MLA_HEADS * 2 * LANES).astype(BF16)
    wukv = mla_w_ukv.reshape(depth, MLA_KV_RANK, MLA_HEADS, MLA_NOPE + HEAD_DIM)
    w_k_all = wukv[..., :MLA_NOPE].reshape(depth, MLA_KV_RANK, MLA_HEADS * MLA_NOPE).astype(BF16)
    w_v_all = wukv[..., MLA_NOPE:].reshape(depth, MLA_KV_RANK, MLA_HEADS * HEAD_DIM).astype(BF16)
    w_out_all = w_out.astype(BF16)

    ffn1_w = [w.astype(BF16) for w in (ffn1_w_gate, ffn1_w_up, ffn1_w_down)]
    ffn2_w = [w.astype(BF16) for w in (ffn2_w_gate, ffn2_w_up, ffn2_w_down)]

    gq_all = mla_q_norm.reshape(depth, 1, MLA_Q_RANK)
    gkv_all = mla_kv_norm.reshape(depth, 1, MLA_KV_RANK)

    xf = x.reshape(t, d)
    for l in range(depth):
        last = l == depth - 1

        xf, hn = _ffn(xf, ffn1_norm[l], *ffn1_w, mix_norm[l], layer=l, emit_norm=True)

        hn = hn.reshape(b, s, d)
        fox_scale = HEAD_DIM ** -0.5
        fqk, fvt, logf = _fox_proj(hn, w_fqk_all, w_fv_all, w_fl_all, fbias_all, layer=l)
        logf = logf[:, :, :FOX_HEADS].transpose(0, 2, 1).reshape(b * FOX_HEADS, s)
        key_bias = _key_bias(logf, fox_scale).reshape(b, FOX_HEADS, 1, s)
        out_a = _attention(fqk, fqk, fvt, key_bias, dk=HEAD_DIM, q_off=0, k_off=FOX_HEADS, scale=fox_scale)

        q_b, k_b, vt_b = _mla_proj(hn, w_c_all, gq_all, gkv_all, wuq_all, w_k_all, w_v_all, tab_mla, layer=l)
        out_b = _attention(q_b, k_b, vt_b, None, dk=2 * LANES, q_off=0, k_off=0,
                           scale=(MLA_NOPE + MLA_ROPE) ** -0.5)

        dqkvs = _dil_proj(hn, w_dil_all, tab_dil, layer=l, dils=tuple(dil for _, dil in DIL_BRANCHES))
        outs, lses = [], []
        for dqkv in dqkvs:
            o, lse = _dil_branch(dqkv, scale=HEAD_DIM ** -0.5)
            outs.append(o)
            lses.append(lse)

        xf = _out_proj(xf.reshape(b, s, d), out_a, out_b, outs, lses, w_out_all, layer=l).reshape(t, d)

        xf = _ffn(xf, ffn2_norm[l], *ffn2_w, final_norm if last else None, layer=l, final_norm=last)
    return xf.reshape(b, s, d)
```

```python
import functools

import jax
import jax.numpy as jnp
from jax import lax
from jax.experimental import pallas as pl
from jax.experimental.pallas import tpu as pltpu

F32 = jnp.float32
BF16 = jnp.bfloat16

LANES = 128
HEAD_DIM = 128
BLOCK = 128
EPS = 1e-6
ROPE_THETA = 500000.0
PARTIAL_ROPE = HEAD_DIM // 4
FOX_HEADS = 4
MLA_HEADS = 8
MLA_Q_RANK = 512
MLA_KV_RANK = 512
MLA_NOPE = 128
MLA_ROPE = 64
DIL_HEADS = 4
DIL_BRANCHES = ((128, 1), (512, 4), (2048, 16))
NEG = -1e30
LOG2E = 1.4426950408889634
EXP2_HEADROOM = 100.0
VMEM_LIMIT = 56 * 1024 * 1024
FFN_TILE = 512


def _params(*sem):
    return pltpu.CompilerParams(dimension_semantics=sem, vmem_limit_bytes=VMEM_LIMIT)


def _layer_block(stacked, layer):
    return pl.BlockSpec((None,) + stacked.shape[1:], lambda *_: (layer,) + (0,) * (stacked.ndim - 1))


def _rms(x, g):
    return x * lax.rsqrt(jnp.mean(x * x, axis=-1, keepdims=True) + EPS) * g


def _apply_rope(x, cos, s_hi, s_lo, half):
    return x * cos + pltpu.roll(x, LANES - half, 1) * s_hi + pltpu.roll(x, half, 1) * s_lo


def _ffn_kernel(*refs, emit_norm, final_norm):
    x_hbm, g_ref, wg_ref, wu_ref, wd_ref = refs[:5]
    pos = 5
    g2_ref = None
    if emit_norm or final_norm:
        g2_ref = refs[pos]
        pos += 1
    o_ref = refs[pos]
    pos += 1
    hn_ref = None
    if emit_norm:
        hn_ref = refs[pos]
        pos += 1
    xn_ref, xbuf, sem = refs[pos:pos + 3]

    i, j = pl.program_id(0), pl.program_id(1)
    ni, nj = pl.num_programs(0), pl.num_programs(1)
    tm = xbuf.shape[0]

    def x_copy(tile):
        return pltpu.make_async_copy(x_hbm.at[pl.ds(pl.multiple_of(tile * tm, tm), tm), :], xbuf, sem)

    @pl.when(jnp.logical_and(i == 0, j == 0))
    def _():
        x_copy(0).start()

    def ff_tile():
        xn = xn_ref[...]
        gate = jnp.dot(xn, wg_ref[...], preferred_element_type=F32)
        up = jnp.dot(xn, wu_ref[...], preferred_element_type=F32)
        h = (gate * jax.nn.sigmoid(gate) * up * 0.5).astype(BF16)
        return jnp.dot(h, wd_ref[...], preferred_element_type=F32)

    @pl.when(j == 0)
    def _():
        x_copy(i).wait()
        xn_ref[...] = _rms(xbuf[...], g_ref[...]).astype(BF16)
        o_ref[...] = xbuf[...] + ff_tile()

    @pl.when(jnp.logical_and(j == 1, i + 1 < ni))
    def _():
        x_copy(i + 1).start()

    @pl.when(j > 0)
    def _():
        o_ref[...] += ff_tile()

    if emit_norm or final_norm:
        @pl.when(j == nj - 1)
        def _():
            y = _rms(o_ref[...], g2_ref[...])
            if emit_norm:
                hn_ref[...] = y.astype(BF16)
            else:
                o_ref[...] = y


def _ffn(x, g, wg, wu, wd, g2=None, *, layer, emit_norm=False, final_norm=False, tm=1024, tf=FFN_TILE):
    t, d = x.shape
    nj = wd.shape[1] // tf
    tm = min(tm, t)
    assert nj >= 2
    in_specs = [
        pl.BlockSpec(memory_space=pl.ANY),
        pl.BlockSpec((1, d), lambda i, j: (0, 0)),
        pl.BlockSpec((None, d, tf), lambda i, j: (layer, 0, j)),
        pl.BlockSpec((None, d, tf), lambda i, j: (layer, 0, j)),
        pl.BlockSpec((None, tf, d), lambda i, j: (layer, j, 0)),
    ]
    args = [x, g.reshape(1, d), wg, wu, wd]
    if emit_norm or final_norm:
        in_specs.append(pl.BlockSpec((1, d), lambda i, j: (0, 0)))
        args.append(g2.reshape(1, d))
    out_shape = [jax.ShapeDtypeStruct((t, d), F32)]
    out_specs = [pl.BlockSpec((tm, d), lambda i, j: (i, 0))]
    if emit_norm:
        out_shape.append(jax.ShapeDtypeStruct((t, d), BF16))
        out_specs.append(pl.BlockSpec((tm, d), lambda i, j: (i, 0)))
    res = pl.pallas_call(
        functools.partial(_ffn_kernel, emit_norm=emit_norm, final_norm=final_norm),
        grid=(t // tm, nj),
        in_specs=in_specs,
        out_specs=out_specs,
        out_shape=out_shape,
        scratch_shapes=[pltpu.VMEM((tm, d), BF16), pltpu.VMEM((tm, d), F32), pltpu.SemaphoreType.DMA],
        compiler_params=_params("arbitrary", "arbitrary"),
        name="ffn",
    )(*args)
    return res if emit_norm else res[0]


def _store_heads_transposed(vt_ref, v):
    for h in range(vt_ref.shape[0]):
        vt_ref[h] = v[:, h * HEAD_DIM:(h + 1) * HEAD_DIM].T.astype(BF16)


def _fox_proj_kernel(hn_ref, wqk_ref, wv_ref, wl_ref, b_ref, qk_ref, vt_ref, lf_ref):
    hn = hn_ref[...]
    v = jnp.dot(hn, wv_ref[...], preferred_element_type=F32)
    z = jnp.dot(hn, wl_ref[...], preferred_element_type=F32) + b_ref[...]
    qk = jnp.dot(hn, wqk_ref[...], preferred_element_type=F32)
    _store_heads_transposed(vt_ref, v)
    lf_ref[...] = jnp.minimum(z, 0.0) - jnp.log1p(jnp.exp(-jnp.abs(z)))
    qk_ref[...] = qk.astype(BF16)


def _fox_proj(hn, wqk, wv, wl, bias, *, layer, tm=512):
    b, s, d = hn.shape
    tm = min(tm, s)
    n = wqk.shape[2]
    heads = wv.shape[2] // HEAD_DIM
    full = lambda a: _layer_block(a, layer)
    return pl.pallas_call(
        _fox_proj_kernel,
        grid=(b, s // tm),
        in_specs=[pl.BlockSpec((None, tm, d), lambda bi, i: (bi, i, 0)), full(wqk), full(wv), full(wl), full(bias)],
        out_specs=[
            pl.BlockSpec((None, tm, n), lambda bi, i: (bi, i, 0)),
            pl.BlockSpec((None, heads, HEAD_DIM, tm), lambda bi, i: (bi, 0, 0, i)),
            pl.BlockSpec((None, tm, LANES), lambda bi, i: (bi, i, 0)),
        ],
        out_shape=[
            jax.ShapeDtypeStruct((b, s, n), BF16),
            jax.ShapeDtypeStruct((b, heads, HEAD_DIM, s), BF16),
            jax.ShapeDtypeStruct((b, s, LANES), F32),
        ],
        compiler_params=_params("parallel", "parallel"),
        name="fox_proj",
    )(hn, wqk, wv, wl, bias)


def _mla_proj_kernel(hn_ref, wc_ref, gq_ref, gkv_ref, wuq_ref, wk_ref, wv_ref, tab_ref,
                     q_ref, k_ref, vt_ref):
    hn = hn_ref[...]
    o_kv, o_kr = MLA_Q_RANK, MLA_Q_RANK + MLA_KV_RANK
    cq = jnp.dot(hn, wc_ref[:, :o_kv], preferred_element_type=F32)
    ckv = jnp.dot(hn, wc_ref[:, o_kv:o_kr], preferred_element_type=F32)
    cqn = _rms(cq, gq_ref[...]).astype(BF16)
    kr_raw = jnp.dot(hn, wc_ref[:, o_kr:], preferred_element_type=F32)
    q = jnp.dot(cqn, wuq_ref[...], preferred_element_type=F32)
    ckvn = _rms(ckv, gkv_ref[...]).astype(BF16)
    cos, s_hi, s_lo = tab_ref[0], tab_ref[1], tab_ref[2]
    half = MLA_ROPE // 2
    kr = _apply_rope(kr_raw, cos, s_hi, s_lo, half).astype(BF16)
    v = jnp.dot(ckvn, wv_ref[...], preferred_element_type=F32)
    kn = jnp.dot(ckvn, wk_ref[...], preferred_element_type=F32)
    _store_heads_transposed(vt_ref, v)
    for h in range(MLA_HEADS):
        lo = 2 * h * LANES
        q_ref[:, lo:lo + LANES] = q[:, lo:lo + LANES].astype(BF16)
        q_ref[:, lo + LANES:lo + 2 * LANES] = _apply_rope(
            q[:, lo + LANES:lo + 2 * LANES], cos, s_hi, s_lo, half).astype(BF16)
        k_ref[:, lo:lo + LANES] = kn[:, h * LANES:(h + 1) * LANES].astype(BF16)
        k_ref[:, lo + LANES:lo + 2 * LANES] = kr


def _mla_proj(hn, wc, gq, gkv, wuq, wk, wv, tab, *, layer, tm=512):
    b, s, d = hn.shape
    tm = min(tm, s)
    nq = wuq.shape[2]
    heads = wv.shape[2] // HEAD_DIM
    full = lambda a: _layer_block(a, layer)
    return pl.pallas_call(
        _mla_proj_kernel,
        grid=(b, s // tm),
        in_specs=[
            pl.BlockSpec((None, tm, d), lambda bi, i: (bi, i, 0)),
            full(wc), full(gq), full(gkv), full(wuq), full(wk), full(wv),
            pl.BlockSpec((3, tm, LANES), lambda bi, i: (0, i, 0)),
        ],
        out_specs=[
            pl.BlockSpec((None, tm, nq), lambda bi, i: (bi, i, 0)),
            pl.BlockSpec((None, tm, nq), lambda bi, i: (bi, i, 0)),
            pl.BlockSpec((None, heads, HEAD_DIM, tm), lambda bi, i: (bi, 0, 0, i)),
        ],
        out_shape=[
            jax.ShapeDtypeStruct((b, s, nq), BF16),
            jax.ShapeDtypeStruct((b, s, nq), BF16),
            jax.ShapeDtypeStruct((b, heads, HEAD_DIM, s), BF16),
        ],
        compiler_params=_params("parallel", "parallel"),
        name="mla_proj",
    )(hn, wc, gq, gkv, wuq, wk, wv, tab)


def _dil_proj_kernel(hn_ref, w_ref, tab_ref, *rest, dils):
    out_refs, scr = rest[:len(dils)], rest[len(dils):]
    hn = hn_ref[...]
    cos, s_hi, s_lo = tab_ref[0], tab_ref[1], tab_ref[2]
    half = PARTIAL_ROPE // 2
    n_rot = 2 * DIL_HEADS
    tm = hn.shape[0]
    groups = [(0, DIL_HEADS), (DIL_HEADS, n_rot), (n_rot, n_rot + DIL_HEADS // 2),
              (n_rot + DIL_HEADS // 2, n_rot + DIL_HEADS)]

    def project(lo, hi):
        return jnp.dot(hn, w_ref[:, lo * LANES:hi * LANES], preferred_element_type=F32)

    def emit(lo, hi, y):
        for j in range(lo, hi):
            v = y[:, (j - lo) * LANES:(j - lo + 1) * LANES]
            scr[j][...] = _apply_rope(v, cos, s_hi, s_lo, half) if j < n_rot else v
            for o_ref, dil in zip(out_refs, dils):
                rows = tm // dil
                for r in range(dil):
                    o_ref[r, :, j * LANES:(j + 1) * LANES] = scr[j][pl.ds(r, rows, stride=dil), :].astype(BF16)

    ys = [project(*groups[0])]
    for g, (lo, hi) in enumerate(groups):
        if g + 1 < len(groups):
            ys.append(project(*groups[g + 1]))
        emit(lo, hi, ys[g])


def _dil_proj(hn, w, tab, *, layer, dils, tm=512):
    b, s, d = hn.shape
    tm = min(tm, s)
    n = w.shape[2]
    return pl.pallas_call(
        functools.partial(_dil_proj_kernel, dils=dils),
        grid=(b, s // tm),
        in_specs=[
            pl.BlockSpec((None, tm, d), lambda bi, i: (bi, i, 0)),
            _layer_block(w, layer),
            pl.BlockSpec((3, tm, LANES), lambda bi, i: (0, i, 0)),
        ],
        out_specs=[pl.BlockSpec((None, dil, tm // dil, n), lambda bi, i: (bi, 0, i, 0)) for dil in dils],
        out_shape=[jax.ShapeDtypeStruct((b, dil, s // dil, n), BF16) for dil in dils],
        scratch_shapes=[pltpu.VMEM((tm, LANES), F32)] * (n // LANES),
        compiler_params=_params("parallel", "parallel"),
        name="dil_proj",
    )(hn, w, tab)


def _cumsum_kernel(x_ref, o_ref, *, out_scale):
    x = x_ref[...]
    n = x.shape[-1]
    lane = lax.broadcasted_iota(jnp.int32, x.shape, 1)
    k = 1
    while k < n:
        x = x + jnp.where(lane >= k, pltpu.roll(x, k, 1), 0.0)
        k *= 2
    o_ref[...] = x * out_scale


def _key_bias(x, scale):
    return pl.pallas_call(
        functools.partial(_cumsum_kernel, out_scale=-1.0 / scale),
        out_shape=jax.ShapeDtypeStruct(x.shape, F32),
        compiler_params=pltpu.CompilerParams(vmem_limit_bytes=VMEM_LIMIT),
        name="forget_cumsum",
    )(x)


def _attn_kernel(*refs, tq, cq, tk, ahead, scale, has_bias):
    if has_bias:
        q_ref, k_ref, vt_ref, b_ref, o_ref, m_ref, acc_ref, worst_ref, brep_ref = refs
    else:
        q_ref, k_ref, vt_ref, o_ref, m_ref, acc_ref, worst_ref = refs
    qi = pl.program_id(2)
    n_chain = tq // cq
    dv = vt_ref.shape[0]
    c2 = scale * LOG2E

    if has_bias:
        @pl.when(qi == 0)
        def _():
            for j in range(brep_ref.shape[0] // tq):
                rows = jnp.broadcast_to(b_ref[:, j * tq:(j + 1) * tq], (LANES, tq))
                brep_ref[j * tq:(j + 1) * tq, :] = rows.T

    def scores(c, ks, size):
        q = q_ref[c * cq:(c + 1) * cq, :]
        k = k_ref[pl.ds(ks, size), :]
        return lax.dot_general(k, q, (((1,), (1,)), ((), ())), preferred_element_type=F32)

    def update(c, st, ks, size, exact):
        if has_bias:
            st = st + jnp.tile(brep_ref[pl.ds(ks, size), :], (1, cq // LANES))
        m_prev = m_ref[c]
        top = jnp.max(st, axis=0, keepdims=True)
        m_new = jnp.maximum(m_prev, top)
        alpha = jnp.exp2((m_prev - m_new) * c2)
        v_ext = jnp.concatenate([vt_ref[:, pl.ds(ks, size)], jnp.ones((dv, size), BF16)], axis=0)
        if exact:
            pt = jnp.exp2((st - m_new) * c2).astype(BF16)
            acc_ref[c] = alpha * acc_ref[c] + jnp.dot(v_ext, pt, preferred_element_type=F32)
        else:
            pt = jnp.exp2((st - m_prev) * c2).astype(BF16)
            acc_ref[c] = alpha * (acc_ref[c] + jnp.dot(v_ext, pt, preferred_element_type=F32))
            worst_ref[c] = jnp.maximum(worst_ref[c], (top - m_prev) * c2)
        m_ref[c] = m_new

    def run(work, ahead):
        pending = [scores(*w[:3]) for w in work[:ahead]]
        for i, (c, ks, size, exact) in enumerate(work):
            st = pending.pop(0)
            if i + ahead < len(work):
                pending.append(scores(*work[i + ahead][:3]))
            update(c, st, ks, size, exact)

    def diagonal_tile():
        work = [(c, pl.multiple_of(qi * tq + j * cq, cq), c == j)
                for j in range(n_chain) for c in range(j, n_chain)]
        look = 4
        pending = [scores(c, ks, cq) for c, ks, _ in work[:look]]
        partial = [[] for _ in range(n_chain)]
        for i, (c, ks, on_diagonal) in enumerate(work):
            st = pending.pop(0)
            if i + look < len(work):
                pending.append(scores(work[i + look][0], work[i + look][1], cq))
            if has_bias:
                st = st + jnp.tile(brep_ref[pl.ds(ks, cq), :], (1, cq // LANES))
            if on_diagonal:
                key = lax.broadcasted_iota(jnp.int32, st.shape, 0)
                qry = lax.broadcasted_iota(jnp.int32, st.shape, 1)
                st = jnp.where(key <= qry, st, NEG)
            top = jnp.max(st, axis=0, keepdims=True)
            pt = jnp.exp2((st - top) * c2).astype(BF16)
            v_ext = jnp.concatenate([vt_ref[:, pl.ds(ks, cq)], jnp.ones((dv, cq), BF16)], axis=0)
            partial[c].append((top, jnp.dot(v_ext, pt, preferred_element_type=F32)))
        for c in range(n_chain):
            m_all = functools.reduce(jnp.maximum, [m for m, _ in partial[c]])
            acc = None
            for m, pv in partial[c]:
                term = jnp.exp2((m - m_all) * c2) * pv
                acc = term if acc is None else acc + term
            m_ref[c] = m_all
            acc_ref[c] = acc

    def attend(exact):
        diagonal_tile()

        def body(j, carry):
            run([(c, pl.multiple_of(j * tq + part * tk, tk), tk, exact)
                 for part in range(tq // tk) for c in range(n_chain)], ahead)
            return carry

        lax.fori_loop(0, qi, body, 0)

    worst_ref[...] = jnp.zeros(worst_ref.shape, F32)
    attend(exact=False)

    @pl.when(jnp.max(worst_ref[...]) > EXP2_HEADROOM)
    def _():
        attend(exact=True)

    for c in range(n_chain):
        acc = acc_ref[c]
        o_ref[c * cq:(c + 1) * cq, :] = (acc[:dv] / acc[dv:]).T.astype(o_ref.dtype)


def _attention(q_arr, k_arr, vt_arr, bias, *, dk, q_off, k_off, scale, tq=2048, cq=256, tk=1024, ahead=2):
    b, s, _ = q_arr.shape
    heads, dv = vt_arr.shape[1], vt_arr.shape[2]
    tq = min(tq, s)
    cq = min(cq, tq)
    n_chain = tq // cq
    in_specs = [
        pl.BlockSpec((None, tq, dk), lambda bi, h, qi: (bi, qi, q_off + h)),
        pl.BlockSpec((None, s, dk), lambda bi, h, qi: (bi, 0, k_off + h)),
        pl.BlockSpec((None, None, dv, s), lambda bi, h, qi: (bi, h, 0, 0)),
    ]
    args = [q_arr, k_arr, vt_arr]
    stat = pltpu.VMEM((n_chain, 1, cq), F32)
    scratch = [stat, pltpu.VMEM((n_chain, 2 * dv, cq), F32), stat]
    if bias is not None:
        in_specs.append(pl.BlockSpec((None, None, 1, s), lambda bi, h, qi: (bi, h, 0, 0)))
        args.append(bias)
        scratch.append(pltpu.VMEM((s, LANES), F32))
    return pl.pallas_call(
        functools.partial(_attn_kernel, tq=tq, cq=cq, tk=min(tk, tq), ahead=ahead, scale=scale,
                          has_bias=bias is not None),
        grid=(b, heads, s // tq),
        in_specs=in_specs,
        out_specs=pl.BlockSpec((None, tq, dv), lambda bi, h, qi: (bi, qi, h)),
        out_shape=jax.ShapeDtypeStruct((b, s, heads * dv), BF16),
        scratch_shapes=scratch,
        compiler_params=_params("arbitrary", "arbitrary", "arbitrary"),
        name="attn_bias" if bias is not None else "attn",
    )(*args)


def _dil_kernel(q_ref, k_ref, v_ref, kp_ref, vp_ref, o_ref, lse_ref, kbuf, vbuf, *, scale, group):
    planes, tl, _ = q_ref.shape
    nb = tl // BLOCK
    c2 = scale * LOG2E
    kbuf[:, :BLOCK, :] = kp_ref[...]
    kbuf[:, BLOCK:, :] = k_ref[...]
    vbuf[:, :BLOCK, :] = vp_ref[...]
    vbuf[:, BLOCK:, :] = v_ref[...]
    row = lax.broadcasted_iota(jnp.int32, (BLOCK, 2 * BLOCK), 0)
    col = lax.broadcasted_iota(jnp.int32, (BLOCK, 2 * BLOCK), 1)
    band = (col >= row) & (col <= row + BLOCK)
    lane = lax.broadcasted_iota(jnp.int32, (BLOCK, LANES), 1)
    ones = jnp.ones((2 * BLOCK, HEAD_DIM), BF16)
    heads = [slice(h * HEAD_DIM, (h + 1) * HEAD_DIM) for h in range(DIL_HEADS)]
    first_chunk = pl.program_id(2) == 0

    def locate(g, u):
        if nb % group == 0:
            gpp = nb // group
            p = g // gpp if planes > 1 else 0
            first = (g % gpp == 0) if u == 0 else None
            return p, (g % gpp) * group + u, first
        per = group // nb
        return g * per + u // nb, u % nb, (True if u % nb == 0 else None)

    def body(g, carry):
        work, masks = [], []
        for u in range(group):
            p, n, first = locate(g, u)
            start = n * BLOCK if isinstance(n, int) else pl.multiple_of(n * BLOCK, BLOCK)
            if first is None:
                masks.append(band)
            else:
                cond = first_chunk if first is True else jnp.logical_and(first_chunk, first)
                masks.append(band & (col >= jnp.where(cond, BLOCK, 0)))
            work += [(u, p, start, h) for h in range(DIL_HEADS)]

        def scores(u, p, start, h):
            q = q_ref[p, pl.ds(start, BLOCK), heads[h]]
            kc = kbuf[p, pl.ds(start, 2 * BLOCK), heads[h]]
            return lax.dot_general(q, kc, (((1,), (1,)), ((), ())), preferred_element_type=F32)

        ahead = 2 * DIL_HEADS
        pending = [scores(*w) for w in work[:ahead]]
        tiles = {}
        for i, (u, p, start, h) in enumerate(work):
            s = pending.pop(0)
            if i + ahead < len(work):
                pending.append(scores(*work[i + ahead]))
            s = jnp.where(masks[u], s, NEG)
            m = jnp.max(s, axis=-1, keepdims=True)
            e = jnp.exp2((s - m) * c2).astype(BF16)
            v_ext = jnp.concatenate([vbuf[p, pl.ds(start, 2 * BLOCK), heads[h]], ones], axis=1)
            pv = jnp.dot(e, v_ext, preferred_element_type=F32)
            denom = pv[:, HEAD_DIM:]
            o_ref[p, pl.ds(start, BLOCK), heads[h]] = pv[:, :HEAD_DIM] / denom
            tile = tiles.get(u, jnp.zeros((BLOCK, LANES), F32))
            tiles[u] = jnp.where(lane == h, m * scale + jnp.log(denom), tile)
            if h == DIL_HEADS - 1:
                lse_ref[p, pl.ds(start, BLOCK), :] = tiles.pop(u)
        return carry

    n_groups = planes * nb // group
    if n_groups == 1:
        body(0, 0)
    else:
        lax.fori_loop(0, n_groups, body, 0)


def _dil_branch(qkv, *, scale, rows=1024, group=4):
    b, dil, sub, w3 = qkv.shape
    w = w3 // 3
    tl = min(rows, sub)
    planes = min(rows // tl, dil)
    bpc = tl // BLOCK
    group = min(group, planes * bpc)
    assert (bpc % group == 0 or group % bpc == 0) and (planes * bpc) % group == 0
    own = lambda j: pl.BlockSpec((None, planes, tl, w), lambda bi, r, c: (bi, r, c, j))
    prev = lambda j: pl.BlockSpec((None, planes, BLOCK, w),
                                  lambda bi, r, c: (bi, r, jnp.maximum(c * bpc - 1, 0), j))
    return pl.pallas_call(
        functools.partial(_dil_kernel, scale=scale, group=group),
        grid=(b, dil // planes, sub // tl),
        in_specs=[own(0), own(1), own(2), prev(1), prev(2)],
        out_specs=[
            pl.BlockSpec((None, planes, tl, w), lambda bi, r, c: (bi, r, c, 0)),
            pl.BlockSpec((None, planes, tl, LANES), lambda bi, r, c: (bi, r, c, 0)),
        ],
        out_shape=[
            jax.ShapeDtypeStruct((b, dil, sub, w), F32),
            jax.ShapeDtypeStruct((b, dil, sub, LANES), F32),
        ],
        scratch_shapes=[pltpu.VMEM((planes, tl + BLOCK, w), BF16)] * 2,
        compiler_params=_params("parallel", "parallel", "parallel"),
        name=f"dilated_{dil}",
    )(qkv, qkv, qkv, qkv, qkv)


def _out_proj_kernel(x_ref, a_ref, b_ref, *rest, dils):
    nbr = len(dils)
    o_refs, l_refs = rest[:nbr], rest[nbr:2 * nbr]
    w_ref, y_ref = rest[2 * nbr:2 * nbr + 2]
    scr = rest[2 * nbr + 2:]
    tm = x_ref.shape[0]

    def natural(ref, dil, slot, j):
        sl = slice(j * LANES, (j + 1) * LANES)
        if dil == 1:
            return ref[0, :, sl]
        rows = tm // dil
        for r in range(dil):
            scr[slot][pl.ds(r, rows, stride=dil), :] = ref[r, :, sl]
        return scr[slot][...]

    ab = jnp.concatenate([a_ref[...], b_ref[...]], axis=-1)
    n_ab = ab.shape[1]
    y = x_ref[...] + jnp.dot(ab, w_ref[:n_ab, :], preferred_element_type=F32)

    lses = [natural(l_refs[i], dils[i], 2 * i, 0) for i in range(nbr)]
    m = functools.reduce(jnp.maximum, lses)
    es = [jnp.exp(l - m) for l in lses]
    tot = functools.reduce(lambda u, v: u + v, es)
    wts = [e / tot for e in es]
    parts = []
    for h in range(DIL_HEADS):
        c = None
        for i in range(nbr):
            term = wts[i][:, h:h + 1] * natural(o_refs[i], dils[i], 2 * i + 1, h)
            c = term if c is None else c + term
        parts.append(c.astype(BF16))
    mixed = jnp.concatenate(parts, axis=-1)
    y_ref[...] = y + jnp.dot(mixed, w_ref[n_ab:, :], preferred_element_type=F32)


def _out_proj(x, a, bm, os_, ls_, w, *, layer, tm=512):
    b, s, d = x.shape
    tm = min(tm, s)
    dils = tuple(o.shape[1] for o in os_)
    row = lambda arr: pl.BlockSpec((None, tm, arr.shape[2]), lambda bi, i: (bi, i, 0))
    plane = lambda arr: pl.BlockSpec((None, arr.shape[1], tm // arr.shape[1], arr.shape[3]),
                                     lambda bi, i: (bi, 0, i, 0))
    return pl.pallas_call(
        functools.partial(_out_proj_kernel, dils=dils),
        grid=(b, s // tm),
        in_specs=[row(x), row(a), row(bm)] + [plane(o) for o in os_] + [plane(l) for l in ls_]
                 + [_layer_block(w, layer)],
        out_specs=row(x),
        out_shape=jax.ShapeDtypeStruct((b, s, d), F32),
        scratch_shapes=[pltpu.VMEM((tm, LANES), F32)] * (2 * len(dils)),
        compiler_params=_params("parallel", "parallel"),
        name="out_proj",
    )(x, a, bm, *os_, *ls_, w)


def _rope_table(seq, dim, fill):
    half = dim // 2
    inv = 1.0 / (ROPE_THETA ** (jnp.arange(0, dim, 2, dtype=F32) / dim))
    ang = jnp.arange(seq, dtype=F32)[:, None] * inv[None, :]
    cos, sin = jnp.cos(ang), jnp.sin(ang)
    rest = LANES - dim
    zeros_h = jnp.zeros((seq, half), F32)
    zeros_r = jnp.zeros((seq, rest), F32)
    cos_t = jnp.concatenate([cos, cos, jnp.full((seq, rest), fill, F32)], axis=1)
    s_hi = jnp.concatenate([-sin, zeros_h, zeros_r], axis=1)
    s_lo = jnp.concatenate([zeros_h, sin, zeros_r], axis=1)
    return jnp.stack([cos_t, s_hi, s_lo], axis=0)


_FW = FOX_HEADS * HEAD_DIM
_O_FL = 3 * _FW
_O_CQ = _O_FL + FOX_HEADS
_O_KR = _O_CQ + MLA_Q_RANK + MLA_KV_RANK
_O_DQ = _O_KR + MLA_ROPE


def _split_w_in_kernel(w_ref, fqk_ref, fv_ref, fl_ref, c_ref, dil_ref):
    w = w_ref[...]
    lane = lax.broadcasted_iota(jnp.int32, (w.shape[0], LANES), 1)
    fqk_ref[...] = w[:, :2 * _FW].astype(BF16)
    fv_ref[...] = w[:, 2 * _FW:_O_FL].astype(BF16)
    fl_ref[...] = jnp.where(lane < FOX_HEADS, w[:, _O_FL:_O_FL + LANES], 0.0).astype(BF16)
    c_ref[:, :_O_KR - _O_CQ] = w[:, _O_CQ:_O_KR].astype(BF16)
    c_ref[:, _O_KR - _O_CQ:] = jnp.where(lane < MLA_ROPE, w[:, _O_KR:_O_KR + LANES], 0.0).astype(BF16)
    dil_ref[...] = w[:, _O_DQ:].astype(BF16)


def _split_w_in(w_in, *, rows=256):
    depth, d, n_in = w_in.shape
    n_c = _O_KR - _O_CQ + LANES
    widths = (2 * _FW, _FW, LANES, n_c, n_in - _O_DQ)
    return pl.pallas_call(
        _split_w_in_kernel,
        grid=(depth, d // rows),
        in_specs=[pl.BlockSpec((None, rows, n_in), lambda l, i: (l, i, 0))],
        out_specs=[pl.BlockSpec((None, rows, n), lambda l, i: (l, i, 0)) for n in widths],
        out_shape=[jax.ShapeDtypeStruct((depth, d, n), BF16) for n in widths],
        compiler_params=_params("parallel", "parallel"),
        name="split_w_in",
    )(w_in)


def kernel(x, ffn1_norm, ffn1_w_gate, ffn1_w_up, ffn1_w_down, mix_norm, w_in, fox_forget_bias, mla_q_norm, mla_kv_norm, mla_w_uq, mla_w_ukv, w_out, ffn2_norm, ffn2_w_gate, ffn2_w_up, ffn2_w_down, final_norm):
    b, s, d = x.shape
    depth = w_in.shape[0]
    t = b * s

    assert all(window == BLOCK * dil and s % (BLOCK * dil) == 0 for window, dil in DIL_BRANCHES)
    tab_mla = _rope_table(s, MLA_ROPE, 1.0)
    tab_dil = _rope_table(s, PARTIAL_ROPE, 1.0)

    pad_last = lambda w, n: jnp.pad(w, [(0, 0)] * (w.ndim - 1) + [(0, n - w.shape[-1])])
    w_fqk_all, w_fv_all, w_fl_all, w_c_all, w_dil_all = _split_w_in(w_in)
    fbias_all = pad_last(fox_forget_bias.reshape(depth, 1, FOX_HEADS), LANES)
    wuq_all = mla_w_uq.reshape(depth, MLA_Q_RANK, MLA_HEADS, MLA_NOPE + MLA_ROPE)
    wuq_all = pad_last(wuq_all, 2 * LANES).reshape(depth, MLA_Q_RANK, MLA_HEADS * 2 * LANES).astype(BF16)
    wukv = mla_w_ukv.reshape(depth, MLA_KV_RANK, MLA_HEADS, MLA_NOPE + HEAD_DIM)
    w_k_all = wukv[..., :MLA_NOPE].reshape(depth, MLA_KV_RANK, MLA_HEADS * MLA_NOPE).astype(BF16)
    w_v_all = wukv[..., MLA_NOPE:].reshape(depth, MLA_KV_RANK, MLA_HEADS * HEAD_DIM).astype(BF16)
    w_out_all = w_out.astype(BF16)

    ffn1_w = [w.astype(BF16) for w in (ffn1_w_gate, ffn1_w_up, ffn1_w_down)]
    ffn2_w = [w.astype(BF16) for w in (ffn2_w_gate, ffn2_w_up, ffn2_w_down)]

    gq_all = mla_q_norm.reshape(depth, 1, MLA_Q_RANK)
    gkv_all = mla_kv_norm.reshape(depth, 1, MLA_KV_RANK)

    xf = x.reshape(t, d)
    for l in range(depth):
        last = l == depth - 1

        xf, hn = _ffn(xf, ffn1_norm[l], *ffn1_w, mix_norm[l], layer=l, emit_norm=True)

        hn = hn.reshape(b, s, d)
        fox_scale = HEAD_DIM ** -0.5
        fqk, fvt, logf = _fox_proj(hn, w_fqk_all, w_fv_all, w_fl_all, fbias_all, layer=l)
        logf = logf[:, :, :FOX_HEADS].transpose(0, 2, 1).reshape(b * FOX_HEADS, s)
        key_bias = _key_bias(logf, fox_scale).reshape(b, FOX_HEADS, 1, s)
        out_a = _attention(fqk, fqk, fvt, key_bias, dk=HEAD_DIM, q_off=0, k_off=FOX_HEADS, scale=fox_scale)

        q_b, k_b, vt_b = _mla_proj(hn, w_c_all, gq_all, gkv_all, wuq_all, w_k_all, w_v_all, tab_mla, layer=l)
        out_b = _attention(q_b, k_b, vt_b, None, dk=2 * LANES, q_off=0, k_off=0,
                           scale=(MLA_NOPE + MLA_ROPE) ** -0.5)

        dqkvs = _dil_proj(hn, w_dil_all, tab_dil, layer=l, dils=tuple(dil for _, dil in DIL_BRANCHES))
        outs, lses = [], []
        for dqkv in dqkvs:
            o, lse = _dil_branch(dqkv, scale=HEAD_DIM ** -0.5)
            outs.append(o)
            lses.append(lse)

        xf = _out_proj(xf.reshape(b, s, d), out_a, out_b, outs, lses, w_out_all, layer=l).reshape(t, d)

        xf = _ffn(xf, ffn2_norm[l], *ffn2_w, final_norm if last else None, layer=l, final_norm=last)
    return xf.reshape(b, s, d)
```

```python
import functools

import jax
import jax.numpy as jnp
from jax import lax
from jax.experimental import pallas as pl
from jax.experimental.pallas import tpu as pltpu

F32 = jnp.float32
BF16 = jnp.bfloat16

LANES = 128
HEAD_DIM = 128
BLOCK = 128
EPS = 1e-6
ROPE_THETA = 500000.0
PARTIAL_ROPE = HEAD_DIM // 4
FOX_HEADS = 4
MLA_HEADS = 8
MLA_Q_RANK = 512
MLA_KV_RANK = 512
MLA_NOPE = 128
MLA_ROPE = 64
DIL_HEADS = 4
DIL_BRANCHES = ((128, 1), (512, 4), (2048, 16))
NEG = -1e30
LOG2E = 1.4426950408889634
EXP2_HEADROOM = 100.0
VMEM_LIMIT = 56 * 1024 * 1024
FFN_TILE = 512


def _params(*sem):
    return pltpu.CompilerParams(dimension_semantics=sem, vmem_limit_bytes=VMEM_LIMIT)


def _layer_block(stacked, layer):
    return pl.BlockSpec((None,) + stacked.shape[1:], lambda *_: (layer,) + (0,) * (stacked.ndim - 1))


def _rms(x, g):
    return x * lax.rsqrt(jnp.mean(x * x, axis=-1, keepdims=True) + EPS) * g


def _apply_rope(x, cos, s_hi, s_lo, half):
    return x * cos + pltpu.roll(x, LANES - half, 1) * s_hi + pltpu.roll(x, half, 1) * s_lo


def _ffn_kernel(*refs, emit_norm, final_norm):
    x_hbm, g_ref, wg_ref, wu_ref, wd_ref = refs[:5]
    pos = 5
    g2_ref = None
    if emit_norm or final_norm:
        g2_ref = refs[pos]
        pos += 1
    o_ref = refs[pos]
    pos += 1
    hn_ref = None
    if emit_norm:
        hn_ref = refs[pos]
        pos += 1
    xn_ref, xbuf, sem = refs[pos:pos + 3]

    i, j = pl.program_id(0), pl.program_id(1)
    ni, nj = pl.num_programs(0), pl.num_programs(1)
    tm = xbuf.shape[0]

    def x_copy(tile):
        return pltpu.make_async_copy(x_hbm.at[pl.ds(pl.multiple_of(tile * tm, tm), tm), :], xbuf, sem)

    @pl.when(jnp.logical_and(i == 0, j == 0))
    def _():
        x_copy(0).start()

    def ff_tile():
        xn = xn_ref[...]
        gate = jnp.dot(xn, wg_ref[...], preferred_element_type=F32)
        up = jnp.dot(xn, wu_ref[...], preferred_element_type=F32)
        h = (gate * jax.nn.sigmoid(gate) * up * 0.5).astype(BF16)
        return jnp.dot(h, wd_ref[...], preferred_element_type=F32)

    @pl.when(j == 0)
    def _():
        x_copy(i).wait()
        xn_ref[...] = _rms(xbuf[...], g_ref[...]).astype(BF16)
        o_ref[...] = xbuf[...] + ff_tile()

    @pl.when(jnp.logical_and(j == 1, i + 1 < ni))
    def _():
        x_copy(i + 1).start()

    @pl.when(j > 0)
    def _():
        o_ref[...] += ff_tile()

    if emit_norm or final_norm:
        @pl.when(j == nj - 1)
        def _():
            y = _rms(o_ref[...], g2_ref[...])
            if emit_norm:
                hn_ref[...] = y.astype(BF16)
            else:
                o_ref[...] = y


def _ffn(x, g, wg, wu, wd, g2=None, *, layer, emit_norm=False, final_norm=False, tm=1024, tf=FFN_TILE):
    t, d = x.shape
    nj = wd.shape[1] // tf
    tm = min(tm, t)
    assert nj >= 2
    in_specs = [
        pl.BlockSpec(memory_space=pl.ANY),
        pl.BlockSpec((1, d), lambda i, j: (0, 0)),
        pl.BlockSpec((None, d, tf), lambda i, j: (layer, 0, j)),
        pl.BlockSpec((None, d, tf), lambda i, j: (layer, 0, j)),
        pl.BlockSpec((None, tf, d), lambda i, j: (layer, j, 0)),
    ]
    args = [x, g.reshape(1, d), wg, wu, wd]
    if emit_norm or final_norm:
        in_specs.append(pl.BlockSpec((1, d), lambda i, j: (0, 0)))
        args.append(g2.reshape(1, d))
    out_shape = [jax.ShapeDtypeStruct((t, d), F32)]
    out_specs = [pl.BlockSpec((tm, d), lambda i, j: (i, 0))]
    if emit_norm:
        out_shape.append(jax.ShapeDtypeStruct((t, d), BF16))
        out_specs.append(pl.BlockSpec((tm, d), lambda i, j: (i, 0)))
    res = pl.pallas_call(
        functools.partial(_ffn_kernel, emit_norm=emit_norm, final_norm=final_norm),
        grid=(t // tm, nj),
        in_specs=in_specs,
        out_specs=out_specs,
        out_shape=out_shape,
        scratch_shapes=[pltpu.VMEM((tm, d), BF16), pltpu.VMEM((tm, d), F32), pltpu.SemaphoreType.DMA],
        compiler_params=_params("arbitrary", "arbitrary"),
        name="ffn",
    )(*args)
    return res if emit_norm else res[0]


def _store_heads_transposed(vt_ref, v):
    for h in range(vt_ref.shape[0]):
        vt_ref[h] = v[:, h * HEAD_DIM:(h + 1) * HEAD_DIM].T.astype(BF16)


def _fox_proj_kernel(hn_ref, wqk_ref, wv_ref, wl_ref, b_ref, qk_ref, vt_ref, lf_ref):
    hn = hn_ref[...]
    v = jnp.dot(hn, wv_ref[...], preferred_element_type=F32)
    z = jnp.dot(hn, wl_ref[...], preferred_element_type=F32) + b_ref[...]
    qk = jnp.dot(hn, wqk_ref[...], preferred_element_type=F32)
    _store_heads_transposed(vt_ref, v)
    lf_ref[...] = jnp.minimum(z, 0.0) - jnp.log1p(jnp.exp(-jnp.abs(z)))
    qk_ref[...] = qk.astype(BF16)


def _fox_proj(hn, wqk, wv, wl, bias, *, layer, tm=512):
    b, s, d = hn.shape
    tm = min(tm, s)
    n = wqk.shape[2]
    heads = wv.shape[2] // HEAD_DIM
    full = lambda a: _layer_block(a, layer)
    return pl.pallas_call(
        _fox_proj_kernel,
        grid=(b, s // tm),
        in_specs=[pl.BlockSpec((None, tm, d), lambda bi, i: (bi, i, 0)), full(wqk), full(wv), full(wl), full(bias)],
        out_specs=[
            pl.BlockSpec((None, tm, n), lambda bi, i: (bi, i, 0)),
            pl.BlockSpec((None, heads, HEAD_DIM, tm), lambda bi, i: (bi, 0, 0, i)),
            pl.BlockSpec((None, tm, LANES), lambda bi, i: (bi, i, 0)),
        ],
        out_shape=[
            jax.ShapeDtypeStruct((b, s, n), BF16),
            jax.ShapeDtypeStruct((b, heads, HEAD_DIM, s), BF16),
            jax.ShapeDtypeStruct((b, s, LANES), F32),
        ],
        compiler_params=_params("parallel", "parallel"),
        name="fox_proj",
    )(hn, wqk, wv, wl, bias)


def _mla_proj_kernel(hn_ref, wc_ref, gq_ref, gkv_ref, wuq_ref, wk_ref, wv_ref, tab_ref,
                     q_ref, k_ref, vt_ref):
    hn = hn_ref[...]
    o_kv, o_kr = MLA_Q_RANK, MLA_Q_RANK + MLA_KV_RANK
    cq = jnp.dot(hn, wc_ref[:, :o_kv], preferred_element_type=F32)
    ckv = jnp.dot(hn, wc_ref[:, o_kv:o_kr], preferred_element_type=F32)
    cqn = _rms(cq, gq_ref[...]).astype(BF16)
    kr_raw = jnp.dot(hn, wc_ref[:, o_kr:], preferred_element_type=F32)
    q = jnp.dot(cqn, wuq_ref[...], preferred_element_type=F32)
    ckvn = _rms(ckv, gkv_ref[...]).astype(BF16)
    cos, s_hi, s_lo = tab_ref[0], tab_ref[1], tab_ref[2]
    half = MLA_ROPE // 2
    kr = _apply_rope(kr_raw, cos, s_hi, s_lo, half).astype(BF16)
    v = jnp.dot(ckvn, wv_ref[...], preferred_element_type=F32)
    kn = jnp.dot(ckvn, wk_ref[...], preferred_element_type=F32)
    _store_heads_transposed(vt_ref, v)
    for h in range(MLA_HEADS):
        lo = 2 * h * LANES
        q_ref[:, lo:lo + LANES] = q[:, lo:lo + LANES].astype(BF16)
        q_ref[:, lo + LANES:lo + 2 * LANES] = _apply_rope(
            q[:, lo + LANES:lo + 2 * LANES], cos, s_hi, s_lo, half).astype(BF16)
        k_ref[:, lo:lo + LANES] = kn[:, h * LANES:(h + 1) * LANES].astype(BF16)
        k_ref[:, lo + LANES:lo + 2 * LANES] = kr


def _mla_proj(hn, wc, gq, gkv, wuq, wk, wv, tab, *, layer, tm=512):
    b, s, d = hn.shape
    tm = min(tm, s)
    nq = wuq.shape[2]
    heads = wv.shape[2] // HEAD_DIM
    full = lambda a: _layer_block(a, layer)
    return pl.pallas_call(
        _mla_proj_kernel,
        grid=(b, s // tm),
        in_specs=[
            pl.BlockSpec((None, tm, d), lambda bi, i: (bi, i, 0)),
            full(wc), full(gq), full(gkv), full(wuq), full(wk), full(wv),
            pl.BlockSpec((3, tm, LANES), lambda bi, i: (0, i, 0)),
        ],
        out_specs=[
            pl.BlockSpec((None, tm, nq), lambda bi, i: (bi, i, 0)),
            pl.BlockSpec((None, tm, nq), lambda bi, i: (bi, i, 0)),
            pl.BlockSpec((None, heads, HEAD_DIM, tm), lambda bi, i: (bi, 0, 0, i)),
        ],
        out_shape=[
            jax.ShapeDtypeStruct((b, s, nq), BF16),
            jax.ShapeDtypeStruct((b, s, nq), BF16),
            jax.ShapeDtypeStruct((b, heads, HEAD_DIM, s), BF16),
        ],
        compiler_params=_params("parallel", "parallel"),
        name="mla_proj",
    )(hn, wc, gq, gkv, wuq, wk, wv, tab)


def _dil_proj_kernel(hn_ref, w_ref, tab_ref, *rest, dils):
    out_refs, scr = rest[:len(dils)], rest[len(dils):]
    hn = hn_ref[...]
    cos, s_hi, s_lo = tab_ref[0], tab_ref[1], tab_ref[2]
    half = PARTIAL_ROPE // 2
    n_rot = 2 * DIL_HEADS
    tm = hn.shape[0]
    groups = [(0, DIL_HEADS), (DIL_HEADS, n_rot), (n_rot, n_rot + DIL_HEADS // 2),
              (n_rot + DIL_HEADS // 2, n_rot + DIL_HEADS)]

    def project(lo, hi):
        return jnp.dot(hn, w_ref[:, lo * LANES:hi * LANES], preferred_element_type=F32)

    def emit(lo, hi, y):
        for j in range(lo, hi):
            v = y[:, (j - lo) * LANES:(j - lo + 1) * LANES]
            scr[j][...] = _apply_rope(v, cos, s_hi, s_lo, half) if j < n_rot else v
            for o_ref, dil in zip(out_refs, dils):
                rows = tm // dil
                for r in range(dil):
                    o_ref[r, :, j * LANES:(j + 1) * LANES] = scr[j][pl.ds(r, rows, stride=dil), :].astype(BF16)

    ys = [project(*groups[0])]
    for g, (lo, hi) in enumerate(groups):
        if g + 1 < len(groups):
            ys.append(project(*groups[g + 1]))
        emit(lo, hi, ys[g])


def _dil_proj(hn, w, tab, *, layer, dils, tm=512):
    b, s, d = hn.shape
    tm = min(tm, s)
    n = w.shape[2]
    return pl.pallas_call(
        functools.partial(_dil_proj_kernel, dils=dils),
        grid=(b, s // tm),
        in_specs=[
            pl.BlockSpec((None, tm, d), lambda bi, i: (bi, i, 0)),
            _layer_block(w, layer),
            pl.BlockSpec((3, tm, LANES), lambda bi, i: (0, i, 0)),
        ],
        out_specs=[pl.BlockSpec((None, dil, tm // dil, n), lambda bi, i: (bi, 0, i, 0)) for dil in dils],
        out_shape=[jax.ShapeDtypeStruct((b, dil, s // dil, n), BF16) for dil in dils],
        scratch_shapes=[pltpu.VMEM((tm, LANES), F32)] * (n // LANES),
        compiler_params=_params("parallel", "parallel"),
        name="dil_proj",
    )(hn, w, tab)


def _cumsum_kernel(x_ref, o_ref, *, out_scale):
    x = x_ref[...]
    n = x.shape[-1]
    lane = lax.broadcasted_iota(jnp.int32, x.shape, 1)
    k = 1
    while k < n:
        x = x + jnp.where(lane >= k, pltpu.roll(x, k, 1), 0.0)
        k *= 2
    o_ref[...] = x * out_scale


def _key_bias(x, scale):
    return pl.pallas_call(
        functools.partial(_cumsum_kernel, out_scale=-1.0 / scale),
        out_shape=jax.ShapeDtypeStruct(x.shape, F32),
        compiler_params=pltpu.CompilerParams(vmem_limit_bytes=VMEM_LIMIT),
        name="forget_cumsum",
    )(x)


def _attn_kernel(*refs, tq, cq, tk, ahead, scale, has_bias):
    if has_bias:
        q_ref, k_ref, vt_ref, b_ref, o_ref, m_ref, acc_ref, worst_ref, brep_ref = refs
    else:
        q_ref, k_ref, vt_ref, o_ref, m_ref, acc_ref, worst_ref = refs
    qi = pl.program_id(2)
    n_chain = tq // cq
    dv = vt_ref.shape[0]
    c2 = scale * LOG2E

    if has_bias:
        @pl.when(qi == 0)
        def _():
            for j in range(brep_ref.shape[0] // tq):
                rows = jnp.broadcast_to(b_ref[:, j * tq:(j + 1) * tq], (LANES, tq))
                brep_ref[j * tq:(j + 1) * tq, :] = rows.T

    def scores(c, ks, size):
        q = q_ref[c * cq:(c + 1) * cq, :]
        k = k_ref[pl.ds(ks, size), :]
        return lax.dot_general(k, q, (((1,), (1,)), ((), ())), preferred_element_type=F32)

    def update(c, st, ks, size, exact):
        if has_bias:
            st = st + jnp.tile(brep_ref[pl.ds(ks, size), :], (1, cq // LANES))
        m_prev = m_ref[c]
        top = jnp.max(st, axis=0, keepdims=True)
        m_new = jnp.maximum(m_prev, top)
        alpha = jnp.exp2((m_prev - m_new) * c2)
        v_ext = jnp.concatenate([vt_ref[:, pl.ds(ks, size)], jnp.ones((dv, size), BF16)], axis=0)
        if exact:
            pt = jnp.exp2((st - m_new) * c2).astype(BF16)
            acc_ref[c] = alpha * acc_ref[c] + jnp.dot(v_ext, pt, preferred_element_type=F32)
        else:
            pt = jnp.exp2((st - m_prev) * c2).astype(BF16)
            acc_ref[c] = alpha * (acc_ref[c] + jnp.dot(v_ext, pt, preferred_element_type=F32))
            worst_ref[c] = jnp.maximum(worst_ref[c], (top - m_prev) * c2)
        m_ref[c] = m_new

    def run(work, ahead):
        pending = [scores(*w[:3]) for w in work[:ahead]]
        for i, (c, ks, size, exact) in enumerate(work):
            st = pending.pop(0)
            if i + ahead < len(work):
                pending.append(scores(*work[i + ahead][:3]))
            update(c, st, ks, size, exact)

    def diagonal_tile(exact):
        base = qi * tq
        look = min(4, n_chain)
        pending = [scores(c, pl.multiple_of(base + c * cq, cq), cq) for c in range(look)]
        for c in range(n_chain):
            ks = pl.multiple_of(base + c * cq, cq)
            st = pending.pop(0)
            if c + look < n_chain:
                pending.append(scores(c + look, pl.multiple_of(base + (c + look) * cq, cq), cq))
            if has_bias:
                st = st + jnp.tile(brep_ref[pl.ds(ks, cq), :], (1, cq // LANES))
            key = lax.broadcasted_iota(jnp.int32, st.shape, 0)
            qry = lax.broadcasted_iota(jnp.int32, st.shape, 1)
            st = jnp.where(key <= qry, st, NEG)
            top = jnp.max(st, axis=0, keepdims=True)
            pt = jnp.exp2((st - top) * c2).astype(BF16)
            v_ext = jnp.concatenate([vt_ref[:, pl.ds(ks, cq)], jnp.ones((dv, cq), BF16)], axis=0)
            m_ref[c] = top
            acc_ref[c] = jnp.dot(v_ext, pt, preferred_element_type=F32)
        n_piece = -(-((n_chain - 1) * cq) // tk)
        work = []
        for piece in range(n_piece):
            for c in range(1, n_chain):
                size = min(tk, c * cq - piece * tk)
                if size > 0:
                    work.append((c, pl.multiple_of(base + piece * tk, cq), size, exact))
        run(work, ahead)

    def attend(exact):
        diagonal_tile(exact)

        def body(j, carry):
            run([(c, pl.multiple_of(j * tq + part * tk, tk), tk, exact)
                 for part in range(tq // tk) for c in range(n_chain)], ahead)
            return carry

        lax.fori_loop(0, qi, body, 0)

    worst_ref[...] = jnp.zeros(worst_ref.shape, F32)
    attend(exact=False)

    @pl.when(jnp.max(worst_ref[...]) > EXP2_HEADROOM)
    def _():
        attend(exact=True)

    for c in range(n_chain):
        acc = acc_ref[c]
        o_ref[c * cq:(c + 1) * cq, :] = (acc[:dv] / acc[dv:]).T.astype(o_ref.dtype)


def _attention(q_arr, k_arr, vt_arr, bias, *, dk, q_off, k_off, scale, tq=2048, cq=256, tk=1024, ahead=2):
    b, s, _ = q_arr.shape
    heads, dv = vt_arr.shape[1], vt_arr.shape[2]
    tq = min(tq, s)
    cq = min(cq, tq)
    n_chain = tq // cq
    in_specs = [
        pl.BlockSpec((None, tq, dk), lambda bi, h, qi: (bi, qi, q_off + h)),
        pl.BlockSpec((None, s, dk), lambda bi, h, qi: (bi, 0, k_off + h)),
        pl.BlockSpec((None, None, dv, s), lambda bi, h, qi: (bi, h, 0, 0)),
    ]
    args = [q_arr, k_arr, vt_arr]
    stat = pltpu.VMEM((n_chain, 1, cq), F32)
    scratch = [stat, pltpu.VMEM((n_chain, 2 * dv, cq), F32), stat]
    if bias is not None:
        in_specs.append(pl.BlockSpec((None, None, 1, s), lambda bi, h, qi: (bi, h, 0, 0)))
        args.append(bias)
        scratch.append(pltpu.VMEM((s, LANES), F32))
    return pl.pallas_call(
        functools.partial(_attn_kernel, tq=tq, cq=cq, tk=min(tk, tq), ahead=ahead, scale=scale,
                          has_bias=bias is not None),
        grid=(b, heads, s // tq),
        in_specs=in_specs,
        out_specs=pl.BlockSpec((None, tq, dv), lambda bi, h, qi: (bi, qi, h)),
        out_shape=jax.ShapeDtypeStruct((b, s, heads * dv), BF16),
        scratch_shapes=scratch,
        compiler_params=_params("arbitrary", "arbitrary", "arbitrary"),
        name="attn_bias" if bias is not None else "attn",
    )(*args)


def _dil_kernel(q_ref, k_ref, v_ref, kp_ref, vp_ref, o_ref, lse_ref, kbuf, vbuf, *, scale, group):
    planes, tl, _ = q_ref.shape
    nb = tl // BLOCK
    c2 = scale * LOG2E
    kbuf[:, :BLOCK, :] = kp_ref[...]
    kbuf[:, BLOCK:, :] = k_ref[...]
    vbuf[:, :BLOCK, :] = vp_ref[...]
    vbuf[:, BLOCK:, :] = v_ref[...]
    row = lax.broadcasted_iota(jnp.int32, (BLOCK, 2 * BLOCK), 0)
    col = lax.broadcasted_iota(jnp.int32, (BLOCK, 2 * BLOCK), 1)
    band = (col >= row) & (col <= row + BLOCK)
    lane = lax.broadcasted_iota(jnp.int32, (BLOCK, LANES), 1)
    ones = jnp.ones((2 * BLOCK, HEAD_DIM), BF16)
    heads = [slice(h * HEAD_DIM, (h + 1) * HEAD_DIM) for h in range(DIL_HEADS)]
    first_chunk = pl.program_id(2) == 0

    def locate(g, u):
        if nb % group == 0:
            gpp = nb // group
            p = g // gpp if planes > 1 else 0
            first = (g % gpp == 0) if u == 0 else None
            return p, (g % gpp) * group + u, first
        per = group // nb
        return g * per + u // nb, u % nb, (True if u % nb == 0 else None)

    def body(g, carry):
        work, masks = [], []
        for u in range(group):
            p, n, first = locate(g, u)
            start = n * BLOCK if isinstance(n, int) else pl.multiple_of(n * BLOCK, BLOCK)
            if first is None:
                masks.append(band)
            else:
                cond = first_chunk if first is True else jnp.logical_and(first_chunk, first)
                masks.append(band & (col >= jnp.where(cond, BLOCK, 0)))
            work += [(u, p, start, h) for h in range(DIL_HEADS)]

        def scores(u, p, start, h):
            q = q_ref[p, pl.ds(start, BLOCK), heads[h]]
            kc = kbuf[p, pl.ds(start, 2 * BLOCK), heads[h]]
            return lax.dot_general(q, kc, (((1,), (1,)), ((), ())), preferred_element_type=F32)

        ahead = 2 * DIL_HEADS
        pending = [scores(*w) for w in work[:ahead]]
        tiles = {}
        for i, (u, p, start, h) in enumerate(work):
            s = pending.pop(0)
            if i + ahead < len(work):
                pending.append(scores(*work[i + ahead]))
            s = jnp.where(masks[u], s, NEG)
            m = jnp.max(s, axis=-1, keepdims=True)
            e = jnp.exp2((s - m) * c2).astype(BF16)
            v_ext = jnp.concatenate([vbuf[p, pl.ds(start, 2 * BLOCK), heads[h]], ones], axis=1)
            pv = jnp.dot(e, v_ext, preferred_element_type=F32)
            denom = pv[:, HEAD_DIM:]
            o_ref[p, pl.ds(start, BLOCK), heads[h]] = pv[:, :HEAD_DIM] / denom
            tile = tiles.get(u, jnp.zeros((BLOCK, LANES), F32))
            tiles[u] = jnp.where(lane == h, m * scale + jnp.log(denom), tile)
            if h == DIL_HEADS - 1:
                lse_ref[p, pl.ds(start, BLOCK), :] = tiles.pop(u)
        return carry

    n_groups = planes * nb // group
    if n_groups == 1:
        body(0, 0)
    else:
        lax.fori_loop(0, n_groups, body, 0)


def _dil_branch(qkv, *, scale, rows=1024, group=4):
    b, dil, sub, w3 = qkv.shape
    w = w3 // 3
    tl = min(rows, sub)
    planes = min(rows // tl, dil)
    bpc = tl // BLOCK
    group = min(group, planes * bpc)
    assert (bpc % group == 0 or group % bpc == 0) and (planes * bpc) % group == 0
    own = lambda j: pl.BlockSpec((None, planes, tl, w), lambda bi, r, c: (bi, r, c, j))
    prev = lambda j: pl.BlockSpec((None, planes, BLOCK, w),
                                  lambda bi, r, c: (bi, r, jnp.maximum(c * bpc - 1, 0), j))
    return pl.pallas_call(
        functools.partial(_dil_kernel, scale=scale, group=group),
        grid=(b, dil // planes, sub // tl),
        in_specs=[own(0), own(1), own(2), prev(1), prev(2)],
        out_specs=[
            pl.BlockSpec((None, planes, tl, w), lambda bi, r, c: (bi, r, c, 0)),
            pl.BlockSpec((None, planes, tl, LANES), lambda bi, r, c: (bi, r, c, 0)),
        ],
        out_shape=[
            jax.ShapeDtypeStruct((b, dil, sub, w), F32),
            jax.ShapeDtypeStruct((b, dil, sub, LANES), F32),
        ],
        scratch_shapes=[pltpu.VMEM((planes, tl + BLOCK, w), BF16)] * 2,
        compiler_params=_params("parallel", "parallel", "parallel"),
        name=f"dilated_{dil}",
    )(qkv, qkv, qkv, qkv, qkv)


def _out_proj_kernel(x_ref, a_ref, b_ref, *rest, dils):
    nbr = len(dils)
    o_refs, l_refs = rest[:nbr], rest[nbr:2 * nbr]
    w_ref, y_ref = rest[2 * nbr:2 * nbr + 2]
    scr = rest[2 * nbr + 2:]
    tm = x_ref.shape[0]

    def natural(ref, dil, slot, j):
        sl = slice(j * LANES, (j + 1) * LANES)
        if dil == 1:
            return ref[0, :, sl]
        rows = tm // dil
        for r in range(dil):
            scr[slot][pl.ds(r, rows, stride=dil), :] = ref[r, :, sl]
        return scr[slot][...]

    ab = jnp.concatenate([a_ref[...], b_ref[...]], axis=-1)
    n_ab = ab.shape[1]
    y = x_ref[...] + jnp.dot(ab, w_ref[:n_ab, :], preferred_element_type=F32)

    lses = [natural(l_refs[i], dils[i], 2 * i, 0) for i in range(nbr)]
    m = functools.reduce(jnp.maximum, lses)
    es = [jnp.exp(l - m) for l in lses]
    tot = functools.reduce(lambda u, v: u + v, es)
    wts = [e / tot for e in es]
    parts = []
    for h in range(DIL_HEADS):
        c = None
        for i in range(nbr):
            term = wts[i][:, h:h + 1] * natural(o_refs[i], dils[i], 2 * i + 1, h)
            c = term if c is None else c + term
        parts.append(c.astype(BF16))
    mixed = jnp.concatenate(parts, axis=-1)
    y_ref[...] = y + jnp.dot(mixed, w_ref[n_ab:, :], preferred_element_type=F32)


def _out_proj(x, a, bm, os_, ls_, w, *, layer, tm=512):
    b, s, d = x.shape
    tm = min(tm, s)
    dils = tuple(o.shape[1] for o in os_)
    row = lambda arr: pl.BlockSpec((None, tm, arr.shape[2]), lambda bi, i: (bi, i, 0))
    plane = lambda arr: pl.BlockSpec((None, arr.shape[1], tm // arr.shape[1], arr.shape[3]),
                                     lambda bi, i: (bi, 0, i, 0))
    return pl.pallas_call(
        functools.partial(_out_proj_kernel, dils=dils),
        grid=(b, s // tm),
        in_specs=[row(x), row(a), row(bm)] + [plane(o) for o in os_] + [plane(l) for l in ls_]
                 + [_layer_block(w, layer)],
        out_specs=row(x),
        out_shape=jax.ShapeDtypeStruct((b, s, d), F32),
        scratch_shapes=[pltpu.VMEM((tm, LANES), F32)] * (2 * len(dils)),
        compiler_params=_params("parallel", "parallel"),
        name="out_proj",
    )(x, a, bm, *os_, *ls_, w)


def _rope_table(seq, dim, fill):
    half = dim // 2
    inv = 1.0 / (ROPE_THETA ** (jnp.arange(0, dim, 2, dtype=F32) / dim))
    ang = jnp.arange(seq, dtype=F32)[:, None] * inv[None, :]
    cos, sin = jnp.cos(ang), jnp.sin(ang)
    rest = LANES - dim
    zeros_h = jnp.zeros((seq, half), F32)
    zeros_r = jnp.zeros((seq, rest), F32)
    cos_t = jnp.concatenate([cos, cos, jnp.full((seq, rest), fill, F32)], axis=1)
    s_hi = jnp.concatenate([-sin, zeros_h, zeros_r], axis=1)
    s_lo = jnp.concatenate([zeros_h, sin, zeros_r], axis=1)
    return jnp.stack([cos_t, s_hi, s_lo], axis=0)


_FW = FOX_HEADS * HEAD_DIM
_O_FL = 3 * _FW
_O_CQ = _O_FL + FOX_HEADS
_O_KR = _O_CQ + MLA_Q_RANK + MLA_KV_RANK
_O_DQ = _O_KR + MLA_ROPE


def _split_w_in_kernel(w_ref, fqk_ref, fv_ref, fl_ref, c_ref, dil_ref):
    w = w_ref[...]
    lane = lax.broadcasted_iota(jnp.int32, (w.shape[0], LANES), 1)
    fqk_ref[...] = w[:, :2 * _FW].astype(BF16)
    fv_ref[...] = w[:, 2 * _FW:_O_FL].astype(BF16)
    fl_ref[...] = jnp.where(lane < FOX_HEADS, w[:, _O_FL:_O_FL + LANES], 0.0).astype(BF16)
    c_ref[:, :_O_KR - _O_CQ] = w[:, _O_CQ:_O_KR].astype(BF16)
    c_ref[:, _O_KR - _O_CQ:] = jnp.where(lane < MLA_ROPE, w[:, _O_KR:_O_KR + LANES], 0.0).astype(BF16)
    dil_ref[...] = w[:, _O_DQ:].astype(BF16)


def _split_w_in(w_in, *, rows=256):
    depth, d, n_in = w_in.shape
    n_c = _O_KR - _O_CQ + LANES
    widths = (2 * _FW, _FW, LANES, n_c, n_in - _O_DQ)
    return pl.pallas_call(
        _split_w_in_kernel,
        grid=(depth, d // rows),
        in_specs=[pl.BlockSpec((None, rows, n_in), lambda l, i: (l, i, 0))],
        out_specs=[pl.BlockSpec((None, rows, n), lambda l, i: (l, i, 0)) for n in widths],
        out_shape=[jax.ShapeDtypeStruct((depth, d, n), BF16) for n in widths],
        compiler_params=_params("parallel", "parallel"),
        name="split_w_in",
    )(w_in)


def kernel(x, ffn1_norm, ffn1_w_gate, ffn1_w_up, ffn1_w_down, mix_norm, w_in, fox_forget_bias, mla_q_norm, mla_kv_norm, mla_w_uq, mla_w_ukv, w_out, ffn2_norm, ffn2_w_gate, ffn2_w_up, ffn2_w_down, final_norm):
    b, s, d = x.shape
    depth = w_in.shape[0]
    t = b * s

    assert all(window == BLOCK * dil and s % (BLOCK * dil) == 0 for window, dil in DIL_BRANCHES)
    tab_mla = _rope_table(s, MLA_ROPE, 1.0)
    tab_dil = _rope_table(s, PARTIAL_ROPE, 1.0)

    pad_last = lambda w, n: jnp.pad(w, [(0, 0)] * (w.ndim - 1) + [(0, n - w.shape[-1])])
    w_fqk_all, w_fv_all, w_fl_all, w_c_all, w_dil_all = _split_w_in(w_in)
    fbias_all = pad_last(fox_forget_bias.reshape(depth, 1, FOX_HEADS), LANES)
    wuq_all = mla_w_uq.reshape(depth, MLA_Q_RANK, MLA_HEADS, MLA_NOPE + MLA_ROPE)
    wuq_all = pad_last(wuq_all, 2 * LANES).reshape(depth, MLA_Q_RANK, MLA_HEADS * 2 * LANES).astype(BF16)
    wukv = mla_w_ukv.reshape(depth, MLA_KV_RANK, MLA_HEADS, MLA_NOPE + HEAD_DIM)
    w_k_all = wukv[..., :MLA_NOPE].reshape(depth, MLA_KV_RANK, MLA_HEADS * MLA_NOPE).astype(BF16)
    w_v_all = wukv[..., MLA_NOPE:].reshape(depth, MLA_KV_RANK, MLA_HEADS * HEAD_DIM).astype(BF16)
    w_out_all = w_out.astype(BF16)

    ffn1_w = [w.astype(BF16) for w in (ffn1_w_gate, ffn1_w_up, ffn1_w_down)]
    ffn2_w = [w.astype(BF16) for w in (ffn2_w_gate, ffn2_w_up, ffn2_w_down)]

    gq_all = mla_q_norm.reshape(depth, 1, MLA_Q_RANK)
    gkv_all = mla_kv_norm.reshape(depth, 1, MLA_KV_RANK)

    xf = x.reshape(t, d)
    for l in range(depth):
        last = l == depth - 1

        xf, hn = _ffn(xf, ffn1_norm[l], *ffn1_w, mix_norm[l], layer=l, emit_norm=True)

        hn = hn.reshape(b, s, d)
        fox_scale = HEAD_DIM ** -0.5
        fqk, fvt, logf = _fox_proj(hn, w_fqk_all, w_fv_all, w_fl_all, fbias_all, layer=l)
        logf = logf[:, :, :FOX_HEADS].transpose(0, 2, 1).reshape(b * FOX_HEADS, s)
        key_bias = _key_bias(logf, fox_scale).reshape(b, FOX_HEADS, 1, s)
        out_a = _attention(fqk, fqk, fvt, key_bias, dk=HEAD_DIM, q_off=0, k_off=FOX_HEADS, scale=fox_scale)

        q_b, k_b, vt_b = _mla_proj(hn, w_c_all, gq_all, gkv_all, wuq_all, w_k_all, w_v_all, tab_mla, layer=l)
        out_b = _attention(q_b, k_b, vt_b, None, dk=2 * LANES, q_off=0, k_off=0,
                           scale=(MLA_NOPE + MLA_ROPE) ** -0.5)

        dqkvs = _dil_proj(hn, w_dil_all, tab_dil, layer=l, dils=tuple(dil for _, dil in DIL_BRANCHES))
        outs, lses = [], []
        for dqkv in dqkvs:
            o, lse = _dil_branch(dqkv, scale=HEAD_DIM ** -0.5)
            outs.append(o)
            lses.append(lse)

        xf = _out_proj(xf.reshape(b, s, d), out_a, out_b, outs, lses, w_out_all, layer=l).reshape(t, d)

        xf = _ffn(xf, ffn2_norm[l], *ffn2_w, final_norm if last else None, layer=l, final_norm=last)
    return xf.reshape(b, s, d)
```

```python
import functools

import jax
import jax.numpy as jnp
from jax import lax
from jax.experimental import pallas as pl
from jax.experimental.pallas import tpu as pltpu

F32 = jnp.float32
BF16 = jnp.bfloat16

LANES = 128
HEAD_DIM = 128
BLOCK = 128
EPS = 1e-6
ROPE_THETA = 500000.0
PARTIAL_ROPE = HEAD_DIM // 4
FOX_HEADS = 4
MLA_HEADS = 8
MLA_Q_RANK = 512
MLA_KV_RANK = 512
MLA_NOPE = 128
MLA_ROPE = 64
DIL_HEADS = 4
DIL_BRANCHES = ((128, 1), (512, 4), (2048, 16))
NEG = -1e30
LOG2E = 1.4426950408889634
EXP2_HEADROOM = 100.0
VMEM_LIMIT = 56 * 1024 * 1024
FFN_TILE = 512


def _params(*sem):
    return pltpu.CompilerParams(dimension_semantics=sem, vmem_limit_bytes=VMEM_LIMIT)


def _layer_block(stacked, layer):
    return pl.BlockSpec((None,) + stacked.shape[1:], lambda *_: (layer,) + (0,) * (stacked.ndim - 1))


def _rms(x, g):
    return x * lax.rsqrt(jnp.mean(x * x, axis=-1, keepdims=True) + EPS) * g


def _apply_rope(x, cos, s_hi, s_lo, half):
    return x * cos + pltpu.roll(x, LANES - half, 1) * s_hi + pltpu.roll(x, half, 1) * s_lo


def _ffn_kernel(*refs, emit_norm, final_norm):
    x_hbm, g_ref, wg_ref, wu_ref, wd_ref = refs[:5]
    pos = 5
    g2_ref = None
    if emit_norm or final_norm:
        g2_ref = refs[pos]
        pos += 1
    o_ref = refs[pos]
    pos += 1
    hn_ref = None
    if emit_norm:
        hn_ref = refs[pos]
        pos += 1
    xn_ref, xbuf, sem = refs[pos:pos + 3]

    i, j = pl.program_id(0), pl.program_id(1)
    ni, nj = pl.num_programs(0), pl.num_programs(1)
    tm = xbuf.shape[0]

    def x_copy(tile):
        return pltpu.make_async_copy(x_hbm.at[pl.ds(pl.multiple_of(tile * tm, tm), tm), :], xbuf, sem)

    @pl.when(jnp.logical_and(i == 0, j == 0))
    def _():
        x_copy(0).start()

    def ff_tile():
        xn = xn_ref[...]
        gate = jnp.dot(xn, wg_ref[...], preferred_element_type=F32)
        up = jnp.dot(xn, wu_ref[...], preferred_element_type=F32)
        h = (gate * jax.nn.sigmoid(gate) * up * 0.5).astype(BF16)
        return jnp.dot(h, wd_ref[...], preferred_element_type=F32)

    @pl.when(j == 0)
    def _():
        x_copy(i).wait()
        xn_ref[...] = _rms(xbuf[...], g_ref[...]).astype(BF16)
        o_ref[...] = xbuf[...] + ff_tile()

    @pl.when(jnp.logical_and(j == 1, i + 1 < ni))
    def _():
        x_copy(i + 1).start()

    @pl.when(j > 0)
    def _():
        o_ref[...] += ff_tile()

    if emit_norm or final_norm:
        @pl.when(j == nj - 1)
        def _():
            y = _rms(o_ref[...], g2_ref[...])
            if emit_norm:
                hn_ref[...] = y.astype(BF16)
            else:
                o_ref[...] = y


def _ffn(x, g, wg, wu, wd, g2=None, *, layer, emit_norm=False, final_norm=False, tm=1024, tf=FFN_TILE):
    t, d = x.shape
    nj = wd.shape[1] // tf
    tm = min(tm, t)
    assert nj >= 2
    in_specs = [
        pl.BlockSpec(memory_space=pl.ANY),
        pl.BlockSpec((1, d), lambda i, j: (0, 0)),
        pl.BlockSpec((None, d, tf), lambda i, j: (layer, 0, j)),
        pl.BlockSpec((None, d, tf), lambda i, j: (layer, 0, j)),
        pl.BlockSpec((None, tf, d), lambda i, j: (layer, j, 0)),
    ]
    args = [x, g.reshape(1, d), wg, wu, wd]
    if emit_norm or final_norm:
        in_specs.append(pl.BlockSpec((1, d), lambda i, j: (0, 0)))
        args.append(g2.reshape(1, d))
    out_shape = [jax.ShapeDtypeStruct((t, d), F32)]
    out_specs = [pl.BlockSpec((tm, d), lambda i, j: (i, 0))]
    if emit_norm:
        out_shape.append(jax.ShapeDtypeStruct((t, d), BF16))
        out_specs.append(pl.BlockSpec((tm, d), lambda i, j: (i, 0)))
    res = pl.pallas_call(
        functools.partial(_ffn_kernel, emit_norm=emit_norm, final_norm=final_norm),
        grid=(t // tm, nj),
        in_specs=in_specs,
        out_specs=out_specs,
        out_shape=out_shape,
        scratch_shapes=[pltpu.VMEM((tm, d), BF16), pltpu.VMEM((tm, d), F32), pltpu.SemaphoreType.DMA],
        compiler_params=_params("arbitrary", "arbitrary"),
        name="ffn",
    )(*args)
    return res if emit_norm else res[0]


def _store_heads_transposed(vt_ref, v):
    for h in range(vt_ref.shape[0]):
        vt_ref[h] = v[:, h * HEAD_DIM:(h + 1) * HEAD_DIM].T.astype(BF16)


def _fox_proj_kernel(hn_ref, wqk_ref, wv_ref, wl_ref, b_ref, qk_ref, vt_ref, lf_ref):
    hn = hn_ref[...]
    v = jnp.dot(hn, wv_ref[...], preferred_element_type=F32)
    z = jnp.dot(hn, wl_ref[...], preferred_element_type=F32) + b_ref[...]
    qk = jnp.dot(hn, wqk_ref[...], preferred_element_type=F32)
    _store_heads_transposed(vt_ref, v)
    lf_ref[...] = jnp.minimum(z, 0.0) - jnp.log1p(jnp.exp(-jnp.abs(z)))
    qk_ref[...] = qk.astype(BF16)


def _fox_proj(hn, wqk, wv, wl, bias, *, layer, tm=512):
    b, s, d = hn.shape
    tm = min(tm, s)
    n = wqk.shape[2]
    heads = wv.shape[2] // HEAD_DIM
    full = lambda a: _layer_block(a, layer)
    return pl.pallas_call(
        _fox_proj_kernel,
        grid=(b, s // tm),
        in_specs=[pl.BlockSpec((None, tm, d), lambda bi, i: (bi, i, 0)), full(wqk), full(wv), full(wl), full(bias)],
        out_specs=[
            pl.BlockSpec((None, tm, n), lambda bi, i: (bi, i, 0)),
            pl.BlockSpec((None, heads, HEAD_DIM, tm), lambda bi, i: (bi, 0, 0, i)),
            pl.BlockSpec((None, tm, LANES), lambda bi, i: (bi, i, 0)),
        ],
        out_shape=[
            jax.ShapeDtypeStruct((b, s, n), BF16),
            jax.ShapeDtypeStruct((b, heads, HEAD_DIM, s), BF16),
            jax.ShapeDtypeStruct((b, s, LANES), F32),
        ],
        compiler_params=_params("parallel", "parallel"),
        name="fox_proj",
    )(hn, wqk, wv, wl, bias)


def _mla_proj_kernel(hn_ref, wc_ref, gq_ref, gkv_ref, wuq_ref, wk_ref, wv_ref, tab_ref,
                     q_ref, k_ref, vt_ref):
    hn = hn_ref[...]
    o_kv, o_kr = MLA_Q_RANK, MLA_Q_RANK + MLA_KV_RANK
    cq = jnp.dot(hn, wc_ref[:, :o_kv], preferred_element_type=F32)
    ckv = jnp.dot(hn, wc_ref[:, o_kv:o_kr], preferred_element_type=F32)
    cqn = _rms(cq, gq_ref[...]).astype(BF16)
    kr_raw = jnp.dot(hn, wc_ref[:, o_kr:], preferred_element_type=F32)
    q = jnp.dot(cqn, wuq_ref[...], preferred_element_type=F32)
    ckvn = _rms(ckv, gkv_ref[...]).astype(BF16)
    cos, s_hi, s_lo = tab_ref[0], tab_ref[1], tab_ref[2]
    half = MLA_ROPE // 2
    kr = _apply_rope(kr_raw, cos, s_hi, s_lo, half).astype(BF16)
    v = jnp.dot(ckvn, wv_ref[...], preferred_element_type=F32)
    kn = jnp.dot(ckvn, wk_ref[...], preferred_element_type=F32)
    _store_heads_transposed(vt_ref, v)
    for h in range(MLA_HEADS):
        lo = 2 * h * LANES
        q_ref[:, lo:lo + LANES] = q[:, lo:lo + LANES].astype(BF16)
        q_ref[:, lo + LANES:lo + 2 * LANES] = _apply_rope(
            q[:, lo + LANES:lo + 2 * LANES], cos, s_hi, s_lo, half).astype(BF16)
        k_ref[:, lo:lo + LANES] = kn[:, h * LANES:(h + 1) * LANES].astype(BF16)
        k_ref[:, lo + LANES:lo + 2 * LANES] = kr


def _mla_proj(hn, wc, gq, gkv, wuq, wk, wv, tab, *, layer, tm=512):
    b, s, d = hn.shape
    tm = min(tm, s)
    nq = wuq.shape[2]
    heads = wv.shape[2] // HEAD_DIM
    full = lambda a: _layer_block(a, layer)
    return pl.pallas_call(
        _mla_proj_kernel,
        grid=(b, s // tm),
        in_specs=[
            pl.BlockSpec((None, tm, d), lambda bi, i: (bi, i, 0)),
            full(wc), full(gq), full(gkv), full(wuq), full(wk), full(wv),
            pl.BlockSpec((3, tm, LANES), lambda bi, i: (0, i, 0)),
        ],
        out_specs=[
            pl.BlockSpec((None, tm, nq), lambda bi, i: (bi, i, 0)),
            pl.BlockSpec((None, tm, nq), lambda bi, i: (bi, i, 0)),
            pl.BlockSpec((None, heads, HEAD_DIM, tm), lambda bi, i: (bi, 0, 0, i)),
        ],
        out_shape=[
            jax.ShapeDtypeStruct((b, s, nq), BF16),
            jax.ShapeDtypeStruct((b, s, nq), BF16),
            jax.ShapeDtypeStruct((b, heads, HEAD_DIM, s), BF16),
        ],
        compiler_params=_params("parallel", "parallel"),
        name="mla_proj",
    )(hn, wc, gq, gkv, wuq, wk, wv, tab)


def _dil_proj_kernel(hn_ref, w_ref, tab_ref, *rest, dils):
    out_refs, scr = rest[:len(dils)], rest[len(dils):]
    hn = hn_ref[...]
    cos, s_hi, s_lo = tab_ref[0], tab_ref[1], tab_ref[2]
    half = PARTIAL_ROPE // 2
    n_rot = 2 * DIL_HEADS
    tm = hn.shape[0]
    groups = [(0, DIL_HEADS), (DIL_HEADS, n_rot), (n_rot, n_rot + DIL_HEADS // 2),
              (n_rot + DIL_HEADS // 2, n_rot + DIL_HEADS)]

    def project(lo, hi):
        return jnp.dot(hn, w_ref[:, lo * LANES:hi * LANES], preferred_element_type=F32)

    def emit(lo, hi, y):
        for j in range(lo, hi):
            v = y[:, (j - lo) * LANES:(j - lo + 1) * LANES]
            scr[j][...] = _apply_rope(v, cos, s_hi, s_lo, half) if j < n_rot else v
            for o_ref, dil in zip(out_refs, dils):
                rows = tm // dil
                for r in range(dil):
                    o_ref[r, :, j * LANES:(j + 1) * LANES] = scr[j][pl.ds(r, rows, stride=dil), :].astype(BF16)

    ys = [project(*groups[0])]
    for g, (lo, hi) in enumerate(groups):
        if g + 1 < len(groups):
            ys.append(project(*groups[g + 1]))
        emit(lo, hi, ys[g])


def _dil_proj(hn, w, tab, *, layer, dils, tm=512):
    b, s, d = hn.shape
    tm = min(tm, s)
    n = w.shape[2]
    return pl.pallas_call(
        functools.partial(_dil_proj_kernel, dils=dils),
        grid=(b, s // tm),
        in_specs=[
            pl.BlockSpec((None, tm, d), lambda bi, i: (bi, i, 0)),
            _layer_block(w, layer),
            pl.BlockSpec((3, tm, LANES), lambda bi, i: (0, i, 0)),
        ],
        out_specs=[pl.BlockSpec((None, dil, tm // dil, n), lambda bi, i: (bi, 0, i, 0)) for dil in dils],
        out_shape=[jax.ShapeDtypeStruct((b, dil, s // dil, n), BF16) for dil in dils],
        scratch_shapes=[pltpu.VMEM((tm, LANES), F32)] * (n // LANES),
        compiler_params=_params("parallel", "parallel"),
        name="dil_proj",
    )(hn, w, tab)


def _cumsum_kernel(x_ref, o_ref, *, out_scale):
    x = x_ref[...]
    n = x.shape[-1]
    lane = lax.broadcasted_iota(jnp.int32, x.shape, 1)
    k = 1
    while k < n:
        x = x + jnp.where(lane >= k, pltpu.roll(x, k, 1), 0.0)
        k *= 2
    o_ref[...] = x * out_scale


def _key_bias(x, scale):
    return pl.pallas_call(
        functools.partial(_cumsum_kernel, out_scale=-1.0 / scale),
        out_shape=jax.ShapeDtypeStruct(x.shape, F32),
        compiler_params=pltpu.CompilerParams(vmem_limit_bytes=VMEM_LIMIT),
        name="forget_cumsum",
    )(x)


def _attn_kernel(*refs, tq, cq, tk, ahead, scale, has_bias):
    if has_bias:
        q_ref, k_ref, vt_ref, b_ref, o_ref, m_ref, acc_ref, worst_ref, brep_ref = refs
    else:
        q_ref, k_ref, vt_ref, o_ref, m_ref, acc_ref, worst_ref = refs
    qi = pl.program_id(2)
    n_chain = tq // cq
    dv = vt_ref.shape[0]
    c2 = scale * LOG2E

    if has_bias:
        @pl.when(qi == 0)
        def _():
            for j in range(brep_ref.shape[0] // tq):
                rows = jnp.broadcast_to(b_ref[:, j * tq:(j + 1) * tq], (LANES, tq))
                brep_ref[j * tq:(j + 1) * tq, :] = rows.T

    def scores(c, ks, size):
        q = q_ref[c * cq:(c + 1) * cq, :]
        k = k_ref[pl.ds(ks, size), :]
        return lax.dot_general(k, q, (((1,), (1,)), ((), ())), preferred_element_type=F32)

    def update(c, st, ks, size, exact):
        if has_bias:
            st = st + jnp.tile(brep_ref[pl.ds(ks, size), :], (1, cq // LANES))
        m_prev = m_ref[c]
        top = jnp.max(st, axis=0, keepdims=True)
        m_new = jnp.maximum(m_prev, top)
        alpha = jnp.exp2((m_prev - m_new) * c2)
        v_ext = jnp.concatenate([vt_ref[:, pl.ds(ks, size)], jnp.ones((dv, size), BF16)], axis=0)
        if exact:
            pt = jnp.exp2((st - m_new) * c2).astype(BF16)
            acc_ref[c] = alpha * acc_ref[c] + jnp.dot(v_ext, pt, preferred_element_type=F32)
        else:
            pt = jnp.exp2((st - m_prev) * c2).astype(BF16)
            acc_ref[c] = alpha * (acc_ref[c] + jnp.dot(v_ext, pt, preferred_element_type=F32))
            worst_ref[c] = jnp.maximum(worst_ref[c], (top - m_prev) * c2)
        m_ref[c] = m_new

    def run(work, ahead):
        pending = [scores(*w[:3]) for w in work[:ahead]]
        for i, (c, ks, size, exact) in enumerate(work):
            st = pending.pop(0)
            if i + ahead < len(work):
                pending.append(scores(*work[i + ahead][:3]))
            update(c, st, ks, size, exact)

    def diagonal_tile(exact):
        base = qi * tq
        look = min(4, n_chain)
        pending = [scores(c, pl.multiple_of(base + c * cq, cq), cq) for c in range(look)]
        for c in range(n_chain):
            ks = pl.multiple_of(base + c * cq, cq)
            st = pending.pop(0)
            if c + look < n_chain:
                pending.append(scores(c + look, pl.multiple_of(base + (c + look) * cq, cq), cq))
            if has_bias:
                st = st + jnp.tile(brep_ref[pl.ds(ks, cq), :], (1, cq // LANES))
            key = lax.broadcasted_iota(jnp.int32, st.shape, 0)
            qry = lax.broadcasted_iota(jnp.int32, st.shape, 1)
            st = jnp.where(key <= qry, st, NEG)
            top = jnp.max(st, axis=0, keepdims=True)
            pt = jnp.exp2((st - top) * c2).astype(BF16)
            v_ext = jnp.concatenate([vt_ref[:, pl.ds(ks, cq)], jnp.ones((dv, cq), BF16)], axis=0)
            m_ref[c] = top
            acc_ref[c] = jnp.dot(v_ext, pt, preferred_element_type=F32)
        n_piece = -(-((n_chain - 1) * cq) // tk)
        work = []
        for piece in range(n_piece):
            for c in range(1, n_chain):
                size = min(tk, c * cq - piece * tk)
                if size > 0:
                    work.append((c, pl.multiple_of(base + piece * tk, cq), size, exact))
        run(work, ahead)

    def attend(exact):
        diagonal_tile(exact)

        def body(j, carry):
            run([(c, pl.multiple_of(j * tq + part * tk, tk), tk, exact)
                 for part in range(tq // tk) for c in range(n_chain)], ahead)
            return carry

        lax.fori_loop(0, qi, body, 0)

    worst_ref[...] = jnp.zeros(worst_ref.shape, F32)
    attend(exact=False)

    @pl.when(jnp.max(worst_ref[...]) > EXP2_HEADROOM)
    def _():
        attend(exact=True)

    for c in range(n_chain):
        acc = acc_ref[c]
        o_ref[c * cq:(c + 1) * cq, :] = (acc[:dv] / acc[dv:]).T.astype(o_ref.dtype)


def _attention(q_arr, k_arr, vt_arr, bias, *, dk, q_off, k_off, scale, tq=2048, cq=256, tk=1024, ahead=2):
    b, s, _ = q_arr.shape
    heads, dv = vt_arr.shape[1], vt_arr.shape[2]
    tq = min(tq, s)
    cq = min(cq, tq)
    n_chain = tq // cq
    in_specs = [
        pl.BlockSpec((None, tq, dk), lambda bi, h, qi: (bi, qi, q_off + h)),
        pl.BlockSpec((None, s, dk), lambda bi, h, qi: (bi, 0, k_off + h)),
        pl.BlockSpec((None, None, dv, s), lambda bi, h, qi: (bi, h, 0, 0)),
    ]
    args = [q_arr, k_arr, vt_arr]
    stat = pltpu.VMEM((n_chain, 1, cq), F32)
    scratch = [stat, pltpu.VMEM((n_chain, 2 * dv, cq), F32), stat]
    if bias is not None:
        in_specs.append(pl.BlockSpec((None, None, 1, s), lambda bi, h, qi: (bi, h, 0, 0)))
        args.append(bias)
        scratch.append(pltpu.VMEM((s, LANES), F32))
    return pl.pallas_call(
        functools.partial(_attn_kernel, tq=tq, cq=cq, tk=min(tk, tq), ahead=ahead, scale=scale,
                          has_bias=bias is not None),
        grid=(b, heads, s // tq),
        in_specs=in_specs,
        out_specs=pl.BlockSpec((None, tq, dv), lambda bi, h, qi: (bi, qi, h)),
        out_shape=jax.ShapeDtypeStruct((b, s, heads * dv), BF16),
        scratch_shapes=scratch,
        compiler_params=_params("arbitrary", "arbitrary", "arbitrary"),
        name="attn_bias" if bias is not None else "attn",
    )(*args)


def _dil_kernel(q_ref, k_ref, v_ref, kp_ref, vp_ref, o_ref, lse_ref, kbuf, vbuf, *, scale, group):
    planes, tl, _ = q_ref.shape
    nb = tl // BLOCK
    c2 = scale * LOG2E
    kbuf[:, :BLOCK, :] = kp_ref[...]
    kbuf[:, BLOCK:, :] = k_ref[...]
    vbuf[:, :BLOCK, :] = vp_ref[...]
    vbuf[:, BLOCK:, :] = v_ref[...]
    row = lax.broadcasted_iota(jnp.int32, (BLOCK, 2 * BLOCK), 0)
    col = lax.broadcasted_iota(jnp.int32, (BLOCK, 2 * BLOCK), 1)
    band = (col >= row) & (col <= row + BLOCK)
    lane = lax.broadcasted_iota(jnp.int32, (BLOCK, LANES), 1)
    ones = jnp.ones((2 * BLOCK, HEAD_DIM), BF16)
    heads = [slice(h * HEAD_DIM, (h + 1) * HEAD_DIM) for h in range(DIL_HEADS)]
    first_chunk = pl.program_id(2) == 0

    def locate(g, u):
        if nb % group == 0:
            gpp = nb // group
            p = g // gpp if planes > 1 else 0
            first = (g % gpp == 0) if u == 0 else None
            return p, (g % gpp) * group + u, first
        per = group // nb
        return g * per + u // nb, u % nb, (True if u % nb == 0 else None)

    def body(g, carry):
        work, masks = [], []
        for u in range(group):
            p, n, first = locate(g, u)
            start = n * BLOCK if isinstance(n, int) else pl.multiple_of(n * BLOCK, BLOCK)
            if first is None:
                masks.append(band)
            else:
                cond = first_chunk if first is True else jnp.logical_and(first_chunk, first)
                masks.append(band & (col >= jnp.where(cond, BLOCK, 0)))
            work += [(u, p, start, h) for h in range(DIL_HEADS)]

        def scores(u, p, start, h):
            q = q_ref[p, pl.ds(start, BLOCK), heads[h]]
            kc = kbuf[p, pl.ds(start, 2 * BLOCK), heads[h]]
            return lax.dot_general(q, kc, (((1,), (1,)), ((), ())), preferred_element_type=F32)

        ahead = 2 * DIL_HEADS
        pending = [scores(*w) for w in work[:ahead]]
        tiles = {}
        for i, (u, p, start, h) in enumerate(work):
            s = pending.pop(0)
            if i + ahead < len(work):
                pending.append(scores(*work[i + ahead]))
            s = jnp.where(masks[u], s, NEG)
            m = jnp.max(s, axis=-1, keepdims=True)
            e = jnp.exp2((s - m) * c2).astype(BF16)
            v_ext = jnp.concatenate([vbuf[p, pl.ds(start, 2 * BLOCK), heads[h]], ones], axis=1)
            pv = jnp.dot(e, v_ext, preferred_element_type=F32)
            denom = pv[:, HEAD_DIM:]
            o_ref[p, pl.ds(start, BLOCK), heads[h]] = pv[:, :HEAD_DIM] / denom
            tile = tiles.get(u, jnp.zeros((BLOCK, LANES), F32))
            tiles[u] = jnp.where(lane == h, m * scale + jnp.log(denom), tile)
            if h == DIL_HEADS - 1:
                lse_ref[p, pl.ds(start, BLOCK), :] = tiles.pop(u)
        return carry

    n_groups = planes * nb // group
    if n_groups == 1:
        body(0, 0)
    else:
        lax.fori_loop(0, n_groups, body, 0)


def _dil_branch(qkv, *, scale, rows=1024, group=8):
    b, dil, sub, w3 = qkv.shape
    w = w3 // 3
    tl = min(rows, sub)
    planes = min(rows // tl, dil)
    bpc = tl // BLOCK
    group = min(group, planes * bpc)
    assert (bpc % group == 0 or group % bpc == 0) and (planes * bpc) % group == 0
    own = lambda j: pl.BlockSpec((None, planes, tl, w), lambda bi, r, c: (bi, r, c, j))
    prev = lambda j: pl.BlockSpec((None, planes, BLOCK, w),
                                  lambda bi, r, c: (bi, r, jnp.maximum(c * bpc - 1, 0), j))
    return pl.pallas_call(
        functools.partial(_dil_kernel, scale=scale, group=group),
        grid=(b, dil // planes, sub // tl),
        in_specs=[own(0), own(1), own(2), prev(1), prev(2)],
        out_specs=[
            pl.BlockSpec((None, planes, tl, w), lambda bi, r, c: (bi, r, c, 0)),
            pl.BlockSpec((None, planes, tl, LANES), lambda bi, r, c: (bi, r, c, 0)),
        ],
        out_shape=[
            jax.ShapeDtypeStruct((b, dil, sub, w), F32),
            jax.ShapeDtypeStruct((b, dil, sub, LANES), F32),
        ],
        scratch_shapes=[pltpu.VMEM((planes, tl + BLOCK, w), BF16)] * 2,
        compiler_params=_params("parallel", "parallel", "parallel"),
        name=f"dilated_{dil}",
    )(qkv, qkv, qkv, qkv, qkv)


def _out_proj_kernel(x_ref, a_ref, b_ref, *rest, dils):
    nbr = len(dils)
    o_refs, l_refs = rest[:nbr], rest[nbr:2 * nbr]
    w_ref, y_ref = rest[2 * nbr:2 * nbr + 2]
    scr = rest[2 * nbr + 2:]
    tm = x_ref.shape[0]

    def natural(ref, dil, slot, j):
        sl = slice(j * LANES, (j + 1) * LANES)
        if dil == 1:
            return ref[0, :, sl]
        rows = tm // dil
        for r in range(dil):
            scr[slot][pl.ds(r, rows, stride=dil), :] = ref[r, :, sl]
        return scr[slot][...]

    ab = jnp.concatenate([a_ref[...], b_ref[...]], axis=-1)
    n_ab = ab.shape[1]
    y = x_ref[...] + jnp.dot(ab, w_ref[:n_ab, :], preferred_element_type=F32)

    lses = [natural(l_refs[i], dils[i], 2 * i, 0) for i in range(nbr)]
    m = functools.reduce(jnp.maximum, lses)
    es = [jnp.exp(l - m) for l in lses]
    tot = functools.reduce(lambda u, v: u + v, es)
    wts = [e / tot for e in es]
    parts = []
    for h in range(DIL_HEADS):
        c = None
        for i in range(nbr):
            term = wts[i][:, h:h + 1] * natural(o_refs[i], dils[i], 2 * i + 1, h)
            c = term if c is None else c + term
        parts.append(c.astype(BF16))
    mixed = jnp.concatenate(parts, axis=-1)
    y_ref[...] = y + jnp.dot(mixed, w_ref[n_ab:, :], preferred_element_type=F32)


def _out_proj(x, a, bm, os_, ls_, w, *, layer, tm=512):
    b, s, d = x.shape
    tm = min(tm, s)
    dils = tuple(o.shape[1] for o in os_)
    row = lambda arr: pl.BlockSpec((None, tm, arr.shape[2]), lambda bi, i: (bi, i, 0))
    plane = lambda arr: pl.BlockSpec((None, arr.shape[1], tm // arr.shape[1], arr.shape[3]),
                                     lambda bi, i: (bi, 0, i, 0))
    return pl.pallas_call(
        functools.partial(_out_proj_kernel, dils=dils),
        grid=(b, s // tm),
        in_specs=[row(x), row(a), row(bm)] + [plane(o) for o in os_] + [plane(l) for l in ls_]
                 + [_layer_block(w, layer)],
        out_specs=row(x),
        out_shape=jax.ShapeDtypeStruct((b, s, d), F32),
        scratch_shapes=[pltpu.VMEM((tm, LANES), F32)] * (2 * len(dils)),
        compiler_params=_params("parallel", "parallel"),
        name="out_proj",
    )(x, a, bm, *os_, *ls_, w)


def _rope_table(seq, dim, fill):
    half = dim // 2
    inv = 1.0 / (ROPE_THETA ** (jnp.arange(0, dim, 2, dtype=F32) / dim))
    ang = jnp.arange(seq, dtype=F32)[:, None] * inv[None, :]
    cos, sin = jnp.cos(ang), jnp.sin(ang)
    rest = LANES - dim
    zeros_h = jnp.zeros((seq, half), F32)
    zeros_r = jnp.zeros((seq, rest), F32)
    cos_t = jnp.concatenate([cos, cos, jnp.full((seq, rest), fill, F32)], axis=1)
    s_hi = jnp.concatenate([-sin, zeros_h, zeros_r], axis=1)
    s_lo = jnp.concatenate([zeros_h, sin, zeros_r], axis=1)
    return jnp.stack([cos_t, s_hi, s_lo], axis=0)


_FW = FOX_HEADS * HEAD_DIM
_O_FL = 3 * _FW
_O_CQ = _O_FL + FOX_HEADS
_O_KR = _O_CQ + MLA_Q_RANK + MLA_KV_RANK
_O_DQ = _O_KR + MLA_ROPE


def _split_w_in_kernel(w_ref, fqk_ref, fv_ref, fl_ref, c_ref, dil_ref):
    w = w_ref[...]
    lane = lax.broadcasted_iota(jnp.int32, (w.shape[0], LANES), 1)
    fqk_ref[...] = w[:, :2 * _FW].astype(BF16)
    fv_ref[...] = w[:, 2 * _FW:_O_FL].astype(BF16)
    fl_ref[...] = jnp.where(lane < FOX_HEADS, w[:, _O_FL:_O_FL + LANES], 0.0).astype(BF16)
    c_ref[:, :_O_KR - _O_CQ] = w[:, _O_CQ:_O_KR].astype(BF16)
    c_ref[:, _O_KR - _O_CQ:] = jnp.where(lane < MLA_ROPE, w[:, _O_KR:_O_KR + LANES], 0.0).astype(BF16)
    dil_ref[...] = w[:, _O_DQ:].astype(BF16)


def _split_w_in(w_in, *, rows=256):
    depth, d, n_in = w_in.shape
    n_c = _O_KR - _O_CQ + LANES
    widths = (2 * _FW, _FW, LANES, n_c, n_in - _O_DQ)
    return pl.pallas_call(
        _split_w_in_kernel,
        grid=(depth, d // rows),
        in_specs=[pl.BlockSpec((None, rows, n_in), lambda l, i: (l, i, 0))],
        out_specs=[pl.BlockSpec((None, rows, n), lambda l, i: (l, i, 0)) for n in widths],
        out_shape=[jax.ShapeDtypeStruct((depth, d, n), BF16) for n in widths],
        compiler_params=_params("parallel", "parallel"),
        name="split_w_in",
    )(w_in)


def kernel(x, ffn1_norm, ffn1_w_gate, ffn1_w_up, ffn1_w_down, mix_norm, w_in, fox_forget_bias, mla_q_norm, mla_kv_norm, mla_w_uq, mla_w_ukv, w_out, ffn2_norm, ffn2_w_gate, ffn2_w_up, ffn2_w_down, final_norm):
    b, s, d = x.shape
    depth = w_in.shape[0]
    t = b * s

    assert all(window == BLOCK * dil and s % (BLOCK * dil) == 0 for window, dil in DIL_BRANCHES)
    tab_mla = _rope_table(s, MLA_ROPE, 1.0)
    tab_dil = _rope_table(s, PARTIAL_ROPE, 1.0)

    pad_last = lambda w, n: jnp.pad(w, [(0, 0)] * (w.ndim - 1) + [(0, n - w.shape[-1])])
    w_fqk_all, w_fv_all, w_fl_all, w_c_all, w_dil_all = _split_w_in(w_in)
    fbias_all = pad_last(fox_forget_bias.reshape(depth, 1, FOX_HEADS), LANES)
    wuq_all = mla_w_uq.reshape(depth, MLA_Q_RANK, MLA_HEADS, MLA_NOPE + MLA_ROPE)
    wuq_all = pad_last(wuq_all, 2 * LANES).reshape(depth, MLA_Q_RANK, MLA_HEADS * 2 * LANES).astype(BF16)
    wukv = mla_w_ukv.reshape(depth, MLA_KV_RANK, MLA_HEADS, MLA_NOPE + HEAD_DIM)
    w_k_all = wukv[..., :MLA_NOPE].reshape(depth, MLA_KV_RANK, MLA_HEADS * MLA_NOPE).astype(BF16)
    w_v_all = wukv[..., MLA_NOPE:].reshape(depth, MLA_KV_RANK, MLA_HEADS * HEAD_DIM).astype(BF16)
    w_out_all = w_out.astype(BF16)

    ffn1_w = [w.astype(BF16) for w in (ffn1_w_gate, ffn1_w_up, ffn1_w_down)]
    ffn2_w = [w.astype(BF16) for w in (ffn2_w_gate, ffn2_w_up, ffn2_w_down)]

    gq_all = mla_q_norm.reshape(depth, 1, MLA_Q_RANK)
    gkv_all = mla_kv_norm.reshape(depth, 1, MLA_KV_RANK)

    xf = x.reshape(t, d)
    for l in range(depth):
        last = l == depth - 1

        xf, hn = _ffn(xf, ffn1_norm[l], *ffn1_w, mix_norm[l], layer=l, emit_norm=True)

        hn = hn.reshape(b, s, d)
        fox_scale = HEAD_DIM ** -0.5
        fqk, fvt, logf = _fox_proj(hn, w_fqk_all, w_fv_all, w_fl_all, fbias_all, layer=l)
        logf = logf[:, :, :FOX_HEADS].transpose(0, 2, 1).reshape(b * FOX_HEADS, s)
        key_bias = _key_bias(logf, fox_scale).reshape(b, FOX_HEADS, 1, s)
        out_a = _attention(fqk, fqk, fvt, key_bias, dk=HEAD_DIM, q_off=0, k_off=FOX_HEADS, scale=fox_scale)

        q_b, k_b, vt_b = _mla_proj(hn, w_c_all, gq_all, gkv_all, wuq_all, w_k_all, w_v_all, tab_mla, layer=l)
        out_b = _attention(q_b, k_b, vt_b, None, dk=2 * LANES, q_off=0, k_off=0,
                           scale=(MLA_NOPE + MLA_ROPE) ** -0.5)

        dqkvs = _dil_proj(hn, w_dil_all, tab_dil, layer=l, dils=tuple(dil for _, dil in DIL_BRANCHES))
        outs, lses = [], []
        for dqkv in dqkvs:
            o, lse = _dil_branch(dqkv, scale=HEAD_DIM ** -0.5)
            outs.append(o)
            lses.append(lse)

        xf = _out_proj(xf.reshape(b, s, d), out_a, out_b, outs, lses, w_out_all, layer=l).reshape(t, d)

        xf = _ffn(xf, ffn2_norm[l], *ffn2_w, final_norm if last else None, layer=l, final_norm=last)
    return xf.reshape(b, s, d)
```

```python
import functools

import jax
import jax.numpy as jnp
from jax import lax
from jax.experimental import pallas as pl
from jax.experimental.pallas import tpu as pltpu

F32 = jnp.float32
BF16 = jnp.bfloat16

LANES = 128
HEAD_DIM = 128
BLOCK = 128
EPS = 1e-6
ROPE_THETA = 500000.0
PARTIAL_ROPE = HEAD_DIM // 4
FOX_HEADS = 4
MLA_HEADS = 8
MLA_Q_RANK = 512
MLA_KV_RANK = 512
MLA_NOPE = 128
MLA_ROPE = 64
DIL_HEADS = 4
DIL_BRANCHES = ((128, 1), (512, 4), (2048, 16))
NEG = -1e30
LOG2E = 1.4426950408889634
EXP2_HEADROOM = 100.0
VMEM_LIMIT = 56 * 1024 * 1024
FFN_TILE = 512


def _params(*sem):
    return pltpu.CompilerParams(dimension_semantics=sem, vmem_limit_bytes=VMEM_LIMIT)


def _layer_block(stacked, layer):
    return pl.BlockSpec((None,) + stacked.shape[1:], lambda *_: (layer,) + (0,) * (stacked.ndim - 1),
                        pipeline_mode=pl.Buffered(1))


def _rms(x, g):
    return x * lax.rsqrt(jnp.mean(x * x, axis=-1, keepdims=True) + EPS) * g


def _apply_rope(x, cos, s_hi, s_lo, half):
    return x * cos + pltpu.roll(x, LANES - half, 1) * s_hi + pltpu.roll(x, half, 1) * s_lo


def _ffn_kernel(*refs, emit_norm, final_norm):
    x_hbm, g_ref, wg_ref, wu_ref, wd_ref = refs[:5]
    pos = 5
    g2_ref = None
    if emit_norm or final_norm:
        g2_ref = refs[pos]
        pos += 1
    o_ref = refs[pos]
    pos += 1
    hn_ref = None
    if emit_norm:
        hn_ref = refs[pos]
        pos += 1
    xn_ref, xbuf, sem = refs[pos:pos + 3]

    i, j = pl.program_id(0), pl.program_id(1)
    ni, nj = pl.num_programs(0), pl.num_programs(1)
    tm = xbuf.shape[0]

    def x_copy(tile):
        return pltpu.make_async_copy(x_hbm.at[pl.ds(pl.multiple_of(tile * tm, tm), tm), :], xbuf, sem)

    @pl.when(jnp.logical_and(i == 0, j == 0))
    def _():
        x_copy(0).start()

    def ff_tile():
        xn = xn_ref[...]
        gate = jnp.dot(xn, wg_ref[...], preferred_element_type=F32)
        up = jnp.dot(xn, wu_ref[...], preferred_element_type=F32)
        h = (gate * jax.nn.sigmoid(gate) * up * 0.5).astype(BF16)
        return jnp.dot(h, wd_ref[...], preferred_element_type=F32)

    @pl.when(j == 0)
    def _():
        x_copy(i).wait()
        xn_ref[...] = _rms(xbuf[...], g_ref[...]).astype(BF16)
        o_ref[...] = xbuf[...] + ff_tile()

    @pl.when(jnp.logical_and(j == 1, i + 1 < ni))
    def _():
        x_copy(i + 1).start()

    @pl.when(j > 0)
    def _():
        o_ref[...] += ff_tile()

    if emit_norm or final_norm:
        @pl.when(j == nj - 1)
        def _():
            y = _rms(o_ref[...], g2_ref[...])
            if emit_norm:
                hn_ref[...] = y.astype(BF16)
            else:
                o_ref[...] = y


def _ffn(x, g, wg, wu, wd, g2=None, *, layer, emit_norm=False, final_norm=False, tm=1024, tf=FFN_TILE):
    t, d = x.shape
    nj = wd.shape[1] // tf
    tm = min(tm, t)
    assert nj >= 2
    in_specs = [
        pl.BlockSpec(memory_space=pl.ANY),
        pl.BlockSpec((1, d), lambda i, j: (0, 0)),
        pl.BlockSpec((None, d, tf), lambda i, j: (layer, 0, j)),
        pl.BlockSpec((None, d, tf), lambda i, j: (layer, 0, j)),
        pl.BlockSpec((None, tf, d), lambda i, j: (layer, j, 0)),
    ]
    args = [x, g.reshape(1, d), wg, wu, wd]
    if emit_norm or final_norm:
        in_specs.append(pl.BlockSpec((1, d), lambda i, j: (0, 0)))
        args.append(g2.reshape(1, d))
    out_shape = [jax.ShapeDtypeStruct((t, d), F32)]
    out_specs = [pl.BlockSpec((tm, d), lambda i, j: (i, 0))]
    if emit_norm:
        out_shape.append(jax.ShapeDtypeStruct((t, d), BF16))
        out_specs.append(pl.BlockSpec((tm, d), lambda i, j: (i, 0)))
    res = pl.pallas_call(
        functools.partial(_ffn_kernel, emit_norm=emit_norm, final_norm=final_norm),
        grid=(t // tm, nj),
        in_specs=in_specs,
        out_specs=out_specs,
        out_shape=out_shape,
        scratch_shapes=[pltpu.VMEM((tm, d), BF16), pltpu.VMEM((tm, d), F32), pltpu.SemaphoreType.DMA],
        compiler_params=_params("arbitrary", "arbitrary"),
        name="ffn",
    )(*args)
    return res if emit_norm else res[0]


def _store_heads_transposed(vt_ref, v):
    for h in range(vt_ref.shape[0]):
        vt_ref[h] = v[:, h * HEAD_DIM:(h + 1) * HEAD_DIM].T.astype(BF16)


def _fox_proj_kernel(hn_ref, wqk_ref, wv_ref, wl_ref, b_ref, qk_ref, vt_ref, lf_ref):
    hn = hn_ref[...]
    v = jnp.dot(hn, wv_ref[...], preferred_element_type=F32)
    z = jnp.dot(hn, wl_ref[...], preferred_element_type=F32) + b_ref[...]
    qk = jnp.dot(hn, wqk_ref[...], preferred_element_type=F32)
    _store_heads_transposed(vt_ref, v)
    lf_ref[...] = jnp.minimum(z, 0.0) - jnp.log1p(jnp.exp(-jnp.abs(z)))
    qk_ref[...] = qk.astype(BF16)


def _fox_proj(hn, wqk, wv, wl, bias, *, layer, tm=1024):
    b, s, d = hn.shape
    tm = min(tm, s)
    n = wqk.shape[2]
    heads = wv.shape[2] // HEAD_DIM
    full = lambda a: _layer_block(a, layer)
    return pl.pallas_call(
        _fox_proj_kernel,
        grid=(b, s // tm),
        in_specs=[pl.BlockSpec((None, tm, d), lambda bi, i: (bi, i, 0)), full(wqk), full(wv), full(wl), full(bias)],
        out_specs=[
            pl.BlockSpec((None, tm, n), lambda bi, i: (bi, i, 0)),
            pl.BlockSpec((None, heads, HEAD_DIM, tm), lambda bi, i: (bi, 0, 0, i)),
            pl.BlockSpec((None, tm, LANES), lambda bi, i: (bi, i, 0)),
        ],
        out_shape=[
            jax.ShapeDtypeStruct((b, s, n), BF16),
            jax.ShapeDtypeStruct((b, heads, HEAD_DIM, s), BF16),
            jax.ShapeDtypeStruct((b, s, LANES), F32),
        ],
        compiler_params=_params("parallel", "parallel"),
        name="fox_proj",
    )(hn, wqk, wv, wl, bias)


def _mla_proj_kernel(hn_ref, wc_ref, gq_ref, gkv_ref, wuq_ref, wk_ref, wv_ref, tab_ref,
                     q_ref, k_ref, vt_ref):
    hn = hn_ref[...]
    o_kv, o_kr = MLA_Q_RANK, MLA_Q_RANK + MLA_KV_RANK
    cq = jnp.dot(hn, wc_ref[:, :o_kv], preferred_element_type=F32)
    ckv = jnp.dot(hn, wc_ref[:, o_kv:o_kr], preferred_element_type=F32)
    cqn = _rms(cq, gq_ref[...]).astype(BF16)
    kr_raw = jnp.dot(hn, wc_ref[:, o_kr:], preferred_element_type=F32)
    q = jnp.dot(cqn, wuq_ref[...], preferred_element_type=F32)
    ckvn = _rms(ckv, gkv_ref[...]).astype(BF16)
    cos, s_hi, s_lo = tab_ref[0], tab_ref[1], tab_ref[2]
    half = MLA_ROPE // 2
    kr = _apply_rope(kr_raw, cos, s_hi, s_lo, half).astype(BF16)
    v = jnp.dot(ckvn, wv_ref[...], preferred_element_type=F32)
    kn = jnp.dot(ckvn, wk_ref[...], preferred_element_type=F32)
    _store_heads_transposed(vt_ref, v)
    for h in range(MLA_HEADS):
        lo = 2 * h * LANES
        q_ref[:, lo:lo + LANES] = q[:, lo:lo + LANES].astype(BF16)
        q_ref[:, lo + LANES:lo + 2 * LANES] = _apply_rope(
            q[:, lo + LANES:lo + 2 * LANES], cos, s_hi, s_lo, half).astype(BF16)
        k_ref[:, lo:lo + LANES] = kn[:, h * LANES:(h + 1) * LANES].astype(BF16)
        k_ref[:, lo + LANES:lo + 2 * LANES] = kr


def _mla_proj(hn, wc, gq, gkv, wuq, wk, wv, tab, *, layer, tm=512):
    b, s, d = hn.shape
    tm = min(tm, s)
    nq = wuq.shape[2]
    heads = wv.shape[2] // HEAD_DIM
    full = lambda a: _layer_block(a, layer)
    return pl.pallas_call(
        _mla_proj_kernel,
        grid=(b, s // tm),
        in_specs=[
            pl.BlockSpec((None, tm, d), lambda bi, i: (bi, i, 0)),
            full(wc), full(gq), full(gkv), full(wuq), full(wk), full(wv),
            pl.BlockSpec((3, tm, LANES), lambda bi, i: (0, i, 0)),
        ],
        out_specs=[
            pl.BlockSpec((None, tm, nq), lambda bi, i: (bi, i, 0)),
            pl.BlockSpec((None, tm, nq), lambda bi, i: (bi, i, 0)),
            pl.BlockSpec((None, heads, HEAD_DIM, tm), lambda bi, i: (bi, 0, 0, i)),
        ],
        out_shape=[
            jax.ShapeDtypeStruct((b, s, nq), BF16),
            jax.ShapeDtypeStruct((b, s, nq), BF16),
            jax.ShapeDtypeStruct((b, heads, HEAD_DIM, s), BF16),
        ],
        compiler_params=_params("parallel", "parallel"),
        name="mla_proj",
    )(hn, wc, gq, gkv, wuq, wk, wv, tab)


def _dil_proj_kernel(hn_ref, w_ref, tab_ref, *rest, dils):
    out_refs, scr = rest[:len(dils)], rest[len(dils):]
    hn = hn_ref[...]
    cos, s_hi, s_lo = tab_ref[0], tab_ref[1], tab_ref[2]
    half = PARTIAL_ROPE // 2
    n_rot = 2 * DIL_HEADS
    tm = hn.shape[0]
    groups = [(0, DIL_HEADS), (DIL_HEADS, n_rot), (n_rot, n_rot + DIL_HEADS // 2),
              (n_rot + DIL_HEADS // 2, n_rot + DIL_HEADS)]

    def project(lo, hi):
        return jnp.dot(hn, w_ref[:, lo * LANES:hi * LANES], preferred_element_type=F32)

    def emit(lo, hi, y):
        for j in range(lo, hi):
            v = y[:, (j - lo) * LANES:(j - lo + 1) * LANES]
            scr[j][...] = _apply_rope(v, cos, s_hi, s_lo, half) if j < n_rot else v
            for o_ref, dil in zip(out_refs, dils):
                rows = tm // dil
                for r in range(dil):
                    o_ref[r, :, j * LANES:(j + 1) * LANES] = scr[j][pl.ds(r, rows, stride=dil), :].astype(BF16)

    ys = [project(*groups[0])]
    for g, (lo, hi) in enumerate(groups):
        if g + 1 < len(groups):
            ys.append(project(*groups[g + 1]))
        emit(lo, hi, ys[g])


def _dil_proj(hn, w, tab, *, layer, dils, tm=512):
    b, s, d = hn.shape
    tm = min(tm, s)
    n = w.shape[2]
    return pl.pallas_call(
        functools.partial(_dil_proj_kernel, dils=dils),
        grid=(b, s // tm),
        in_specs=[
            pl.BlockSpec((None, tm, d), lambda bi, i: (bi, i, 0)),
            _layer_block(w, layer),
            pl.BlockSpec((3, tm, LANES), lambda bi, i: (0, i, 0)),
        ],
        out_specs=[pl.BlockSpec((None, dil, tm // dil, n), lambda bi, i: (bi, 0, i, 0)) for dil in dils],
        out_shape=[jax.ShapeDtypeStruct((b, dil, s // dil, n), BF16) for dil in dils],
        scratch_shapes=[pltpu.VMEM((tm, LANES), F32)] * (n // LANES),
        compiler_params=_params("parallel", "parallel"),
        name="dil_proj",
    )(hn, w, tab)


def _cumsum_kernel(x_ref, o_ref, *, out_scale):
    x = x_ref[...]
    n = x.shape[-1]
    lane = lax.broadcasted_iota(jnp.int32, x.shape, 1)
    k = 1
    while k < n:
        x = x + jnp.where(lane >= k, pltpu.roll(x, k, 1), 0.0)
        k *= 2
    o_ref[...] = x * out_scale


def _key_bias(x, scale):
    return pl.pallas_call(
        functools.partial(_cumsum_kernel, out_scale=-1.0 / scale),
        out_shape=jax.ShapeDtypeStruct(x.shape, F32),
        compiler_params=pltpu.CompilerParams(vmem_limit_bytes=VMEM_LIMIT),
        name="forget_cumsum",
    )(x)


def _attn_kernel(*refs, tq, cq, tk, ahead, scale, has_bias):
    if has_bias:
        q_ref, k_ref, vt_ref, b_ref, o_ref, m_ref, acc_ref, worst_ref, brep_ref = refs
    else:
        q_ref, k_ref, vt_ref, o_ref, m_ref, acc_ref, worst_ref = refs
    qi = pl.program_id(2)
    n_chain = tq // cq
    dv = vt_ref.shape[0]
    c2 = scale * LOG2E

    if has_bias:
        @pl.when(qi == 0)
        def _():
            for j in range(brep_ref.shape[0] // tq):
                rows = jnp.broadcast_to(b_ref[:, j * tq:(j + 1) * tq], (LANES, tq))
                brep_ref[j * tq:(j + 1) * tq, :] = rows.T

    def scores(c, ks, size):
        q = q_ref[c * cq:(c + 1) * cq, :]
        k = k_ref[pl.ds(ks, size), :]
        return lax.dot_general(k, q, (((1,), (1,)), ((), ())), preferred_element_type=F32)

    def update(c, st, ks, size, exact):
        if has_bias:
            st = st + jnp.tile(brep_ref[pl.ds(ks, size), :], (1, cq // LANES))
        m_prev = m_ref[c]
        top = jnp.max(st, axis=0, keepdims=True)
        m_new = jnp.maximum(m_prev, top)
        alpha = jnp.exp2((m_prev - m_new) * c2)
        v_ext = jnp.concatenate([vt_ref[:, pl.ds(ks, size)], jnp.ones((dv, size), BF16)], axis=0)
        if exact:
            pt = jnp.exp2((st - m_new) * c2).astype(BF16)
            acc_ref[c] = alpha * acc_ref[c] + jnp.dot(v_ext, pt, preferred_element_type=F32)
        else:
            pt = jnp.exp2((st - m_prev) * c2).astype(BF16)
            acc_ref[c] = alpha * (acc_ref[c] + jnp.dot(v_ext, pt, preferred_element_type=F32))
            worst_ref[c] = jnp.maximum(worst_ref[c], (top - m_prev) * c2)
        m_ref[c] = m_new

    def run(work, ahead):
        pending = [scores(*w[:3]) for w in work[:ahead]]
        for i, (c, ks, size, exact) in enumerate(work):
            st = pending.pop(0)
            if i + ahead < len(work):
                pending.append(scores(*work[i + ahead][:3]))
            update(c, st, ks, size, exact)

    def diagonal_tile(exact):
        base = qi * tq
        look = min(4, n_chain)
        pending = [scores(c, pl.multiple_of(base + c * cq, cq), cq) for c in range(look)]
        for c in range(n_chain):
            ks = pl.multiple_of(base + c * cq, cq)
            st = pending.pop(0)
            if c + look < n_chain:
                pending.append(scores(c + look, pl.multiple_of(base + (c + look) * cq, cq), cq))
            if has_bias:
                st = st + jnp.tile(brep_ref[pl.ds(ks, cq), :], (1, cq // LANES))
            key = lax.broadcasted_iota(jnp.int32, st.shape, 0)
            qry = lax.broadcasted_iota(jnp.int32, st.shape, 1)
            st = jnp.where(key <= qry, st, NEG)
            top = jnp.max(st, axis=0, keepdims=True)
            pt = jnp.exp2((st - top) * c2).astype(BF16)
            v_ext = jnp.concatenate([vt_ref[:, pl.ds(ks, cq)], jnp.ones((dv, cq), BF16)], axis=0)
            m_ref[c] = top
            acc_ref[c] = jnp.dot(v_ext, pt, preferred_element_type=F32)
        n_piece = -(-((n_chain - 1) * cq) // tk)
        work = []
        for piece in range(n_piece):
            for c in range(1, n_chain):
                size = min(tk, c * cq - piece * tk)
                if size > 0:
                    work.append((c, pl.multiple_of(base + piece * tk, cq), size, exact))
        run(work, ahead)

    def attend(exact):
        diagonal_tile(exact)

        def body(j, carry):
            run([(c, pl.multiple_of(j * tq + part * tk, tk), tk, exact)
                 for part in range(tq // tk) for c in range(n_chain)], ahead)
            return carry

        lax.fori_loop(0, qi, body, 0)

    worst_ref[...] = jnp.zeros(worst_ref.shape, F32)
    attend(exact=False)

    @pl.when(jnp.max(worst_ref[...]) > EXP2_HEADROOM)
    def _():
        attend(exact=True)

    for c in range(n_chain):
        acc = acc_ref[c]
        o_ref[c * cq:(c + 1) * cq, :] = (acc[:dv] / acc[dv:]).T.astype(o_ref.dtype)


def _attention(q_arr, k_arr, vt_arr, bias, *, dk, q_off, k_off, scale, tq=2048, cq=256, tk=1024, ahead=2):
    b, s, _ = q_arr.shape
    heads, dv = vt_arr.shape[1], vt_arr.shape[2]
    tq = min(tq, s)
    cq = min(cq, tq)
    n_chain = tq // cq
    in_specs = [
        pl.BlockSpec((None, tq, dk), lambda bi, h, qi: (bi, qi, q_off + h)),
        pl.BlockSpec((None, s, dk), lambda bi, h, qi: (bi, 0, k_off + h)),
        pl.BlockSpec((None, None, dv, s), lambda bi, h, qi: (bi, h, 0, 0)),
    ]
    args = [q_arr, k_arr, vt_arr]
    stat = pltpu.VMEM((n_chain, 1, cq), F32)
    scratch = [stat, pltpu.VMEM((n_chain, 2 * dv, cq), F32), stat]
    if bias is not None:
        in_specs.append(pl.BlockSpec((None, None, 1, s), lambda bi, h, qi: (bi, h, 0, 0)))
        args.append(bias)
        scratch.append(pltpu.VMEM((s, LANES), F32))
    return pl.pallas_call(
        functools.partial(_attn_kernel, tq=tq, cq=cq, tk=min(tk, tq), ahead=ahead, scale=scale,
                          has_bias=bias is not None),
        grid=(b, heads, s // tq),
        in_specs=in_specs,
        out_specs=pl.BlockSpec((None, tq, dv), lambda bi, h, qi: (bi, qi, h)),
        out_shape=jax.ShapeDtypeStruct((b, s, heads * dv), BF16),
        scratch_shapes=scratch,
        compiler_params=_params("arbitrary", "arbitrary", "arbitrary"),
        name="attn_bias" if bias is not None else "attn",
    )(*args)


def _dil_kernel(q_ref, k_ref, v_ref, kp_ref, vp_ref, o_ref, lse_ref, kbuf, vbuf, *, scale, group):
    planes, tl, _ = q_ref.shape
    nb = tl // BLOCK
    c2 = scale * LOG2E
    kbuf[:, :BLOCK, :] = kp_ref[...]
    kbuf[:, BLOCK:, :] = k_ref[...]
    vbuf[:, :BLOCK, :] = vp_ref[...]
    vbuf[:, BLOCK:, :] = v_ref[...]
    row = lax.broadcasted_iota(jnp.int32, (BLOCK, 2 * BLOCK), 0)
    col = lax.broadcasted_iota(jnp.int32, (BLOCK, 2 * BLOCK), 1)
    band = (col >= row) & (col <= row + BLOCK)
    lane = lax.broadcasted_iota(jnp.int32, (BLOCK, LANES), 1)
    ones = jnp.ones((2 * BLOCK, HEAD_DIM), BF16)
    heads = [slice(h * HEAD_DIM, (h + 1) * HEAD_DIM) for h in range(DIL_HEADS)]
    first_chunk = pl.program_id(2) == 0

    def locate(g, u):
        if nb % group == 0:
            gpp = nb // group
            p = g // gpp if planes > 1 else 0
            first = (g % gpp == 0) if u == 0 else None
            return p, (g % gpp) * group + u, first
        per = group // nb
        return g * per + u // nb, u % nb, (True if u % nb == 0 else None)

    def body(g, carry):
        work, masks = [], []
        for u in range(group):
            p, n, first = locate(g, u)
            start = n * BLOCK if isinstance(n, int) else pl.multiple_of(n * BLOCK, BLOCK)
            if first is None:
                masks.append(band)
            else:
                cond = first_chunk if first is True else jnp.logical_and(first_chunk, first)
                masks.append(band & (col >= jnp.where(cond, BLOCK, 0)))
            work += [(u, p, start, h) for h in range(DIL_HEADS)]

        def scores(u, p, start, h):
            q = q_ref[p, pl.ds(start, BLOCK), heads[h]]
            kc = kbuf[p, pl.ds(start, 2 * BLOCK), heads[h]]
            return lax.dot_general(q, kc, (((1,), (1,)), ((), ())), preferred_element_type=F32)

        ahead = 2 * DIL_HEADS
        pending = [scores(*w) for w in work[:ahead]]
        tiles = {}
        for i, (u, p, start, h) in enumerate(work):
            s = pending.pop(0)
            if i + ahead < len(work):
                pending.append(scores(*work[i + ahead]))
            s = jnp.where(masks[u], s, NEG)
            m = jnp.max(s, axis=-1, keepdims=True)
            e = jnp.exp2((s - m) * c2).astype(BF16)
            v_ext = jnp.concatenate([vbuf[p, pl.ds(start, 2 * BLOCK), heads[h]], ones], axis=1)
            pv = jnp.dot(e, v_ext, preferred_element_type=F32)
            denom = pv[:, HEAD_DIM:]
            o_ref[p, pl.ds(start, BLOCK), heads[h]] = pv[:, :HEAD_DIM] / denom
            tile = tiles.get(u, jnp.zeros((BLOCK, LANES), F32))
            tiles[u] = jnp.where(lane == h, m * scale + jnp.log(denom), tile)
            if h == DIL_HEADS - 1:
                lse_ref[p, pl.ds(start, BLOCK), :] = tiles.pop(u)
        return carry

    n_groups = planes * nb // group
    if n_groups == 1:
        body(0, 0)
    else:
        lax.fori_loop(0, n_groups, body, 0)


def _dil_branch(qkv, *, scale, rows=1024, group=4):
    b, dil, sub, w3 = qkv.shape
    w = w3 // 3
    tl = min(rows, sub)
    planes = min(rows // tl, dil)
    bpc = tl // BLOCK
    group = min(group, planes * bpc)
    assert (bpc % group == 0 or group % bpc == 0) and (planes * bpc) % group == 0
    own = lambda j: pl.BlockSpec((None, planes, tl, w), lambda bi, r, c: (bi, r, c, j))
    prev = lambda j: pl.BlockSpec((None, planes, BLOCK, w),
                                  lambda bi, r, c: (bi, r, jnp.maximum(c * bpc - 1, 0), j))
    return pl.pallas_call(
        functools.partial(_dil_kernel, scale=scale, group=group),
        grid=(b, dil // planes, sub // tl),
        in_specs=[own(0), own(1), own(2), prev(1), prev(2)],
        out_specs=[
            pl.BlockSpec((None, planes, tl, w), lambda bi, r, c: (bi, r, c, 0)),
            pl.BlockSpec((None, planes, tl, LANES), lambda bi, r, c: (bi, r, c, 0)),
        ],
        out_shape=[
            jax.ShapeDtypeStruct((b, dil, sub, w), F32),
            jax.ShapeDtypeStruct((b, dil, sub, LANES), F32),
        ],
        scratch_shapes=[pltpu.VMEM((planes, tl + BLOCK, w), BF16)] * 2,
        compiler_params=_params("parallel", "parallel", "parallel"),
        name=f"dilated_{dil}",
    )(qkv, qkv, qkv, qkv, qkv)


def _out_proj_kernel(x_ref, a_ref, b_ref, *rest, dils):
    nbr = len(dils)
    o_refs, l_refs = rest[:nbr], rest[nbr:2 * nbr]
    w_ref, y_ref = rest[2 * nbr:2 * nbr + 2]
    scr = rest[2 * nbr + 2:]
    tm = x_ref.shape[0]

    def natural(ref, dil, slot, j):
        sl = slice(j * LANES, (j + 1) * LANES)
        if dil == 1:
            return ref[0, :, sl]
        rows = tm // dil
        for r in range(dil):
            scr[slot][pl.ds(r, rows, stride=dil), :] = ref[r, :, sl]
        return scr[slot][...]

    ab = jnp.concatenate([a_ref[...], b_ref[...]], axis=-1)
    n_ab = ab.shape[1]
    y = x_ref[...] + jnp.dot(ab, w_ref[:n_ab, :], preferred_element_type=F32)

    lses = [natural(l_refs[i], dils[i], 2 * i, 0) for i in range(nbr)]
    m = functools.reduce(jnp.maximum, lses)
    es = [jnp.exp(l - m) for l in lses]
    tot = functools.reduce(lambda u, v: u + v, es)
    wts = [e / tot for e in es]
    parts = []
    for h in range(DIL_HEADS):
        c = None
        for i in range(nbr):
            term = wts[i][:, h:h + 1] * natural(o_refs[i], dils[i], 2 * i + 1, h)
            c = term if c is None else c + term
        parts.append(c.astype(BF16))
    mixed = jnp.concatenate(parts, axis=-1)
    y_ref[...] = y + jnp.dot(mixed, w_ref[n_ab:, :], preferred_element_type=F32)


def _out_proj(x, a, bm, os_, ls_, w, *, layer, tm=512):
    b, s, d = x.shape
    tm = min(tm, s)
    dils = tuple(o.shape[1] for o in os_)
    row = lambda arr: pl.BlockSpec((None, tm, arr.shape[2]), lambda bi, i: (bi, i, 0))
    plane = lambda arr: pl.BlockSpec((None, arr.shape[1], tm // arr.shape[1], arr.shape[3]),
                                     lambda bi, i: (bi, 0, i, 0))
    return pl.pallas_call(
        functools.partial(_out_proj_kernel, dils=dils),
        grid=(b, s // tm),
        in_specs=[row(x), row(a), row(bm)] + [plane(o) for o in os_] + [plane(l) for l in ls_]
                 + [_layer_block(w, layer)],
        out_specs=row(x),
        out_shape=jax.ShapeDtypeStruct((b, s, d), F32),
        scratch_shapes=[pltpu.VMEM((tm, LANES), F32)] * (2 * len(dils)),
        compiler_params=_params("parallel", "parallel"),
        name="out_proj",
    )(x, a, bm, *os_, *ls_, w)


def _rope_table(seq, dim, fill):
    half = dim // 2
    inv = 1.0 / (ROPE_THETA ** (jnp.arange(0, dim, 2, dtype=F32) / dim))
    ang = jnp.arange(seq, dtype=F32)[:, None] * inv[None, :]
    cos, sin = jnp.cos(ang), jnp.sin(ang)
    rest = LANES - dim
    zeros_h = jnp.zeros((seq, half), F32)
    zeros_r = jnp.zeros((seq, rest), F32)
    cos_t = jnp.concatenate([cos, cos, jnp.full((seq, rest), fill, F32)], axis=1)
    s_hi = jnp.concatenate([-sin, zeros_h, zeros_r], axis=1)
    s_lo = jnp.concatenate([zeros_h, sin, zeros_r], axis=1)
    return jnp.stack([cos_t, s_hi, s_lo], axis=0)


_FW = FOX_HEADS * HEAD_DIM
_O_FL = 3 * _FW
_O_CQ = _O_FL + FOX_HEADS
_O_KR = _O_CQ + MLA_Q_RANK + MLA_KV_RANK
_O_DQ = _O_KR + MLA_ROPE


def _split_w_in_kernel(w_ref, fqk_ref, fv_ref, fl_ref, c_ref, dil_ref):
    w = w_ref[...]
    lane = lax.broadcasted_iota(jnp.int32, (w.shape[0], LANES), 1)
    fqk_ref[...] = w[:, :2 * _FW].astype(BF16)
    fv_ref[...] = w[:, 2 * _FW:_O_FL].astype(BF16)
    fl_ref[...] = jnp.where(lane < FOX_HEADS, w[:, _O_FL:_O_FL + LANES], 0.0).astype(BF16)
    c_ref[:, :_O_KR - _O_CQ] = w[:, _O_CQ:_O_KR].astype(BF16)
    c_ref[:, _O_KR - _O_CQ:] = jnp.where(lane < MLA_ROPE, w[:, _O_KR:_O_KR + LANES], 0.0).astype(BF16)
    dil_ref[...] = w[:, _O_DQ:].astype(BF16)


def _split_w_in(w_in, *, rows=256):
    depth, d, n_in = w_in.shape
    n_c = _O_KR - _O_CQ + LANES
    widths = (2 * _FW, _FW, LANES, n_c, n_in - _O_DQ)
    return pl.pallas_call(
        _split_w_in_kernel,
        grid=(depth, d // rows),
        in_specs=[pl.BlockSpec((None, rows, n_in), lambda l, i: (l, i, 0))],
        out_specs=[pl.BlockSpec((None, rows, n), lambda l, i: (l, i, 0)) for n in widths],
        out_shape=[jax.ShapeDtypeStruct((depth, d, n), BF16) for n in widths],
        compiler_params=_params("parallel", "parallel"),
        name="split_w_in",
    )(w_in)


def kernel(x, ffn1_norm, ffn1_w_gate, ffn1_w_up, ffn1_w_down, mix_norm, w_in, fox_forget_bias, mla_q_norm, mla_kv_norm, mla_w_uq, mla_w_ukv, w_out, ffn2_norm, ffn2_w_gate, ffn2_w_up, ffn2_w_down, final_norm):
    b, s, d = x.shape
    depth = w_in.shape[0]
    t = b * s

    assert all(window == BLOCK * dil and s % (BLOCK * dil) == 0 for window, dil in DIL_BRANCHES)
    tab_mla = _rope_table(s, MLA_ROPE, 1.0)
    tab_dil = _rope_table(s, PARTIAL_ROPE, 1.0)

    pad_last = lambda w, n: jnp.pad(w, [(0, 0)] * (w.ndim - 1) + [(0, n - w.shape[-1])])
    w_fqk_all, w_fv_all, w_fl_all, w_c_all, w_dil_all = _split_w_in(w_in)
    fbias_all = pad_last(fox_forget_bias.reshape(depth, 1, FOX_HEADS), LANES)
    wuq_all = mla_w_uq.reshape(depth, MLA_Q_RANK, MLA_HEADS, MLA_NOPE + MLA_ROPE)
    wuq_all = pad_last(wuq_all, 2 * LANES).reshape(depth, MLA_Q_RANK, MLA_HEADS * 2 * LANES).astype(BF16)
    wukv = mla_w_ukv.reshape(depth, MLA_KV_RANK, MLA_HEADS, MLA_NOPE + HEAD_DIM)
    w_k_all = wukv[..., :MLA_NOPE].reshape(depth, MLA_KV_RANK, MLA_HEADS * MLA_NOPE).astype(BF16)
    w_v_all = wukv[..., MLA_NOPE:].reshape(depth, MLA_KV_RANK, MLA_HEADS * HEAD_DIM).astype(BF16)
    w_out_all = w_out.astype(BF16)

    ffn1_w = [w.astype(BF16) for w in (ffn1_w_gate, ffn1_w_up, ffn1_w_down)]
    ffn2_w = [w.astype(BF16) for w in (ffn2_w_gate, ffn2_w_up, ffn2_w_down)]

    gq_all = mla_q_norm.reshape(depth, 1, MLA_Q_RANK)
    gkv_all = mla_kv_norm.reshape(depth, 1, MLA_KV_RANK)

    xf = x.reshape(t, d)
    for l in range(depth):
        last = l == depth - 1

        xf, hn = _ffn(xf, ffn1_norm[l], *ffn1_w, mix_norm[l], layer=l, emit_norm=True)

        hn = hn.reshape(b, s, d)
        fox_scale = HEAD_DIM ** -0.5
        fqk, fvt, logf = _fox_proj(hn, w_fqk_all, w_fv_all, w_fl_all, fbias_all, layer=l)
        logf = logf[:, :, :FOX_HEADS].transpose(0, 2, 1).reshape(b * FOX_HEADS, s)
        key_bias = _key_bias(logf, fox_scale).reshape(b, FOX_HEADS, 1, s)
        out_a = _attention(fqk, fqk, fvt, key_bias, dk=HEAD_DIM, q_off=0, k_off=FOX_HEADS, scale=fox_scale)

        q_b, k_b, vt_b = _mla_proj(hn, w_c_all, gq_all, gkv_all, wuq_all, w_k_all, w_v_all, tab_mla, layer=l)
        out_b = _attention(q_b, k_b, vt_b, None, dk=2 * LANES, q_off=0, k_off=0,
                           scale=(MLA_NOPE + MLA_ROPE) ** -0.5)

        dqkvs = _dil_proj(hn, w_dil_all, tab_dil, layer=l, dils=tuple(dil for _, dil in DIL_BRANCHES))
        outs, lses = [], []
        for dqkv in dqkvs:
            o, lse = _dil_branch(dqkv, scale=HEAD_DIM ** -0.5)
            outs.append(o)
            lses.append(lse)

        xf = _out_proj(xf.reshape(b, s, d), out_a, out_b, outs, lses, w_out_all, layer=l).reshape(t, d)

        xf = _ffn(xf, ffn2_norm[l], *ffn2_w, final_norm if last else None, layer=l, final_norm=last)
    return xf.reshape(b, s, d)
```
